```python
import math
import jax, jax.numpy as jnp
from jax import lax
import numpy as np

D_MODEL = 1024
BATCH = 8
SEQ = 2048
DEPTH = 4
DEC_BATCH = 128
DEC_SEQ = 4
PAST_LEN = 16384
PAGE_SIZE = 128

N_MIXERS = 2
N_LRU = (DEPTH + 1) // 2
N_GDN = DEPTH // 2
CONV_W = 4
LRU_WIDTH = D_MODEL
LRU_BLOCKS = 4
LRU_BW = LRU_WIDTH // LRU_BLOCKS
LRU_C = 8.0
GDN_HEADS = 8
GDN_DK = 128
GDN_DV = 128
GDN_QK = GDN_HEADS * GDN_DK
GDN_VW = GDN_HEADS * GDN_DV
GDN_CONV_DIM = 2 * GDN_QK + GDN_VW
GDN_IN_DIM = GDN_CONV_DIM + GDN_VW + 2 * GDN_HEADS
GDN_CHUNK = 64
MEM_LEN = 256
XA_HEADS = 4
XA_HD = D_MODEL // XA_HEADS
D_FF = 3 * D_MODEL
N_EXPERTS = 8
TOP_K = 2
D_FF_EXPERT = D_FF // 2
DN_ALPHA = (2 * DEPTH) ** 0.25
DN_BETA = (8 * DEPTH) ** -0.25
LN_EPS = 1e-5
RMS_EPS = 1e-6

kernel_name = "hybrid_rglru_gdn_memxattn_decode_step"


def _layer_norm(x, g, b):
    xf = x.astype(jnp.float32)
    mu = jnp.mean(xf, axis=-1, keepdims=True)
    var = jnp.mean(jnp.square(xf - mu), axis=-1, keepdims=True)
    return ((xf - mu) * lax.rsqrt(var + LN_EPS) * g + b).astype(x.dtype)


def _causal_conv(x, buf, w, b=None):
    T = x.shape[1]
    xp = jnp.concatenate([buf.astype(x.dtype), x], axis=1)
    out = xp[:, 0:T] * w[0]
    for j in range(1, CONV_W):
        out = out + xp[:, j:j + T] * w[j]
    if b is not None:
        out = out + b
    return out, xp[:, T:]


def _rglru_mixer(x, h0, conv0, w_in, conv_w, conv_b, gate_w, gate_b, lam, w_out):
    B, T, _ = x.shape
    gu = x @ w_in
    gate_branch, rec = gu[..., :LRU_WIDTH], gu[..., LRU_WIDTH:]
    rec, conv_new = _causal_conv(rec, conv0, conv_w, conv_b)
    xb = rec.reshape(B, T, LRU_BLOCKS, LRU_BW)
    gates = (jnp.einsum('btnc,ncg->btng', xb, gate_w) + gate_b).astype(jnp.float32)
    r = jax.nn.sigmoid(gates[..., :LRU_BW]).reshape(B, T, LRU_WIDTH)
    i = jax.nn.sigmoid(gates[..., LRU_BW:]).reshape(B, T, LRU_WIDTH)
    log_a = -LRU_C * r * jax.nn.softplus(-lam.astype(jnp.float32))
    a = jnp.exp(log_a)
    mult = jnp.sqrt(-jnp.expm1(2.0 * log_a))
    bterm = mult * i * rec.astype(jnp.float32)
    bterm = bterm.at[:, 0].add(a[:, 0] * h0.astype(jnp.float32))

    def comb(left, right):
        a1, b1 = left
        a2, b2 = right
        return a1 * a2, a2 * b1 + b2

    _, h = lax.associative_scan(comb, (a, bterm), axis=1)
    y = h.astype(x.dtype) * jax.nn.gelu(gate_branch)
    return y @ w_out, h[:, -1].astype(h0.dtype), conv_new.astype(conv0.dtype)


def _l2norm(t):
    return t * lax.rsqrt(jnp.sum(t * t, axis=-1, keepdims=True) + RMS_EPS)


def _chunk_gated_delta(q, k, v, g, beta, s0):
    B, T, H, _ = q.shape
    C = min(GDN_CHUNK, T)
    pad = (-T) % C
    def pad_t(t):
        return jnp.pad(t, [(0, 0), (0, pad)] + [(0, 0)] * (t.ndim - 2))
    q, k, v, g, beta = pad_t(q), pad_t(k), pad_t(v), pad_t(g), pad_t(beta)
    N = (T + pad) // C
    q = q * (GDN_DK ** -0.5)
    def chunks4(t):
        return t.reshape(B, N, C, H, t.shape[-1]).transpose(1, 0, 3, 2, 4)
    def chunks3(t):
        return t.reshape(B, N, C, H).transpose(1, 0, 3, 2)
    q, k, v = chunks4(q), chunks4(k), chunks4(v)
    g, beta = chunks3(g), chunks3(beta)
    gc = jnp.cumsum(g, axis=-1)
    lower_incl = jnp.tril(jnp.ones((C, C), dtype=bool))
    strict_lower = jnp.tril(jnp.ones((C, C), dtype=bool), -1)
    decay = jnp.exp(jnp.where(lower_incl, gc[..., :, None] - gc[..., None, :], -jnp.inf))
    kb = k * beta[..., None]
    vb = v * beta[..., None]
    A = jnp.where(strict_lower, jnp.einsum('nbhid,nbhjd->nbhij', kb, k) * decay, 0.0)
    eye = jnp.eye(C, dtype=jnp.float32)
    Tinv = lax.linalg.triangular_solve(A + eye, jnp.broadcast_to(eye, A.shape),
                                       left_side=True, lower=True, unit_diagonal=True)
    u = Tinv @ vb
    w = Tinv @ (kb * jnp.exp(gc)[..., None])
    attn = jnp.where(lower_incl, jnp.einsum('nbhid,nbhjd->nbhij', q, k) * decay, 0.0)

    def step(S, inp):
        q_i, k_i, u_i, w_i, gc_i, attn_i = inp
        v_new = u_i - w_i @ S
        o = (q_i * jnp.exp(gc_i)[..., None]) @ S + attn_i @ v_new
        g_last = gc_i[..., -1]
        S = S * jnp.exp(g_last)[..., None, None] + jnp.einsum(
            'bhcd,bhce->bhde', k_i * jnp.exp(g_last[..., None] - gc_i)[..., None], v_new)
        return S, o

    S_fin, o = lax.scan(step, s0.astype(jnp.float32), (q, k, u, w, gc, attn))
    o = o.transpose(1, 0, 3, 2, 4).reshape(B, N * C, H, GDN_DV)[:, :T]
    return o, S_fin


def _gdn_mixer(x, s0, conv0, w_in, conv_w, a_log, dt_bias, norm_g, w_out):
    B, T, _ = x.shape
    proj = x @ w_in
    qkv = proj[..., :GDN_CONV_DIM]
    z = proj[..., GDN_CONV_DIM:GDN_CONV_DIM + GDN_VW]
    b_in = proj[..., GDN_CONV_DIM + GDN_VW:GDN_CONV_DIM + GDN_VW + GDN_HEADS]
    a_in = proj[..., GDN_CONV_DIM + GDN_VW + GDN_HEADS:]
    qkv, conv_new = _causal_conv(qkv, conv0, conv_w)
    qkv = jax.nn.silu(qkv).astype(jnp.float32)
    q = _l2norm(qkv[..., :GDN_QK].reshape(B, T, GDN_HEADS, GDN_DK))
    k = _l2norm(qkv[..., GDN_QK:2 * GDN_QK].reshape(B, T, GDN_HEADS, GDN_DK))
    v = qkv[..., 2 * GDN_QK:].reshape(B, T, GDN_HEADS, GDN_DV)
    beta = jax.nn.sigmoid(b_in.astype(jnp.float32))
    g = -jnp.exp(a_log.astype(jnp.float32)) * jax.nn.softplus(
        a_in.astype(jnp.float32) + dt_bias.astype(jnp.float32))
    o, s_new = _chunk_gated_delta(q, k, v, g, beta, s0)
    o = o * lax.rsqrt(jnp.mean(o * o, axis=-1, keepdims=True) + RMS_EPS) * norm_g
    o = o * jax.nn.silu(z.astype(jnp.float32).reshape(B, T, GDN_HEADS, GDN_DV))
    y = o.reshape(B, T, GDN_VW).astype(x.dtype) @ w_out
    return y, s_new.astype(s0.dtype), conv_new.astype(conv0.dtype)


def _mem_attention(x, mk, mv, wq, wo):
    B, T, _ = x.shape
    q = (x @ wq).reshape(B, T, XA_HEADS, XA_HD)
    s = jnp.einsum('bthd,bmhd->bhtm', q, mk.astype(x.dtype)).astype(jnp.float32) * (XA_HD ** -0.5)
    p = jax.nn.softmax(s, axis=-1).astype(x.dtype)
    o = jnp.einsum('bhtm,bmhd->bthd', p, mv.astype(x.dtype)).reshape(B, T, XA_HEADS * XA_HD)
    return o @ wo


def _swiglu(x, w_gu, w_down, f):
    gu = x @ w_gu
    return (jax.nn.silu(gu[..., :f]) * gu[..., f:]) @ w_down


def _moe_ffn(x, router, w_gu, w_down):
    logits = (x @ router).astype(jnp.float32)
    top_v, top_i = lax.top_k(logits, TOP_K)
    wts = jax.nn.softmax(top_v, axis=-1)
    gate = jnp.sum(jax.nn.one_hot(top_i, N_EXPERTS, dtype=jnp.float32) * wts[..., None], axis=-2)
    gate = gate.astype(x.dtype)
    y = jnp.zeros_like(x)
    for e in range(N_EXPERTS):
        y = y + gate[..., e:e + 1] * _swiglu(x, w_gu[e], w_down[e], D_FF_EXPERT)
    return y


def _trunk(x, mem_k, mem_v, lru_h, lru_conv, gdn_s, gdn_conv, p):
    new_h, new_lc, new_s, new_gc = [], [], [], []
    for i in range(DEPTH):
        j = i // N_MIXERS
        if i % N_MIXERS == 0:
            mix, h, c = _rglru_mixer(x, lru_h[j], lru_conv[j], p['lru_w_in'][j], p['lru_conv_w'][j],
                                     p['lru_conv_b'][j], p['lru_gate_w'][j], p['lru_gate_b'][j],
                                     p['lru_lambda'][j], p['lru_w_out'][j])
            new_h.append(h)
            new_lc.append(c)
        else:
            mix, s, c = _gdn_mixer(x, gdn_s[j], gdn_conv[j], p['gdn_w_in'][j], p['gdn_conv_w'][j],
                                   p['gdn_a_log'][j], p['gdn_dt_bias'][j], p['gdn_norm_g'][j],
                                   p['gdn_w_out'][j])
            new_s.append(s)
            new_gc.append(c)
        x = _layer_norm(DN_ALPHA * x + mix, p['ln_g'][i, 0], p['ln_b'][i, 0])
        xa = _mem_attention(x, mem_k[i], mem_v[i], p['xa_wq'][i], p['xa_wo'][i])
        x = _layer_norm(DN_ALPHA * x + xa, p['ln_g'][i, 1], p['ln_b'][i, 1])
        if i % 2 == 0:
            ff = _swiglu(x, p['ffn_w_gu'][j], p['ffn_w_down'][j], D_FF)
        else:
            ff = _moe_ffn(x, p['moe_router'][j], p['moe_w_gu'][j], p['moe_w_down'][j])
        x = _layer_norm(DN_ALPHA * x + ff, p['ln_g'][i, 2], p['ln_b'][i, 2])
    return x, jnp.stack(new_h), jnp.stack(new_lc), jnp.stack(new_s), jnp.stack(new_gc)


def setup_inputs(seed: int = 0) -> dict:
    key = jax.random.key(seed)
    ks = jax.random.split(key, 40)
    f32 = jnp.float32
    def nrm(k, shape, scale):
        return jax.random.normal(k, shape, f32) * scale

    x_prompt = nrm(ks[0], (BATCH, SEQ, D_MODEL), 1.0)
    x_sample = nrm(ks[1], (DEC_BATCH, DEC_SEQ, D_MODEL), 1.0)
    mem_prompt = nrm(ks[2], (BATCH, MEM_LEN, D_MODEL), 1.0)
    state_lru_h = nrm(ks[3], (N_LRU, DEC_BATCH, LRU_WIDTH), 0.5)
    state_lru_conv = nrm(ks[4], (N_LRU, DEC_BATCH, CONV_W - 1, LRU_WIDTH), 1.0)
    state_gdn_s = nrm(ks[5], (N_GDN, DEC_BATCH, GDN_HEADS, GDN_DK, GDN_DV), 0.05)
    state_gdn_conv = nrm(ks[6], (N_GDN, DEC_BATCH, CONV_W - 1, GDN_CONV_DIM), 1.0)
    cache_mem_k = nrm(ks[7], (DEPTH, DEC_BATCH, MEM_LEN, XA_HEADS, XA_HD), 1.0)
    cache_mem_v = nrm(ks[8], (DEPTH, DEC_BATCH, MEM_LEN, XA_HEADS, XA_HD), 1.0)

    ln_g = 1.0 + nrm(ks[9], (DEPTH, 3, D_MODEL), 0.02)
    ln_b = nrm(ks[10], (DEPTH, 3, D_MODEL), 0.02)
    xa_w = XA_HEADS * XA_HD
    xa_wq = nrm(ks[11], (DEPTH, D_MODEL, xa_w), D_MODEL ** -0.5)
    xa_wk = nrm(ks[12], (DEPTH, D_MODEL, xa_w), D_MODEL ** -0.5)
    xa_wv = nrm(ks[13], (DEPTH, D_MODEL, xa_w), DN_BETA * D_MODEL ** -0.5)
    xa_wo = nrm(ks[14], (DEPTH, xa_w, D_MODEL), DN_BETA * xa_w ** -0.5)

    lru_w_in = nrm(ks[15], (N_LRU, D_MODEL, 2 * LRU_WIDTH), D_MODEL ** -0.5)
    lru_conv_w = nrm(ks[16], (N_LRU, CONV_W, LRU_WIDTH), CONV_W ** -0.5)
    lru_conv_b = nrm(ks[17], (N_LRU, LRU_WIDTH), 0.01)
    lru_gate_w = nrm(ks[18], (N_LRU, LRU_BLOCKS, LRU_BW, 2 * LRU_BW), LRU_BW ** -0.5)
    lru_gate_b = nrm(ks[19], (N_LRU, LRU_BLOCKS, 2 * LRU_BW), 0.01)
    a_target = jax.random.uniform(ks[20], (N_LRU, LRU_WIDTH), f32, 0.9, 0.999)
    s_root = a_target ** (1.0 / LRU_C)
    lru_lambda = jnp.log(s_root) - jnp.log1p(-s_root)
    lru_w_out = nrm(ks[21], (N_LRU, LRU_WIDTH, D_MODEL), DN_BETA * LRU_WIDTH ** -0.5)

    gdn_w_in = nrm(ks[22], (N_GDN, D_MODEL, GDN_IN_DIM), D_MODEL ** -0.5)
    gdn_conv_w = nrm(ks[23], (N_GDN, CONV_W, GDN_CONV_DIM), CONV_W ** -0.5)
    gdn_a_log = jnp.log(jax.random.uniform(ks[24], (N_GDN, GDN_HEADS), f32, 1.0, 16.0))
    dt = jnp.exp(jax.random.uniform(ks[25], (N_GDN, GDN_HEADS), f32, math.log(1e-3), math.log(1e-1)))
    gdn_dt_bias = dt + jnp.log(-jnp.expm1(-dt))
    gdn_norm_g = 1.0 + nrm(ks[26], (N_GDN, GDN_DV), 0.02)
    gdn_w_out = nrm(ks[27], (N_GDN, GDN_VW, D_MODEL), DN_BETA * GDN_VW ** -0.5)

    ffn_w_gu = nrm(ks[28], (N_LRU, D_MODEL, 2 * D_FF), D_MODEL ** -0.5)
    ffn_w_down = nrm(ks[29], (N_LRU, D_FF, D_MODEL), DN_BETA * D_FF ** -0.5)
    moe_router = nrm(ks[30], (N_GDN, D_MODEL, N_EXPERTS), D_MODEL ** -0.5)
    moe_w_gu = nrm(ks[31], (N_GDN, N_EXPERTS, D_MODEL, 2 * D_FF_EXPERT), D_MODEL ** -0.5)
    moe_w_down = nrm(ks[32], (N_GDN, N_EXPERTS, D_FF_EXPERT, D_MODEL), DN_BETA * D_FF_EXPERT ** -0.5)

    return {"x_prompt": x_prompt, "x_sample": x_sample, "mem_prompt": mem_prompt,
            "state_lru_h": state_lru_h, "state_lru_conv": state_lru_conv,
            "state_gdn_s": state_gdn_s, "state_gdn_conv": state_gdn_conv,
            "cache_mem_k": cache_mem_k, "cache_mem_v": cache_mem_v,
            "ln_g": ln_g, "ln_b": ln_b,
            "xa_wq": xa_wq, "xa_wk": xa_wk, "xa_wv": xa_wv, "xa_wo": xa_wo,
            "lru_w_in": lru_w_in, "lru_conv_w": lru_conv_w, "lru_conv_b": lru_conv_b,
            "lru_gate_w": lru_gate_w, "lru_gate_b": lru_gate_b, "lru_lambda": lru_lambda,
            "lru_w_out": lru_w_out,
            "gdn_w_in": gdn_w_in, "gdn_conv_w": gdn_conv_w, "gdn_a_log": gdn_a_log,
            "gdn_dt_bias": gdn_dt_bias, "gdn_norm_g": gdn_norm_g, "gdn_w_out": gdn_w_out,
            "ffn_w_gu": ffn_w_gu, "ffn_w_down": ffn_w_down,
            "moe_router": moe_router, "moe_w_gu": moe_w_gu, "moe_w_down": moe_w_down}


def reference(x_prompt, x_sample, mem_prompt, state_lru_h, state_lru_conv, state_gdn_s, state_gdn_conv,
              cache_mem_k, cache_mem_v, ln_g, ln_b, xa_wq, xa_wk, xa_wv, xa_wo,
              lru_w_in, lru_conv_w, lru_conv_b, lru_gate_w, lru_gate_b, lru_lambda, lru_w_out,
              gdn_w_in, gdn_conv_w, gdn_a_log, gdn_dt_bias, gdn_norm_g, gdn_w_out,
              ffn_w_gu, ffn_w_down, moe_router, moe_w_gu, moe_w_down):
    p = dict(ln_g=ln_g, ln_b=ln_b, xa_wq=xa_wq, xa_wo=xa_wo,
             lru_w_in=lru_w_in, lru_conv_w=lru_conv_w, lru_conv_b=lru_conv_b, lru_gate_w=lru_gate_w,
             lru_gate_b=lru_gate_b, lru_lambda=lru_lambda, lru_w_out=lru_w_out,
             gdn_w_in=gdn_w_in, gdn_conv_w=gdn_conv_w, gdn_a_log=gdn_a_log, gdn_dt_bias=gdn_dt_bias,
             gdn_norm_g=gdn_norm_g, gdn_w_out=gdn_w_out,
             ffn_w_gu=ffn_w_gu, ffn_w_down=ffn_w_down,
             moe_router=moe_router, moe_w_gu=moe_w_gu, moe_w_down=moe_w_down)
    B = x_prompt.shape[0]
    mem_k_p = jnp.einsum('bmd,lde->lbme', mem_prompt, xa_wk).reshape(DEPTH, B, MEM_LEN, XA_HEADS, XA_HD)
    mem_v_p = jnp.einsum('bmd,lde->lbme', mem_prompt, xa_wv).reshape(DEPTH, B, MEM_LEN, XA_HEADS, XA_HD)
    h0 = jnp.zeros((N_LRU, B, LRU_WIDTH), state_lru_h.dtype)
    lc0 = jnp.zeros((N_LRU, B, CONV_W - 1, LRU_WIDTH), state_lru_conv.dtype)
    s0 = jnp.zeros((N_GDN, B, GDN_HEADS, GDN_DK, GDN_DV), state_gdn_s.dtype)
    gc0 = jnp.zeros((N_GDN, B, CONV_W - 1, GDN_CONV_DIM), state_gdn_conv.dtype)
    y_prompt, lru_h_p, lru_conv_p, gdn_s_p, gdn_conv_p = _trunk(
        x_prompt, mem_k_p, mem_v_p, h0, lc0, s0, gc0, p)
    y_sample, lru_h_s, lru_conv_s, gdn_s_s, gdn_conv_s = _trunk(
        x_sample, cache_mem_k, cache_mem_v, state_lru_h, state_lru_conv, state_gdn_s, state_gdn_conv, p)
    return (y_prompt, y_sample, lru_h_p, lru_conv_p, gdn_s_p, gdn_conv_p, mem_k_p, mem_v_p,
            lru_h_s, lru_conv_s, gdn_s_s, gdn_conv_s)
```

```python
import functools
import math

import jax
import jax.numpy as jnp
from jax import lax
from jax.experimental import pallas as pl
from jax.experimental.pallas import tpu as pltpu

_F32 = jnp.float32
_BF16 = jnp.bfloat16
_HI = lax.Precision.HIGHEST

CONV_W = 4
LRU_BLOCKS = 4
LRU_C = 8.0
GDN_HEADS = 8
GDN_DK = 128
GDN_DV = 128
GDN_CHUNK = 64
XA_HEADS = 4
N_EXPERTS = 8
LN_EPS = 1e-5
RMS_EPS = 1e-6
LANES = 128
ROW_TILE = 512
VMEM_LIMIT_MB = 56


def _cparams(ndim, vmem_mb=VMEM_LIMIT_MB):
    return pltpu.CompilerParams(dimension_semantics=("arbitrary",) * ndim,
                                vmem_limit_bytes=vmem_mb * 1024 * 1024)


def _dot(a, b, precision=None):
    return jnp.dot(a, b, preferred_element_type=_F32, precision=precision)


def _dot_nt(a, b):
    return lax.dot_general(a, b, (((1,), (1,)), ((), ())), preferred_element_type=_F32)


def _dot_tn(a, b):
    return lax.dot_general(a, b, (((0,), (0,)), ((), ())), preferred_element_type=_F32)


def _layer_norm(v, g, b):
    mu = jnp.mean(v, axis=-1, keepdims=True)
    d = v - mu
    var = jnp.mean(d * d, axis=-1, keepdims=True)
    return d * lax.rsqrt(var + LN_EPS) * g + b


def _mm_body(x_ref, w_ref, o_ref, wb_ref):
    @pl.when(pl.program_id(2) == 0)
    def _():
        wb_ref[...] = w_ref[0].astype(_BF16)

    o_ref[0] = _dot(x_ref[...].astype(_BF16), wb_ref[...]).astype(o_ref.dtype)


def _mm(x, w, *, g0=0, ng=1, n0=0, n=None, tn, out_dtype):
    m, k = x.shape
    n = w.shape[2] if n is None else n
    tm = min(ROW_TILE, m)
    nb0 = n0 // tn
    return pl.pallas_call(
        _mm_body,
        grid=(ng, n // tn, m // tm),
        in_specs=[pl.BlockSpec((tm, k), lambda g, j, i: (i, 0)),
                  pl.BlockSpec((1, k, tn), lambda g, j, i: (g + g0, 0, j + nb0))],
        out_specs=pl.BlockSpec((1, tm, tn), lambda g, j, i: (g, i, j)),
        out_shape=jax.ShapeDtypeStruct((ng, m, n), out_dtype),
        scratch_shapes=[pltpu.VMEM((k, tn), _BF16)],
        compiler_params=_cparams(3),
        name="mm",
    )(x, w)


def _mm_ln_body(y_ref, w_ref, x_ref, g_ref, b_ref, o_ref, wb_ref, *, alpha):
    @pl.when(pl.program_id(0) == 0)
    def _():
        wb_ref[...] = w_ref[0].astype(_BF16)

    mix = _dot(y_ref[...].astype(_BF16), wb_ref[...])
    o_ref[...] = _layer_norm(alpha * x_ref[...] + mix, g_ref[0], b_ref[0])


def _mm_ln(y, w, gi, x, ln_g, ln_b, li, *, alpha):
    m, k = y.shape
    d = w.shape[2]
    tm = min(ROW_TILE, m)
    return pl.pallas_call(
        functools.partial(_mm_ln_body, alpha=alpha),
        grid=(m // tm,),
        in_specs=[pl.BlockSpec((tm, k), lambda i: (i, 0)),
                  pl.BlockSpec((1, k, d), lambda i: (gi, 0, 0), pipeline_mode=pl.Buffered(1)),
                  pl.BlockSpec((tm, d), lambda i: (i, 0)),
                  pl.BlockSpec((1, 1, d), lambda i: (li, 0, 0)),
                  pl.BlockSpec((1, 1, d), lambda i: (li, 0, 0))],
        out_specs=pl.BlockSpec((tm, d), lambda i: (i, 0)),
        out_shape=jax.ShapeDtypeStruct((m, d), _F32),
        scratch_shapes=[pltpu.VMEM((k, d), _BF16)],
        compiler_params=_cparams(1),
        name="mm_ln",
    )(y, w, x, ln_g, ln_b)


def _gu_body(x_ref, wg_ref, wu_ref, o_ref, wgb_ref, wub_ref):
    @pl.when(pl.program_id(2) == 0)
    def _():
        wgb_ref[...] = wg_ref[0].astype(_BF16)
        wub_ref[...] = wu_ref[0].astype(_BF16)

    xb = x_ref[...].astype(_BF16)
    g = _dot(xb, wgb_ref[...])
    u = _dot(xb, wub_ref[...])
    o_ref[0] = (jax.nn.silu(g) * u).astype(o_ref.dtype)


def _gate_up(x, w_gu, *, g0, ng, f, tf):
    m, k = x.shape
    tm = min(ROW_TILE, m)
    nf = f // tf
    return pl.pallas_call(
        _gu_body,
        grid=(ng, nf, m // tm),
        in_specs=[pl.BlockSpec((tm, k), lambda g, j, i: (i, 0)),
                  pl.BlockSpec((1, k, tf), lambda g, j, i: (g + g0, 0, j)),
                  pl.BlockSpec((1, k, tf), lambda g, j, i: (g + g0, 0, j + nf))],
        out_specs=pl.BlockSpec((1, tm, tf), lambda g, j, i: (g, i, j)),
        out_shape=jax.ShapeDtypeStruct((ng, m, f), _BF16),
        scratch_shapes=[pltpu.VMEM((k, tf), _BF16), pltpu.VMEM((k, tf), _BF16)],
        compiler_params=_cparams(3),
        name="gate_up",
    )(x, w_gu, w_gu)


def _router_body(x_ref, r_ref, gate_ref):
    logits = _dot(x_ref[...], r_ref[...], precision=_HI)
    lane = lax.broadcasted_iota(jnp.int32, logits.shape, 1).astype(_F32)
    l1 = jnp.where(lane < N_EXPERTS, logits, -jnp.inf)
    m1 = jnp.max(l1, axis=-1, keepdims=True)
    i1 = jnp.min(jnp.where(l1 == m1, lane, float(LANES)), axis=-1, keepdims=True)
    l2 = jnp.where(lane == i1, -jnp.inf, l1)
    m2 = jnp.max(l2, axis=-1, keepdims=True)
    i2 = jnp.min(jnp.where(l2 == m2, lane, float(LANES)), axis=-1, keepdims=True)
    e2 = jnp.exp(m2 - m1)
    den = 1.0 + e2
    gate_ref[...] = jnp.where(lane == i1, 1.0 / den, 0.0) + jnp.where(lane == i2, e2 / den, 0.0)


def _router(x, router_pad):
    m, k = x.shape
    tm = min(ROW_TILE, m)
    return pl.pallas_call(
        _router_body,
        grid=(m // tm,),
        in_specs=[pl.BlockSpec((tm, k), lambda i: (i, 0)),
                  pl.BlockSpec((k, LANES), lambda i: (0, 0))],
        out_specs=pl.BlockSpec((tm, LANES), lambda i: (i, 0)),
        out_shape=jax.ShapeDtypeStruct((m, LANES), _F32),
        compiler_params=_cparams(1),
        name="router",
    )(x, router_pad)


def _moe_down_body(h_ref, w_ref, gate_ref, x_ref, g_ref, b_ref, o_ref, acc_ref, *, alpha):
    e = pl.program_id(1)
    part = _dot(h_ref[0], w_ref[0])
    gate = gate_ref[...]
    lane = lax.broadcasted_iota(jnp.int32, gate.shape, 1)
    gcol = jnp.sum(jnp.where(lane == e, gate, 0.0), axis=-1, keepdims=True)

    @pl.when(e == 0)
    def _():
        acc_ref[...] = gcol * part

    @pl.when(e > 0)
    def _():
        acc_ref[...] += gcol * part

    @pl.when(e == pl.num_programs(1) - 1)
    def _():
        o_ref[...] = _layer_norm(alpha * x_ref[...] + acc_ref[...], g_ref[0], b_ref[0])


def _moe_down(h, w_down_bf16, gate, x, ln_g, ln_b, li, *, alpha):
    ne, m, f = h.shape
    d = w_down_bf16.shape[2]
    tm = min(ROW_TILE, m)
    return pl.pallas_call(
        functools.partial(_moe_down_body, alpha=alpha),
        grid=(m // tm, ne),
        in_specs=[pl.BlockSpec((1, tm, f), lambda i, e: (e, i, 0)),
                  pl.BlockSpec((1, f, d), lambda i, e: (e, 0, 0)),
                  pl.BlockSpec((tm, LANES), lambda i, e: (i, 0)),
                  pl.BlockSpec((tm, d), lambda i, e: (i, 0)),
                  pl.BlockSpec((1, 1, d), lambda i, e: (li, 0, 0)),
                  pl.BlockSpec((1, 1, d), lambda i, e: (li, 0, 0))],
        out_specs=pl.BlockSpec((tm, d), lambda i, e: (i, 0)),
        out_shape=jax.ShapeDtypeStruct((m, d), _F32),
        scratch_shapes=[pltpu.VMEM((tm, d), _F32)],
        compiler_params=_cparams(2),
        name="moe_down",
    )(h, w_down_bf16, gate, x, ln_g, ln_b)


def _seg_scan(a, b, tstep, n_steps, stride):
    sh = stride
    for s in range(n_steps):
        keep = tstep >= (1 << s)
        a_sh = pltpu.roll(a, sh, 0)
        b_sh = pltpu.roll(b, sh, 0)
        b = jnp.where(keep, a * b_sh + b, b)
        a = jnp.where(keep, a * a_sh, a)
        sh *= 2
    return a, b


def _lru_body(gate_ref, rec_ref, h0_ref, c0_ref, cw_ref, cb_ref, gw_ref, gb_ref, lam_ref,
              y_ref, hout_ref, cout_ref, xp_ref, hc_ref, cv_ref, gwb_ref, *, nseg, seg, stride, pad):
    rows = nseg * seg
    bw = gw_ref.shape[1]

    @pl.when(pl.program_id(0) == 0)
    def _():
        xp_ref[:, 0:pad, :] = c0_ref[...]
        hc_ref[...] = h0_ref[...]
        gwb_ref[...] = gw_ref[...].astype(_BF16)

    for s in range(nseg):
        xp_ref[s, pad:pad + seg, :] = rec_ref[s * seg:(s + 1) * seg, :]
        acc = xp_ref[s, pad - 3 * stride:pad - 3 * stride + seg, :] * cw_ref[0:1, :]
        for j in range(1, CONV_W):
            off = pad - (CONV_W - 1 - j) * stride
            acc = acc + xp_ref[s, off:off + seg, :] * cw_ref[j:j + 1, :]
        cv_ref[s * seg:(s + 1) * seg, :] = acc + cb_ref[...]
        tail = xp_ref[s, seg:seg + pad, :]
        xp_ref[s, 0:pad, :] = tail
        cout_ref[s] = tail

    row = lax.broadcasted_iota(jnp.int32, (rows, bw), 0)
    if stride == 1:
        tstep = row & (seg - 1)
    else:
        tstep = row >> int(math.log2(stride))
    n_steps = int(math.log2(seg // stride))

    for n in range(LRU_BLOCKS):
        cs = slice(n * bw, (n + 1) * bw)
        cv = cv_ref[:, cs]
        gates = _dot(cv.astype(_BF16), gwb_ref[n]) + gb_ref[n]
        r = jax.nn.sigmoid(gates[:, :bw])
        ig = jax.nn.sigmoid(gates[:, bw:])
        log_a = -LRU_C * r * jax.nn.softplus(-lam_ref[:, cs])
        a = jnp.exp(log_a)
        mult = jnp.sqrt(-jnp.tanh(log_a) * (a * a + 1.0))
        a_cum, b_cum = _seg_scan(a, mult * ig * cv, tstep, n_steps, stride)
        hc = hc_ref[:, cs]
        if stride == 1:
            hprev = jnp.concatenate([jnp.broadcast_to(hc[s:s + 1, :], (seg, bw)) for s in range(nseg)], axis=0)
        else:
            hprev = jnp.concatenate([hc] * (seg // stride), axis=0)
        h = a_cum * hprev + b_cum
        y_ref[:, cs] = (h * jax.nn.gelu(gate_ref[:, cs])).astype(y_ref.dtype)
        if stride == 1:
            hc_ref[:, cs] = jnp.concatenate([h[(s + 1) * seg - 1:(s + 1) * seg, :] for s in range(nseg)], axis=0)
        else:
            hc_ref[:, cs] = h[rows - stride:, :]

    hout_ref[...] = hc_ref[...]


def _lru_seq(gu, h0, c0, conv_w, conv_b, gate_w, gate_b, lam, *, n_tiles, nseg, seg, stride, pad):
    rows = nseg * seg
    r = gu.shape[1] // 2
    bw = r // LRU_BLOCKS
    const2 = lambda i: (0, 0)
    const3 = lambda i: (0, 0, 0)
    return pl.pallas_call(
        functools.partial(_lru_body, nseg=nseg, seg=seg, stride=stride, pad=pad),
        grid=(n_tiles,),
        in_specs=[pl.BlockSpec((rows, r), lambda i: (i, 0)),
                  pl.BlockSpec((rows, r), lambda i: (i, 1)),
                  pl.BlockSpec((nseg * stride, r), const2),
                  pl.BlockSpec((nseg, pad, r), const3),
                  pl.BlockSpec((CONV_W, r), const2),
                  pl.BlockSpec((1, r), const2),
                  pl.BlockSpec((LRU_BLOCKS, bw, 2 * bw), const3),
                  pl.BlockSpec((LRU_BLOCKS, 1, 2 * bw), const3),
                  pl.BlockSpec((1, r), const2)],
        out_specs=[pl.BlockSpec((rows, r), lambda i: (i, 0)),
                   pl.BlockSpec((nseg * stride, r), const2),
                   pl.BlockSpec((nseg, pad, r), const3)],
        out_shape=[jax.ShapeDtypeStruct((gu.shape[0], r), _BF16),
                   jax.ShapeDtypeStruct((nseg * stride, r), _F32),
                   jax.ShapeDtypeStruct((nseg, pad, r), _F32)],
        scratch_shapes=[pltpu.VMEM((nseg, pad + seg, r), _F32),
                        pltpu.VMEM((nseg * stride, r), _F32),
                        pltpu.VMEM((rows, r), _F32),
                        pltpu.VMEM((LRU_BLOCKS, bw, 2 * bw), _BF16)],
        compiler_params=_cparams(1),
        name="lru_seq",
    )(gu, gu, h0, c0, conv_w, conv_b, gate_w, gate_b, lam)


def _l2norm(t):
    return t * lax.rsqrt(jnp.sum(t * t, axis=-1, keepdims=True) + RMS_EPS)


def _gdn_chunk(qkv, z, beta, g, state, norm_g, *, c, hg):
    dk = GDN_DK
    qk_w = GDN_HEADS * dk
    r4 = hg * c
    lg_c = int(math.log2(c))
    row = lax.broadcasted_iota(jnp.int32, (r4, r4), 0)
    col = lax.broadcasted_iota(jnp.int32, (r4, r4), 1)
    same = (row >> lg_c) == (col >> lg_c)
    incl = same & (row >= col)
    strict = same & (row > col)
    eye = jnp.where(row == col, 1.0, 0.0)

    tr = lax.broadcasted_iota(jnp.int32, (c, c), 0)
    tc = lax.broadcasted_iota(jnp.int32, (c, c), 1)
    gc = _dot(jnp.where(tr >= tc, 1.0, 0.0), g, precision=_HI)
    gct = jnp.concatenate([gc, jnp.zeros((LANES - c, LANES), _F32)], axis=0).T

    outs, new_state = [], []
    for g0 in range(0, GDN_HEADS, hg):
        hs = list(range(g0, g0 + hg))
        q4 = jnp.concatenate([_l2norm(qkv[:, h * dk:(h + 1) * dk]) * (dk ** -0.5) for h in hs], axis=0)
        k4 = jnp.concatenate([_l2norm(qkv[:, qk_w + h * dk:qk_w + (h + 1) * dk]) for h in hs], axis=0)
        v4 = jnp.concatenate([qkv[:, 2 * qk_w + h * dk:2 * qk_w + (h + 1) * dk] for h in hs], axis=0)
        beta4 = jnp.concatenate([jnp.broadcast_to(beta[:, h:h + 1], (c, dk)) for h in hs], axis=0)
        gc4 = jnp.concatenate([jnp.broadcast_to(gc[:, h:h + 1], (c, dk)) for h in hs], axis=0)
        gcol = jnp.concatenate([jnp.broadcast_to(gc[:, h:h + 1], (c, r4)) for h in hs], axis=0)
        grow = jnp.concatenate([gct[h:h + 1, 0:c] for h in hs], axis=1)
        decay = jnp.exp(jnp.where(incl, gcol - grow, -jnp.inf))
        kb4 = k4 * beta4
        aq = _dot_nt(jnp.concatenate([kb4, q4], axis=0).astype(_BF16), k4.astype(_BF16))
        a4 = jnp.where(strict, aq[:r4] * decay, 0.0)
        attn4 = aq[r4:] * decay
        bp = -a4
        tinv = eye + bp
        for _ in range(lg_c - 1):
            bp = _dot(bp, bp, precision=_HI)
            tinv = tinv + _dot(bp, tinv, precision=_HI)
        egc4 = jnp.exp(gc4)
        rhs = jnp.concatenate([v4 * beta4, kb4 * egc4], axis=1)
        uw = _dot(tinv.astype(_BF16), rhs.astype(_BF16))
        u4, w4 = uw[:, :dk], uw[:, dk:]
        qg4 = q4 * egc4
        vn, qs = [], []
        for hh, h in enumerate(hs):
            rs = slice(hh * c, (hh + 1) * c)
            wq = jnp.concatenate([w4[rs], qg4[rs]], axis=0).astype(_BF16)
            ws_qs = _dot(wq, state[h].astype(_BF16))
            vn.append(u4[rs] - ws_qs[:c])
            qs.append(ws_qs[c:])
        vn4 = jnp.concatenate(vn, axis=0)
        o4 = jnp.concatenate(qs, axis=0) + _dot(attn4.astype(_BF16), vn4.astype(_BF16))
        for hh, h in enumerate(hs):
            rs = slice(hh * c, (hh + 1) * c)
            glast = gc[c - 1:c, h:h + 1]
            kd = k4[rs] * jnp.exp(glast - gc4[rs])
            new_state.append(state[h] * jnp.exp(glast) + _dot_tn(kd.astype(_BF16), vn[hh].astype(_BF16)))
            oh = o4[rs]
            oh = oh * lax.rsqrt(jnp.mean(oh * oh, axis=-1, keepdims=True) + RMS_EPS) * norm_g
            outs.append(oh * jax.nn.silu(z[:, h * dk:(h + 1) * dk]))
    return outs, new_state


def _gdn_gates(ba, alog, dtb):
    beta = jax.nn.sigmoid(ba[:, :LANES])
    g = -jnp.exp(alog) * jax.nn.softplus(ba[:, LANES:] + dtb)
    return beta, g


def _gdn_conv(xp_ref, s, x, cw_ref, *, pad, seg):
    xp_ref[s, pad:pad + seg, :] = x
    acc = xp_ref[s, pl.ds(pad - 3, seg), :] * cw_ref[0:1, :]
    for j in range(1, CONV_W):
        acc = acc + xp_ref[s, pl.ds(pad - (CONV_W - 1 - j), seg), :] * cw_ref[j:j + 1, :]
    return jax.nn.silu(acc)


def _gdn_prompt_body(qkv_ref, z_ref, ba_ref, cw_ref, alog_ref, dtb_ref, ng_ref,
                     o_ref, sout_ref, cout_ref, xp_ref, s_ref, *, nseg, seg, pad):
    @pl.when(pl.program_id(0) == 0)
    def _():
        xp_ref[...] = jnp.zeros(xp_ref.shape, _F32)
        s_ref[...] = jnp.zeros(s_ref.shape, _F32)

    def per_seq(b, carry):
        r0 = pl.multiple_of(b * seg, seg)
        qkv = _gdn_conv(xp_ref, b, qkv_ref[pl.ds(r0, seg), :], cw_ref, pad=pad, seg=seg)
        tail = xp_ref[b, pl.ds(seg, pad), :]
        xp_ref[b, 0:pad, :] = tail
        cout_ref[b] = tail
        beta, g = _gdn_gates(ba_ref[pl.ds(r0, seg), :], alog_ref[...], dtb_ref[...])
        state = [s_ref[b, h] for h in range(GDN_HEADS)]
        outs, new_state = _gdn_chunk(qkv, z_ref[pl.ds(r0, seg), :], beta, g, state, ng_ref[...],
                                     c=seg, hg=256 // seg)
        for h in range(GDN_HEADS):
            s_ref[b, h] = new_state[h]
            o_ref[pl.ds(r0, seg), h * GDN_DV:(h + 1) * GDN_DV] = outs[h].astype(o_ref.dtype)
        return carry

    lax.fori_loop(0, nseg, per_seq, 0)

    @pl.when(pl.program_id(0) == pl.num_programs(0) - 1)
    def _():
        sout_ref[...] = s_ref[...]


def _gdn_prompt(proj, ba, conv_w, alog, dtb, norm_g, *, n_tiles, nseg, seg):
    rows = nseg * seg
    cd = conv_w.shape[1]
    vw = GDN_HEADS * GDN_DV
    pad = 8
    const2 = lambda i: (0, 0)
    return pl.pallas_call(
        functools.partial(_gdn_prompt_body, nseg=nseg, seg=seg, pad=pad),
        grid=(n_tiles,),
        in_specs=[pl.BlockSpec((rows, cd), lambda i: (i, 0)),
                  pl.BlockSpec((rows, vw), lambda i: (i, cd // vw)),
                  pl.BlockSpec((rows, 2 * LANES), lambda i: (i, 0)),
                  pl.BlockSpec((CONV_W, cd), const2),
                  pl.BlockSpec((1, LANES), const2),
                  pl.BlockSpec((1, LANES), const2),
                  pl.BlockSpec((1, GDN_DV), const2)],
        out_specs=[pl.BlockSpec((rows, vw), lambda i: (i, 0)),
                   pl.BlockSpec((nseg, GDN_HEADS, GDN_DK, GDN_DV), lambda i: (0, 0, 0, 0)),
                   pl.BlockSpec((nseg, pad, cd), lambda i: (0, 0, 0))],
        out_shape=[jax.ShapeDtypeStruct((proj.shape[0], vw), _BF16),
                   jax.ShapeDtypeStruct((nseg, GDN_HEADS, GDN_DK, GDN_DV), _F32),
                   jax.ShapeDtypeStruct((nseg, pad, cd), _F32)],
        scratch_shapes=[pltpu.VMEM((nseg, pad + seg, cd), _F32),
                        pltpu.VMEM((nseg, GDN_HEADS, GDN_DK, GDN_DV), _F32)],
        compiler_params=_cparams(1),
        name="gdn_prompt",
    )(proj, proj, ba, conv_w, alog, dtb, norm_g)


def _gdn_sample_body(qkv_ref, z_ref, ba_ref, s0_ref, c0_ref, cw_ref, alog_ref, dtb_ref, ng_ref,
                     o_ref, sout_ref, cout_ref, xp_ref, *, nb, seg, steps, pad):
    def per_seq(b, carry):
        xp_ref[0, 0:pad, :] = c0_ref[b]
        qkv = _gdn_conv(xp_ref, 0, qkv_ref[b], cw_ref, pad=pad, seg=seg)
        cout_ref[b] = xp_ref[0, pl.ds(steps, pad), :]
        beta, g = _gdn_gates(ba_ref[b], alog_ref[...], dtb_ref[...])
        live = lax.broadcasted_iota(jnp.int32, beta.shape, 0) < steps
        beta = jnp.where(live, beta, 0.0)
        g = jnp.where(live, g, 0.0)
        state = [s0_ref[b, h] for h in range(GDN_HEADS)]
        outs, new_state = _gdn_chunk(qkv, z_ref[b], beta, g, state, ng_ref[...], c=seg, hg=GDN_HEADS)
        for h in range(GDN_HEADS):
            sout_ref[b, h] = new_state[h]
            o_ref[b, :, h * GDN_DV:(h + 1) * GDN_DV] = outs[h]
        return carry

    lax.fori_loop(0, nb, per_seq, 0)


def _gdn_sample(proj, ba, s0, c0, conv_w, alog, dtb, norm_g, *, steps):
    nbatch, seg, _ = proj.shape
    cd = conv_w.shape[1]
    vw = GDN_HEADS * GDN_DV
    pad = c0.shape[1]
    nb = 8
    const2 = lambda i: (0, 0)
    return pl.pallas_call(
        functools.partial(_gdn_sample_body, nb=nb, seg=seg, steps=steps, pad=pad),
        grid=(nbatch // nb,),
        in_specs=[pl.BlockSpec((nb, seg, cd), lambda i: (i, 0, 0)),
                  pl.BlockSpec((nb, seg, vw), lambda i: (i, 0, cd // vw)),
                  pl.BlockSpec((nb, seg, 2 * LANES), lambda i: (i, 0, 0)),
                  pl.BlockSpec((nb, GDN_HEADS, GDN_DK, GDN_DV), lambda i: (i, 0, 0, 0)),
                  pl.BlockSpec((nb, pad, cd), lambda i: (i, 0, 0)),
                  pl.BlockSpec((CONV_W, cd), const2),
                  pl.BlockSpec((1, LANES), const2),
                  pl.BlockSpec((1, LANES), const2),
                  pl.BlockSpec((1, GDN_DV), const2)],
        out_specs=[pl.BlockSpec((nb, seg, vw), lambda i: (i, 0, 0)),
                   pl.BlockSpec((nb, GDN_HEADS, GDN_DK, GDN_DV), lambda i: (i, 0, 0, 0)),
                   pl.BlockSpec((nb, pad, cd), lambda i: (i, 0, 0))],
        out_shape=[jax.ShapeDtypeStruct((nbatch, seg, vw), _F32),
                   jax.ShapeDtypeStruct(s0.shape, _F32),
                   jax.ShapeDtypeStruct((nbatch, pad, cd), _F32)],
        scratch_shapes=[pltpu.VMEM((1, pad + seg, cd), _F32)],
        compiler_params=_cparams(1),
        name="gdn_sample",
    )(proj, proj, ba, s0, c0, conv_w, alog, dtb, norm_g)


def _softmax_rows(s):
    e = jnp.exp(s - jnp.max(s, axis=-1, keepdims=True))
    return e / jnp.sum(e, axis=-1, keepdims=True)


def _attn_prompt_body(q_ref, k_ref, v_ref, o_ref, *, nseg, seg):
    hd = q_ref.shape[1] // XA_HEADS
    scale = hd ** -0.5
    for b in range(nseg):
        rs = slice(b * seg, (b + 1) * seg)
        for h in range(XA_HEADS):
            cs = slice(h * hd, (h + 1) * hd)
            s = _dot_nt(q_ref[rs, cs], k_ref[b, :, cs]) * scale
            p = _softmax_rows(s).astype(_BF16)
            o_ref[rs, cs] = _dot(p, v_ref[b, :, cs]).astype(o_ref.dtype)


def _attn_prompt(q, mem_k, mem_v, *, n_tiles, nseg, seg):
    rows = nseg * seg
    d = q.shape[1]
    mem = mem_k.shape[1]
    return pl.pallas_call(
        functools.partial(_attn_prompt_body, nseg=nseg, seg=seg),
        grid=(n_tiles,),
        in_specs=[pl.BlockSpec((rows, d), lambda i: (i, 0)),
                  pl.BlockSpec((nseg, mem, d), lambda i: (0, 0, 0)),
                  pl.BlockSpec((nseg, mem, d), lambda i: (0, 0, 0))],
        out_specs=pl.BlockSpec((rows, d), lambda i: (i, 0)),
        out_shape=jax.ShapeDtypeStruct(q.shape, _BF16),
        compiler_params=_cparams(1),
        name="attn_prompt",
    )(q, mem_k, mem_v)


def _attn_sample_body(q_ref, k_ref, v_ref, o_ref, *, nb):
    hd = q_ref.shape[2] // XA_HEADS
    scale = hd ** -0.5

    def per_seq(b, carry):
        q = q_ref[b].astype(_BF16)
        for h in range(XA_HEADS):
            cs = slice(h * hd, (h + 1) * hd)
            s = _dot_nt(q[:, cs], k_ref[0, b, :, cs].astype(_BF16)) * scale
            p = _softmax_rows(s).astype(_BF16)
            o_ref[b, :, cs] = _dot(p, v_ref[0, b, :, cs].astype(_BF16))
        return carry

    lax.fori_loop(0, nb, per_seq, 0)


def _attn_sample(q, cache_k, cache_v, layer):
    nbatch, t8, d = q.shape
    mem = cache_k.shape[2]
    nb = 8
    return pl.pallas_call(
        functools.partial(_attn_sample_body, nb=nb),
        grid=(nbatch // nb,),
        in_specs=[pl.BlockSpec((nb, t8, d), lambda i: (i, 0, 0)),
                  pl.BlockSpec((1, nb, mem, d), lambda i: (layer, i, 0, 0)),
                  pl.BlockSpec((1, nb, mem, d), lambda i: (layer, i, 0, 0))],
        out_specs=pl.BlockSpec((nb, t8, d), lambda i: (i, 0, 0)),
        out_shape=jax.ShapeDtypeStruct(q.shape, _F32),
        compiler_params=_cparams(1),
        name="attn_sample",
    )(q, cache_k, cache_v)


def kernel(x_prompt, x_sample, mem_prompt, state_lru_h, state_lru_conv, state_gdn_s, state_gdn_conv,
           cache_mem_k, cache_mem_v, ln_g, ln_b, xa_wq, xa_wk, xa_wv, xa_wo,
           lru_w_in, lru_conv_w, lru_conv_b, lru_gate_w, lru_gate_b, lru_lambda, lru_w_out,
           gdn_w_in, gdn_conv_w, gdn_a_log, gdn_dt_bias, gdn_norm_g, gdn_w_out,
           ffn_w_gu, ffn_w_down, moe_router, moe_w_gu, moe_w_down):
    depth = ln_g.shape[0]
    bp, seq, d = x_prompt.shape
    bs, ts, _ = x_sample.shape
    mem = mem_prompt.shape[1]
    alpha = (2 * depth) ** 0.25
    seg = GDN_CHUNK
    nchunk = seq // seg
    mp, ms = bp * seq, bs * ts
    assert bp * seg == ROW_TILE and ms == ROW_TILE and seq % seg == 0 and ts >= CONV_W - 1
    t8 = 8
    pad_p = 8
    pad_s = (CONV_W - 1) * bs
    r = lru_w_in.shape[2] // 2
    cd = gdn_conv_w.shape[2]
    vw = GDN_HEADS * GDN_DV
    d_ff = ffn_w_gu.shape[2] // 2
    d_fe = moe_w_gu.shape[3] // 2

    xp = x_prompt.reshape(bp, nchunk, seg, d).transpose(1, 0, 2, 3).reshape(mp, d)
    x = jnp.concatenate([xp, x_sample.reshape(ms, d)], axis=0)

    memx = mem_prompt.reshape(bp * mem, d)
    mem_k_p = _mm(memx, xa_wk, ng=depth, tn=1024, out_dtype=_F32).reshape(depth, bp, mem, d)
    mem_v_p = _mm(memx, xa_wv, ng=depth, tn=1024, out_dtype=_F32).reshape(depth, bp, mem, d)
    mem_k_b = mem_k_p.astype(_BF16)
    mem_v_b = mem_v_p.astype(_BF16)
    cache_k = cache_mem_k.reshape(depth, bs, mem, d)
    cache_v = cache_mem_v.reshape(depth, bs, mem, d)

    ln_g3 = ln_g.reshape(depth * 3, 1, d)
    moe_w_gu3 = moe_w_gu.reshape((-1,) + moe_w_gu.shape[2:])
    ln_b3 = ln_b.reshape(depth * 3, 1, d)

    def put_sample(full, sample_rows):
        return lax.dynamic_update_slice(full, sample_rows.astype(full.dtype), (mp, 0))

    def to_time_major(a):
        return a.reshape(bs, ts, -1).transpose(1, 0, 2).reshape(ms, -1)

    def to_batch_major(a):
        return a.reshape(ts, bs, -1).transpose(1, 0, 2).reshape(ms, -1)

    def pad_t(a):
        a = a.reshape(bs, ts, -1)
        return jnp.pad(a, ((0, 0), (0, t8 - ts), (0, 0)))

    lru_h_p, lru_c_p, lru_h_s, lru_c_s = [], [], [], []
    gdn_s_p, gdn_c_p, gdn_s_s, gdn_c_s = [], [], [], []

    for i in range(depth):
        j = i // 2
        if i % 2 == 0:
            gu = _mm(x, lru_w_in, g0=j, tn=1024, out_dtype=_F32)[0]
            lam = lru_lambda[j].reshape(1, r)
            cb = lru_conv_b[j].reshape(1, r)
            gb = lru_gate_b[j].reshape(LRU_BLOCKS, 1, -1)
            y, h_p, c_p = _lru_seq(gu, jnp.zeros((bp, r), _F32), jnp.zeros((bp, pad_p, r), _F32),
                                   lru_conv_w[j], cb, lru_gate_w[j], gb, lam,
                                   n_tiles=nchunk, nseg=bp, seg=seg, stride=1, pad=pad_p)
            gu_s = to_time_major(gu[mp:])
            c0_s = state_lru_conv[j].transpose(1, 0, 2).reshape(1, pad_s, r)
            y_s, h_s, c_s = _lru_seq(gu_s, state_lru_h[j], c0_s, lru_conv_w[j], cb, lru_gate_w[j], gb, lam,
                                     n_tiles=1, nseg=1, seg=ms, stride=bs, pad=pad_s)
            y = put_sample(y, to_batch_major(y_s))
            lru_h_p.append(h_p)
            lru_c_p.append(c_p[:, pad_p - (CONV_W - 1):, :])
            lru_h_s.append(h_s)
            lru_c_s.append(c_s.reshape(CONV_W - 1, bs, r).transpose(1, 0, 2))
            w_out, wi = lru_w_out, j
        else:
            proj = _mm(x, gdn_w_in, g0=j, n=cd + vw, tn=1024, out_dtype=_F32)[0]
            w_tail = gdn_w_in[j, :, cd + vw:]
            w_ba = jnp.zeros((1, d, 2 * LANES), _F32)
            w_ba = w_ba.at[0, :, :GDN_HEADS].set(w_tail[:, :GDN_HEADS])
            w_ba = w_ba.at[0, :, LANES:LANES + GDN_HEADS].set(w_tail[:, GDN_HEADS:])
            ba = _mm(x, w_ba, tn=2 * LANES, out_dtype=_F32)[0]
            alog = jnp.zeros((1, LANES), _F32).at[0, :GDN_HEADS].set(gdn_a_log[j])
            dtb = jnp.zeros((1, LANES), _F32).at[0, :GDN_HEADS].set(gdn_dt_bias[j])
            ng = gdn_norm_g[j].reshape(1, GDN_DV)
            y, s_p, c_p = _gdn_prompt(proj, ba, gdn_conv_w[j], alog, dtb, ng, n_tiles=nchunk, nseg=bp, seg=seg)
            c0_s = jnp.pad(state_gdn_conv[j], ((0, 0), (pad_p - (CONV_W - 1), 0), (0, 0)))
            y_s, s_s, c_s = _gdn_sample(pad_t(proj[mp:]), pad_t(ba[mp:]), state_gdn_s[j], c0_s,
                                        gdn_conv_w[j], alog, dtb, ng, steps=ts)
            y = put_sample(y, y_s[:, :ts, :].reshape(ms, vw))
            gdn_s_p.append(s_p)
            gdn_c_p.append(c_p[:, pad_p - (CONV_W - 1):, :])
            gdn_s_s.append(s_s)
            gdn_c_s.append(c_s[:, pad_p - (CONV_W - 1):, :])
            w_out, wi = gdn_w_out, j
        x = _mm_ln(y, w_out, wi, x, ln_g3, ln_b3, 3 * i, alpha=alpha)

        q = _mm(x, xa_wq, g0=i, tn=1024, out_dtype=_BF16)[0]
        o = _attn_prompt(q, mem_k_b[i], mem_v_b[i], n_tiles=nchunk, nseg=bp, seg=seg)
        o_s = _attn_sample(pad_t(q[mp:].astype(_F32)), cache_k, cache_v, i)
        o = put_sample(o, o_s[:, :ts, :].reshape(ms, d))
        x = _mm_ln(o, xa_wo, i, x, ln_g3, ln_b3, 3 * i + 1, alpha=alpha)

        if i % 2 == 0:
            hdn = _gate_up(x, ffn_w_gu, g0=j, ng=1, f=d_ff, tf=512)[0]
            x = _mm_ln(hdn, ffn_w_down, j, x, ln_g3, ln_b3, 3 * i + 2, alpha=alpha)
        else:
            router_pad = jnp.zeros((d, LANES), _F32).at[:, :N_EXPERTS].set(moe_router[j])
            gate = _router(x, router_pad)
            hdn = _gate_up(x, moe_w_gu3, g0=j * N_EXPERTS, ng=N_EXPERTS, f=d_fe, tf=512)
            x = _moe_down(hdn, moe_w_down[j].astype(_BF16), gate, x, ln_g3, ln_b3, 3 * i + 2, alpha=alpha)

    y_prompt = x[:mp].reshape(nchunk, bp, seg, d).transpose(1, 0, 2, 3).reshape(bp, seq, d)
    y_sample = x[mp:].reshape(bs, ts, d)
    hd = d // XA_HEADS
    return (y_prompt, y_sample,
            jnp.stack(lru_h_p), jnp.stack(lru_c_p), jnp.stack(gdn_s_p), jnp.stack(gdn_c_p),
            mem_k_p.reshape(depth, bp, mem, XA_HEADS, hd), mem_v_p.reshape(depth, bp, mem, XA_HEADS, hd),
            jnp.stack(lru_h_s), jnp.stack(lru_c_s), jnp.stack(gdn_s_s), jnp.stack(gdn_c_s))
```

```python
import functools
import math

import jax
import jax.numpy as jnp
from jax import lax
from jax.experimental import pallas as pl
from jax.experimental.pallas import tpu as pltpu

_F32 = jnp.float32
_BF16 = jnp.bfloat16
_HI = lax.Precision.HIGHEST

CONV_W = 4
LRU_BLOCKS = 4
LRU_C = 8.0
GDN_HEADS = 8
GDN_DK = 128
GDN_DV = 128
GDN_CHUNK = 64
GDN_SEQS_PER_STEP = 2
XA_HEADS = 4
N_EXPERTS = 8
LN_EPS = 1e-5
RMS_EPS = 1e-6
LANES = 128
ROW_TILE = 512
VMEM_LIMIT_MB = 56


def _cparams(ndim, vmem_mb=VMEM_LIMIT_MB):
    return pltpu.CompilerParams(dimension_semantics=("arbitrary",) * ndim,
                                vmem_limit_bytes=vmem_mb * 1024 * 1024)


def _dot(a, b, precision=None):
    return jnp.dot(a, b, preferred_element_type=_F32, precision=precision)


def _bdot(a, b):
    return _dot(a.astype(_BF16), b.astype(_BF16))


def _dot_nt(a, b):
    return lax.dot_general(a, b, (((1,), (1,)), ((), ())), preferred_element_type=_F32)


def _dot_tn(a, b):
    return lax.dot_general(a, b, (((0,), (0,)), ((), ())), preferred_element_type=_F32)


def _layer_norm(v, g, b):
    mu = jnp.mean(v, axis=-1, keepdims=True)
    d = v - mu
    var = jnp.mean(d * d, axis=-1, keepdims=True)
    return d * lax.rsqrt(var + LN_EPS) * g + b


def _mm_body(x_ref, w_ref, o_ref, wb_ref):
    @pl.when(pl.program_id(2) == 0)
    def _():
        wb_ref[...] = w_ref[0].astype(_BF16)

    o_ref[0] = _dot(x_ref[...].astype(_BF16), wb_ref[...]).astype(o_ref.dtype)


def _mm(x, w, *, g0=0, ng=1, n0=0, n=None, tn, out_dtype):
    m, k = x.shape
    n = w.shape[2] if n is None else n
    tm = min(ROW_TILE, m)
    nb0 = n0 // tn
    return pl.pallas_call(
        _mm_body,
        grid=(ng, n // tn, m // tm),
        in_specs=[pl.BlockSpec((tm, k), lambda g, j, i: (i, 0)),
                  pl.BlockSpec((1, k, tn), lambda g, j, i: (g + g0, 0, j + nb0))],
        out_specs=pl.BlockSpec((1, tm, tn), lambda g, j, i: (g, i, j)),
        out_shape=jax.ShapeDtypeStruct((ng, m, n), out_dtype),
        scratch_shapes=[pltpu.VMEM((k, tn), _BF16)],
        compiler_params=_cparams(3),
        name="mm",
    )(x, w)


def _mm_ln_body(y_ref, w_ref, x_ref, g_ref, b_ref, o_ref, wb_ref, *, alpha):
    @pl.when(pl.program_id(0) == 0)
    def _():
        wb_ref[...] = w_ref[0].astype(_BF16)

    mix = _dot(y_ref[...].astype(_BF16), wb_ref[...])
    o_ref[...] = _layer_norm(alpha * x_ref[...] + mix, g_ref[0], b_ref[0])


def _mm_ln(y, w, gi, x, ln_g, ln_b, li, *, alpha):
    m, k = y.shape
    d = w.shape[2]
    tm = min(ROW_TILE, m)
    return pl.pallas_call(
        functools.partial(_mm_ln_body, alpha=alpha),
        grid=(m // tm,),
        in_specs=[pl.BlockSpec((tm, k), lambda i: (i, 0)),
                  pl.BlockSpec((1, k, d), lambda i: (gi, 0, 0), pipeline_mode=pl.Buffered(1)),
                  pl.BlockSpec((tm, d), lambda i: (i, 0)),
                  pl.BlockSpec((1, 1, d), lambda i: (li, 0, 0)),
                  pl.BlockSpec((1, 1, d), lambda i: (li, 0, 0))],
        out_specs=pl.BlockSpec((tm, d), lambda i: (i, 0)),
        out_shape=jax.ShapeDtypeStruct((m, d), _F32),
        scratch_shapes=[pltpu.VMEM((k, d), _BF16)],
        compiler_params=_cparams(1),
        name="mm_ln",
    )(y, w, x, ln_g, ln_b)


def _gu_body(x_ref, wg_ref, wu_ref, o_ref, wgb_ref, wub_ref):
    @pl.when(pl.program_id(2) == 0)
    def _():
        wgb_ref[...] = wg_ref[0].astype(_BF16)
        wub_ref[...] = wu_ref[0].astype(_BF16)

    xb = x_ref[...].astype(_BF16)
    g = _dot(xb, wgb_ref[...])
    u = _dot(xb, wub_ref[...])
    o_ref[0] = (jax.nn.silu(g) * u).astype(o_ref.dtype)


def _gate_up(x, w_gu, *, g0, ng, f, tf):
    m, k = x.shape
    tm = min(ROW_TILE, m)
    nf = f // tf
    return pl.pallas_call(
        _gu_body,
        grid=(ng, nf, m // tm),
        in_specs=[pl.BlockSpec((tm, k), lambda g, j, i: (i, 0)),
                  pl.BlockSpec((1, k, tf), lambda g, j, i: (g + g0, 0, j)),
                  pl.BlockSpec((1, k, tf), lambda g, j, i: (g + g0, 0, j + nf))],
        out_specs=pl.BlockSpec((1, tm, tf), lambda g, j, i: (g, i, j)),
        out_shape=jax.ShapeDtypeStruct((ng, m, f), _BF16),
        scratch_shapes=[pltpu.VMEM((k, tf), _BF16), pltpu.VMEM((k, tf), _BF16)],
        compiler_params=_cparams(3),
        name="gate_up",
    )(x, w_gu, w_gu)


def _router_body(x_ref, r_ref, gate_ref):
    logits = _dot(x_ref[...], r_ref[...], precision=_HI)
    lane = lax.broadcasted_iota(jnp.int32, logits.shape, 1).astype(_F32)
    l1 = jnp.where(lane < N_EXPERTS, logits, -jnp.inf)
    m1 = jnp.max(l1, axis=-1, keepdims=True)
    i1 = jnp.min(jnp.where(l1 == m1, lane, float(LANES)), axis=-1, keepdims=True)
    l2 = jnp.where(lane == i1, -jnp.inf, l1)
    m2 = jnp.max(l2, axis=-1, keepdims=True)
    i2 = jnp.min(jnp.where(l2 == m2, lane, float(LANES)), axis=-1, keepdims=True)
    e2 = jnp.exp(m2 - m1)
    den = 1.0 + e2
    gate_ref[...] = jnp.where(lane == i1, 1.0 / den, 0.0) + jnp.where(lane == i2, e2 / den, 0.0)


def _router(x, router_pad):
    m, k = x.shape
    tm = min(ROW_TILE, m)
    return pl.pallas_call(
        _router_body,
        grid=(m // tm,),
        in_specs=[pl.BlockSpec((tm, k), lambda i: (i, 0)),
                  pl.BlockSpec((k, LANES), lambda i: (0, 0))],
        out_specs=pl.BlockSpec((tm, LANES), lambda i: (i, 0)),
        out_shape=jax.ShapeDtypeStruct((m, LANES), _F32),
        compiler_params=_cparams(1),
        name="router",
    )(x, router_pad)


def _moe_down_body(h_ref, w_ref, gate_ref, x_ref, g_ref, b_ref, o_ref, acc_ref, *, alpha):
    e = pl.program_id(1)
    part = _dot(h_ref[0], w_ref[0])
    gate = gate_ref[...]
    lane = lax.broadcasted_iota(jnp.int32, gate.shape, 1)
    gcol = jnp.sum(jnp.where(lane == e, gate, 0.0), axis=-1, keepdims=True)

    @pl.when(e == 0)
    def _():
        acc_ref[...] = gcol * part

    @pl.when(e > 0)
    def _():
        acc_ref[...] += gcol * part

    @pl.when(e == pl.num_programs(1) - 1)
    def _():
        o_ref[...] = _layer_norm(alpha * x_ref[...] + acc_ref[...], g_ref[0], b_ref[0])


def _moe_down(h, w_down_bf16, gate, x, ln_g, ln_b, li, *, alpha):
    ne, m, f = h.shape
    d = w_down_bf16.shape[2]
    tm = min(ROW_TILE, m)
    return pl.pallas_call(
        functools.partial(_moe_down_body, alpha=alpha),
        grid=(m // tm, ne),
        in_specs=[pl.BlockSpec((1, tm, f), lambda i, e: (e, i, 0)),
                  pl.BlockSpec((1, f, d), lambda i, e: (e, 0, 0)),
                  pl.BlockSpec((tm, LANES), lambda i, e: (i, 0)),
                  pl.BlockSpec((tm, d), lambda i, e: (i, 0)),
                  pl.BlockSpec((1, 1, d), lambda i, e: (li, 0, 0)),
                  pl.BlockSpec((1, 1, d), lambda i, e: (li, 0, 0))],
        out_specs=pl.BlockSpec((tm, d), lambda i, e: (i, 0)),
        out_shape=jax.ShapeDtypeStruct((m, d), _F32),
        scratch_shapes=[pltpu.VMEM((tm, d), _F32)],
        compiler_params=_cparams(2),
        name="moe_down",
    )(h, w_down_bf16, gate, x, ln_g, ln_b)


def _seg_scan(a, b, tstep, n_steps, stride):
    sh = stride
    for s in range(n_steps):
        keep = tstep >= (1 << s)
        a_sh = pltpu.roll(a, sh, 0)
        b_sh = pltpu.roll(b, sh, 0)
        b = jnp.where(keep, a * b_sh + b, b)
        a = jnp.where(keep, a * a_sh, a)
        sh *= 2
    return a, b


def _lru_body(gate_ref, rec_ref, h0_ref, c0_ref, cw_ref, cb_ref, gw_ref, gb_ref, lam_ref,
              y_ref, hout_ref, cout_ref, xp_ref, hc_ref, cv_ref, gwb_ref, *, nseg, seg, stride, pad):
    rows = nseg * seg
    bw = gw_ref.shape[1]

    @pl.when(pl.program_id(0) == 0)
    def _():
        xp_ref[:, 0:pad, :] = c0_ref[...]
        hc_ref[...] = h0_ref[...]
        gwb_ref[...] = gw_ref[...].astype(_BF16)

    for s in range(nseg):
        xp_ref[s, pad:pad + seg, :] = rec_ref[s * seg:(s + 1) * seg, :]
        acc = xp_ref[s, pad - 3 * stride:pad - 3 * stride + seg, :] * cw_ref[0:1, :]
        for j in range(1, CONV_W):
            off = pad - (CONV_W - 1 - j) * stride
            acc = acc + xp_ref[s, off:off + seg, :] * cw_ref[j:j + 1, :]
        cv_ref[s * seg:(s + 1) * seg, :] = acc + cb_ref[...]
        tail = xp_ref[s, seg:seg + pad, :]
        xp_ref[s, 0:pad, :] = tail
        cout_ref[s] = tail

    row = lax.broadcasted_iota(jnp.int32, (rows, bw), 0)
    if stride == 1:
        tstep = row & (seg - 1)
    else:
        tstep = row >> int(math.log2(stride))
    n_steps = int(math.log2(seg // stride))

    for n in range(LRU_BLOCKS):
        cs = slice(n * bw, (n + 1) * bw)
        cv = cv_ref[:, cs]
        gates = _dot(cv.astype(_BF16), gwb_ref[n]) + gb_ref[n]
        r = jax.nn.sigmoid(gates[:, :bw])
        ig = jax.nn.sigmoid(gates[:, bw:])
        log_a = -LRU_C * r * jax.nn.softplus(-lam_ref[:, cs])
        a = jnp.exp(log_a)
        mult = jnp.sqrt(-jnp.tanh(log_a) * (a * a + 1.0))
        a_cum, b_cum = _seg_scan(a, mult * ig * cv, tstep, n_steps, stride)
        hc = hc_ref[:, cs]
        if stride == 1:
            hprev = jnp.concatenate([jnp.broadcast_to(hc[s:s + 1, :], (seg, bw)) for s in range(nseg)], axis=0)
        else:
            hprev = jnp.concatenate([hc] * (seg // stride), axis=0)
        h = a_cum * hprev + b_cum
        y_ref[:, cs] = (h * jax.nn.gelu(gate_ref[:, cs])).astype(y_ref.dtype)
        if stride == 1:
            hc_ref[:, cs] = jnp.concatenate([h[(s + 1) * seg - 1:(s + 1) * seg, :] for s in range(nseg)], axis=0)
        else:
            hc_ref[:, cs] = h[rows - stride:, :]

    hout_ref[...] = hc_ref[...]


def _lru_seq(gu, h0, c0, conv_w, conv_b, gate_w, gate_b, lam, *, n_tiles, nseg, seg, stride, pad):
    rows = nseg * seg
    r = gu.shape[1] // 2
    bw = r // LRU_BLOCKS
    const2 = lambda i: (0, 0)
    const3 = lambda i: (0, 0, 0)
    return pl.pallas_call(
        functools.partial(_lru_body, nseg=nseg, seg=seg, stride=stride, pad=pad),
        grid=(n_tiles,),
        in_specs=[pl.BlockSpec((rows, r), lambda i: (i, 0)),
                  pl.BlockSpec((rows, r), lambda i: (i, 1)),
                  pl.BlockSpec((nseg * stride, r), const2),
                  pl.BlockSpec((nseg, pad, r), const3),
                  pl.BlockSpec((CONV_W, r), const2),
                  pl.BlockSpec((1, r), const2),
                  pl.BlockSpec((LRU_BLOCKS, bw, 2 * bw), const3),
                  pl.BlockSpec((LRU_BLOCKS, 1, 2 * bw), const3),
                  pl.BlockSpec((1, r), const2)],
        out_specs=[pl.BlockSpec((rows, r), lambda i: (i, 0)),
                   pl.BlockSpec((nseg * stride, r), const2),
                   pl.BlockSpec((nseg, pad, r), const3)],
        out_shape=[jax.ShapeDtypeStruct((gu.shape[0], r), _BF16),
                   jax.ShapeDtypeStruct((nseg * stride, r), _F32),
                   jax.ShapeDtypeStruct((nseg, pad, r), _F32)],
        scratch_shapes=[pltpu.VMEM((nseg, pad + seg, r), _F32),
                        pltpu.VMEM((nseg * stride, r), _F32),
                        pltpu.VMEM((rows, r), _F32),
                        pltpu.VMEM((LRU_BLOCKS, bw, 2 * bw), _BF16)],
        compiler_params=_cparams(1),
        name="lru_seq",
    )(gu, gu, h0, c0, conv_w, conv_b, gate_w, gate_b, lam)


def _l2norm(t):
    return t * lax.rsqrt(jnp.sum(t * t, axis=-1, keepdims=True) + RMS_EPS)


def _gdn_chunks(seqs, norm_g, *, c, hg):
    dk = GDN_DK
    qk_w = GDN_HEADS * dk
    r4 = hg * c
    lg_c = int(math.log2(c))
    row = lax.broadcasted_iota(jnp.int32, (r4, r4), 0)
    col = lax.broadcasted_iota(jnp.int32, (r4, r4), 1)
    blk = {1 << s: (row >> s) == (col >> s) for s in range(3, lg_c + 1)}
    incl = blk[c] & (row >= col)
    strict = blk[c] & (row > col)
    units = [(si, list(range(g0, g0 + hg))) for si in range(len(seqs)) for g0 in range(0, GDN_HEADS, hg)]
    blocks = [slice(hh * c, (hh + 1) * c) for hh in range(hg)]

    def stack(fn, hs, axis=0):
        return jnp.concatenate([fn(h) for h in hs], axis=axis)

    tr = lax.broadcasted_iota(jnp.int32, (c, c), 0)
    tc = lax.broadcasted_iota(jnp.int32, (c, c), 1)
    tri = jnp.where(tr >= tc, 1.0, 0.0)
    gc = [_dot(tri, sq[3], precision=_HI) for sq in seqs]
    gct = [jnp.concatenate([x, jnp.zeros((LANES - c, LANES), _F32)], axis=0).T for x in gc]

    q4 = [stack(lambda h: _l2norm(seqs[si][0](h * dk)) * (dk ** -0.5), hs) for si, hs in units]
    k4 = [stack(lambda h: _l2norm(seqs[si][0](qk_w + h * dk)), hs) for si, hs in units]
    v4 = [stack(lambda h: seqs[si][0](2 * qk_w + h * dk), hs) for si, hs in units]
    beta4 = [stack(lambda h: jnp.broadcast_to(seqs[si][2][:, h:h + 1], (c, dk)), hs) for si, hs in units]
    gc4 = [stack(lambda h: jnp.broadcast_to(gc[si][:, h:h + 1], (c, dk)), hs) for si, hs in units]
    gcol = [stack(lambda h: jnp.broadcast_to(gc[si][:, h:h + 1], (c, r4)), hs) for si, hs in units]
    grow = [stack(lambda h: gct[si][h:h + 1, 0:c], hs, axis=1) for si, hs in units]
    decay = [jnp.exp(jnp.where(incl, gcl - grw, -jnp.inf)) for gcl, grw in zip(gcol, grow)]
    kb4 = [k * bt for k, bt in zip(k4, beta4)]
    aq = [_dot_nt(jnp.concatenate([kb, q], axis=0).astype(_BF16), k.astype(_BF16)) for kb, q, k in zip(kb4, q4, k4)]
    a4 = [jnp.where(strict, x[:r4] * dc, 0.0) for x, dc in zip(aq, decay)]
    attn4 = [x[r4:] * dc for x, dc in zip(aq, decay)]
    c0 = min(c, 16)
    np_ = [-jnp.where(blk[c0], x, 0.0) for x in a4]
    e = list(np_)
    for _ in range(int(math.log2(c0)) - 1):
        np_ = [_bdot(x, x) for x in np_]
        e = [ei + ni + _bdot(ni, ei) for ei, ni in zip(e, np_)]
    size = c0
    while size < c:
        lo = [jnp.where(blk[2 * size] & ~blk[size], x, 0.0) for x in a4]
        m = [li + _bdot(li, ei) for li, ei in zip(lo, e)]
        e = [ei - (mi + _bdot(ei, mi)) for ei, mi in zip(e, m)]
        size *= 2
    egc4 = [jnp.exp(x) for x in gc4]
    rhs = [jnp.concatenate([v * bt, kb * eg], axis=1) for v, bt, kb, eg in zip(v4, beta4, kb4, egc4)]
    uw = [r + _bdot(ei, r) for r, ei in zip(rhs, e)]
    qg4 = [q * eg for q, eg in zip(q4, egc4)]
    ws_qs = [[_dot(jnp.concatenate([uw[u][rs, dk:], qg4[u][rs]], axis=0).astype(_BF16),
                   seqs[si][4][h].astype(_BF16)) for rs, h in zip(blocks, hs)]
             for u, (si, hs) in enumerate(units)]
    vn = [[uw[u][rs, :dk] - ws_qs[u][hh][:c] for hh, rs in enumerate(blocks)] for u in range(len(units))]
    o4 = [jnp.concatenate([x[c:] for x in ws_qs[u]], axis=0)
          + _bdot(attn4[u], jnp.concatenate(vn[u], axis=0)) for u in range(len(units))]
    outs = [[None] * GDN_HEADS for _ in seqs]
    new_state = [[None] * GDN_HEADS for _ in seqs]
    for u, (si, hs) in enumerate(units):
        z, state = seqs[si][1], seqs[si][4]
        for hh, (rs, h) in enumerate(zip(blocks, hs)):
            glast = gc[si][c - 1:c, h:h + 1]
            kd = k4[u][rs] * jnp.exp(glast - gc4[u][rs])
            new_state[si][h] = state[h] * jnp.exp(glast) + _dot_tn(kd.astype(_BF16), vn[u][hh].astype(_BF16))
            oh = o4[u][rs]
            oh = oh * lax.rsqrt(jnp.mean(oh * oh, axis=-1, keepdims=True) + RMS_EPS) * norm_g
            outs[si][h] = oh * jax.nn.silu(z[:, h * dk:(h + 1) * dk])
    return outs, new_state


def _gdn_gates(ba, alog, dtb):
    beta = jax.nn.sigmoid(ba[:, :LANES])
    g = -jnp.exp(alog) * jax.nn.softplus(ba[:, LANES:] + dtb)
    return beta, g


def _gdn_conv(xp_ref, s, x, cw_ref, *, pad, seg):
    xp_ref[s, pad:pad + seg, :] = x

    def block(col):
        cs = slice(col, col + GDN_DK)
        acc = xp_ref[s, pl.ds(pad - 3, seg), cs] * cw_ref[0:1, cs]
        for j in range(1, CONV_W):
            acc = acc + xp_ref[s, pl.ds(pad - (CONV_W - 1 - j), seg), cs] * cw_ref[j:j + 1, cs]
        return jax.nn.silu(acc)

    return block


def _gdn_prompt_body(qkv_ref, z_ref, ba_ref, cw_ref, alog_ref, dtb_ref, ng_ref,
                     o_ref, sout_ref, cout_ref, xp_ref, s_ref, *, nseg, seg, pad):
    @pl.when(pl.program_id(0) == 0)
    def _():
        xp_ref[...] = jnp.zeros(xp_ref.shape, _F32)
        s_ref[...] = jnp.zeros(s_ref.shape, _F32)

    def per_group(i, carry):
        bs = [i * GDN_SEQS_PER_STEP + k for k in range(GDN_SEQS_PER_STEP)]
        r0 = [pl.multiple_of(b * seg, seg) for b in bs]
        seqs = []
        for b, r in zip(bs, r0):
            qkv = _gdn_conv(xp_ref, b, qkv_ref[pl.ds(r, seg), :], cw_ref, pad=pad, seg=seg)
            beta, g = _gdn_gates(ba_ref[pl.ds(r, seg), :], alog_ref[...], dtb_ref[...])
            seqs.append((qkv, z_ref[pl.ds(r, seg), :], beta, g, [s_ref[b, h] for h in range(GDN_HEADS)]))
        outs, new_state = _gdn_chunks(seqs, ng_ref[...], c=seg, hg=256 // seg)
        for k, (b, r) in enumerate(zip(bs, r0)):
            tail = xp_ref[b, pl.ds(seg, pad), :]
            xp_ref[b, 0:pad, :] = tail
            cout_ref[b] = tail
            for h in range(GDN_HEADS):
                s_ref[b, h] = new_state[k][h]
                o_ref[pl.ds(r, seg), h * GDN_DV:(h + 1) * GDN_DV] = outs[k][h].astype(o_ref.dtype)
        return carry

    lax.fori_loop(0, nseg // GDN_SEQS_PER_STEP, per_group, 0)

    @pl.when(pl.program_id(0) == pl.num_programs(0) - 1)
    def _():
        sout_ref[...] = s_ref[...]


def _gdn_prompt(proj, ba, conv_w, alog, dtb, norm_g, *, n_tiles, nseg, seg):
    rows = nseg * seg
    cd = conv_w.shape[1]
    vw = GDN_HEADS * GDN_DV
    pad = 8
    const2 = lambda i: (0, 0)
    return pl.pallas_call(
        functools.partial(_gdn_prompt_body, nseg=nseg, seg=seg, pad=pad),
        grid=(n_tiles,),
        in_specs=[pl.BlockSpec((rows, cd), lambda i: (i, 0)),
                  pl.BlockSpec((rows, vw), lambda i: (i, cd // vw)),
                  pl.BlockSpec((rows, 2 * LANES), lambda i: (i, 0)),
                  pl.BlockSpec((CONV_W, cd), const2),
                  pl.BlockSpec((1, LANES), const2),
                  pl.BlockSpec((1, LANES), const2),
                  pl.BlockSpec((1, GDN_DV), const2)],
        out_specs=[pl.BlockSpec((rows, vw), lambda i: (i, 0)),
                   pl.BlockSpec((nseg, GDN_HEADS, GDN_DK, GDN_DV), lambda i: (0, 0, 0, 0)),
                   pl.BlockSpec((nseg, pad, cd), lambda i: (0, 0, 0))],
        out_shape=[jax.ShapeDtypeStruct((proj.shape[0], vw), _BF16),
                   jax.ShapeDtypeStruct((nseg, GDN_HEADS, GDN_DK, GDN_DV), _F32),
                   jax.ShapeDtypeStruct((nseg, pad, cd), _F32)],
        scratch_shapes=[pltpu.VMEM((nseg, pad + seg, cd), _F32),
                        pltpu.VMEM((nseg, GDN_HEADS, GDN_DK, GDN_DV), _F32)],
        compiler_params=_cparams(1),
        name="gdn_prompt",
    )(proj, proj, ba, conv_w, alog, dtb, norm_g)


def _gdn_sample_body(qkv_ref, z_ref, ba_ref, s0_ref, c0_ref, cw_ref, alog_ref, dtb_ref, ng_ref,
                     o_ref, sout_ref, cout_ref, xp_ref, *, nb, seg, steps, pad):
    def per_group(i, carry):
        bs = [i * GDN_SEQS_PER_STEP + k for k in range(GDN_SEQS_PER_STEP)]
        seqs = []
        for k, b in enumerate(bs):
            xp_ref[k, 0:pad, :] = c0_ref[b]
            qkv = _gdn_conv(xp_ref, k, qkv_ref[b], cw_ref, pad=pad, seg=seg)
            cout_ref[b] = xp_ref[k, pl.ds(steps, pad), :]
            beta, g = _gdn_gates(ba_ref[b], alog_ref[...], dtb_ref[...])
            live = lax.broadcasted_iota(jnp.int32, beta.shape, 0) < steps
            seqs.append((qkv, z_ref[b], jnp.where(live, beta, 0.0), jnp.where(live, g, 0.0),
                         [s0_ref[b, h] for h in range(GDN_HEADS)]))
        outs, new_state = _gdn_chunks(seqs, ng_ref[...], c=seg, hg=GDN_HEADS)
        for k, b in enumerate(bs):
            for h in range(GDN_HEADS):
                sout_ref[b, h] = new_state[k][h]
                o_ref[b, :, h * GDN_DV:(h + 1) * GDN_DV] = outs[k][h]
        return carry

    lax.fori_loop(0, nb // GDN_SEQS_PER_STEP, per_group, 0)


def _gdn_sample(proj, ba, s0, c0, conv_w, alog, dtb, norm_g, *, steps):
    nbatch, seg, _ = proj.shape
    cd = conv_w.shape[1]
    vw = GDN_HEADS * GDN_DV
    pad = c0.shape[1]
    nb = 8
    const2 = lambda i: (0, 0)
    return pl.pallas_call(
        functools.partial(_gdn_sample_body, nb=nb, seg=seg, steps=steps, pad=pad),
        grid=(nbatch // nb,),
        in_specs=[pl.BlockSpec((nb, seg, cd), lambda i: (i, 0, 0)),
                  pl.BlockSpec((nb, seg, vw), lambda i: (i, 0, cd // vw)),
                  pl.BlockSpec((nb, seg, 2 * LANES), lambda i: (i, 0, 0)),
                  pl.BlockSpec((nb, GDN_HEADS, GDN_DK, GDN_DV), lambda i: (i, 0, 0, 0)),
                  pl.BlockSpec((nb, pad, cd), lambda i: (i, 0, 0)),
                  pl.BlockSpec((CONV_W, cd), const2),
                  pl.BlockSpec((1, LANES), const2),
                  pl.BlockSpec((1, LANES), const2),
                  pl.BlockSpec((1, GDN_DV), const2)],
        out_specs=[pl.BlockSpec((nb, seg, vw), lambda i: (i, 0, 0)),
                   pl.BlockSpec((nb, GDN_HEADS, GDN_DK, GDN_DV), lambda i: (i, 0, 0, 0)),
                   pl.BlockSpec((nb, pad, cd), lambda i: (i, 0, 0))],
        out_shape=[jax.ShapeDtypeStruct((nbatch, seg, vw), _F32),
                   jax.ShapeDtypeStruct(s0.shape, _F32),
                   jax.ShapeDtypeStruct((nbatch, pad, cd), _F32)],
        scratch_shapes=[pltpu.VMEM((GDN_SEQS_PER_STEP, pad + seg, cd), _F32)],
        compiler_params=_cparams(1),
        name="gdn_sample",
    )(proj, proj, ba, s0, c0, conv_w, alog, dtb, norm_g)


def _softmax_rows(s):
    e = jnp.exp(s - jnp.max(s, axis=-1, keepdims=True))
    return e / jnp.sum(e, axis=-1, keepdims=True)


def _attn_prompt_body(q_ref, k_ref, v_ref, o_ref, *, nseg, seg, ntile):
    hd = q_ref.shape[1] // XA_HEADS
    scale = hd ** -0.5
    for b in range(nseg):
        runs = [slice((t * nseg + b) * seg, (t * nseg + b + 1) * seg) for t in range(ntile)]
        heads = [slice(h * hd, (h + 1) * hd) for h in range(XA_HEADS)]
        s = [_dot_nt(jnp.concatenate([q_ref[rs, cs] for rs in runs], axis=0), k_ref[b, :, cs]) * scale
             for cs in heads]
        p = [_softmax_rows(sh).astype(_BF16) for sh in s]
        o = [_dot(ph, v_ref[b, :, cs]).astype(o_ref.dtype) for ph, cs in zip(p, heads)]
        for oh, cs in zip(o, heads):
            for t, rs in enumerate(runs):
                o_ref[rs, cs] = oh[t * seg:(t + 1) * seg]


def _attn_prompt(q, mem_k, mem_v, *, n_tiles, nseg, seg):
    ntile = math.gcd(n_tiles, 4)
    rows = nseg * seg * ntile
    d = q.shape[1]
    mem = mem_k.shape[1]
    return pl.pallas_call(
        functools.partial(_attn_prompt_body, nseg=nseg, seg=seg, ntile=ntile),
        grid=(n_tiles // ntile,),
        in_specs=[pl.BlockSpec((rows, d), lambda i: (i, 0)),
                  pl.BlockSpec((nseg, mem, d), lambda i: (0, 0, 0)),
                  pl.BlockSpec((nseg, mem, d), lambda i: (0, 0, 0))],
        out_specs=pl.BlockSpec((rows, d), lambda i: (i, 0)),
        out_shape=jax.ShapeDtypeStruct(q.shape, _BF16),
        compiler_params=_cparams(1),
        name="attn_prompt",
    )(q, mem_k, mem_v)


def _attn_sample_body(q_ref, k_ref, v_ref, o_ref, *, nb):
    t8 = q_ref.shape[1]
    mem, nh, hd = k_ref.shape[2:]
    scale = hd ** -0.5
    qh = lax.broadcasted_iota(jnp.int32, (nh * t8, mem * nh), 0) // t8
    kh = lax.broadcasted_iota(jnp.int32, (nh * t8, mem * nh), 1) % nh
    own = qh == kh

    def per_seq(b, carry):
        q = q_ref[b]
        q2 = jnp.concatenate([q[:, h * hd:(h + 1) * hd] for h in range(nh)], axis=0).astype(_BF16)
        k2 = k_ref[0, b].reshape(mem * nh, hd).astype(_BF16)
        v2 = v_ref[0, b].reshape(mem * nh, hd).astype(_BF16)
        s = jnp.where(own, _dot_nt(q2, k2) * scale, -jnp.inf)
        o2 = _dot(_softmax_rows(s).astype(_BF16), v2)
        for h in range(nh):
            o_ref[b, :, h * hd:(h + 1) * hd] = o2[h * t8:(h + 1) * t8]
        return carry

    lax.fori_loop(0, nb, per_seq, 0)


def _attn_sample(q, cache_k, cache_v, layer):
    nbatch, t8, d = q.shape
    mem, nh, hd = cache_k.shape[2:]
    nb = 8
    return pl.pallas_call(
        functools.partial(_attn_sample_body, nb=nb),
        grid=(nbatch // nb,),
        in_specs=[pl.BlockSpec((nb, t8, d), lambda i: (i, 0, 0)),
                  pl.BlockSpec((1, nb, mem, nh, hd), lambda i: (layer, i, 0, 0, 0)),
                  pl.BlockSpec((1, nb, mem, nh, hd), lambda i: (layer, i, 0, 0, 0))],
        out_specs=pl.BlockSpec((nb, t8, d), lambda i: (i, 0, 0)),
        out_shape=jax.ShapeDtypeStruct(q.shape, _F32),
        compiler_params=_cparams(1),
        name="attn_sample",
    )(q, cache_k, cache_v)


def kernel(x_prompt, x_sample, mem_prompt, state_lru_h, state_lru_conv, state_gdn_s, state_gdn_conv,
           cache_mem_k, cache_mem_v, ln_g, ln_b, xa_wq, xa_wk, xa_wv, xa_wo,
           lru_w_in, lru_conv_w, lru_conv_b, lru_gate_w, lru_gate_b, lru_lambda, lru_w_out,
           gdn_w_in, gdn_conv_w, gdn_a_log, gdn_dt_bias, gdn_norm_g, gdn_w_out,
           ffn_w_gu, ffn_w_down, moe_router, moe_w_gu, moe_w_down):
    depth = ln_g.shape[0]
    bp, seq, d = x_prompt.shape
    bs, ts, _ = x_sample.shape
    mem = mem_prompt.shape[1]
    alpha = (2 * depth) ** 0.25
    seg = GDN_CHUNK
    nchunk = seq // seg
    mp, ms = bp * seq, bs * ts
    assert bp * seg == ROW_TILE and ms == ROW_TILE and seq % seg == 0 and ts >= CONV_W - 1
    t8 = 8
    pad_p = 8
    pad_s = (CONV_W - 1) * bs
    r = lru_w_in.shape[2] // 2
    cd = gdn_conv_w.shape[2]
    vw = GDN_HEADS * GDN_DV
    d_ff = ffn_w_gu.shape[2] // 2
    d_fe = moe_w_gu.shape[3] // 2

    xp = x_prompt.reshape(bp, nchunk, seg, d).transpose(1, 0, 2, 3).reshape(mp, d)
    x = jnp.concatenate([xp, x_sample.reshape(ms, d)], axis=0)

    memx = mem_prompt.reshape(bp * mem, d)
    mem_k_p = _mm(memx, xa_wk, ng=depth, tn=1024, out_dtype=_F32).reshape(depth, bp, mem, d)
    mem_v_p = _mm(memx, xa_wv, ng=depth, tn=1024, out_dtype=_F32).reshape(depth, bp, mem, d)
    mem_k_b = mem_k_p.astype(_BF16)
    mem_v_b = mem_v_p.astype(_BF16)

    ln_g3 = ln_g.reshape(depth * 3, 1, d)
    moe_w_gu3 = moe_w_gu.reshape((-1,) + moe_w_gu.shape[2:])
    ln_b3 = ln_b.reshape(depth * 3, 1, d)

    def put_sample(full, sample_rows):
        return lax.dynamic_update_slice(full, sample_rows.astype(full.dtype), (mp, 0))

    def to_time_major(a):
        return a.reshape(bs, ts, -1).transpose(1, 0, 2).reshape(ms, -1)

    def to_batch_major(a):
        return a.reshape(ts, bs, -1).transpose(1, 0, 2).reshape(ms, -1)

    def pad_t(a):
        a = a.reshape(bs, ts, -1)
        return jnp.pad(a, ((0, 0), (0, t8 - ts), (0, 0)))

    lru_h_p, lru_c_p, lru_h_s, lru_c_s = [], [], [], []
    gdn_s_p, gdn_c_p, gdn_s_s, gdn_c_s = [], [], [], []

    for i in range(depth):
        j = i // 2
        if i % 2 == 0:
            gu = _mm(x, lru_w_in, g0=j, tn=1024, out_dtype=_F32)[0]
            lam = lru_lambda[j].reshape(1, r)
            cb = lru_conv_b[j].reshape(1, r)
            gb = lru_gate_b[j].reshape(LRU_BLOCKS, 1, -1)
            y, h_p, c_p = _lru_seq(gu, jnp.zeros((bp, r), _F32), jnp.zeros((bp, pad_p, r), _F32),
                                   lru_conv_w[j], cb, lru_gate_w[j], gb, lam,
                                   n_tiles=nchunk, nseg=bp, seg=seg, stride=1, pad=pad_p)
            gu_s = to_time_major(gu[mp:])
            c0_s = state_lru_conv[j].transpose(1, 0, 2).reshape(1, pad_s, r)
            y_s, h_s, c_s = _lru_seq(gu_s, state_lru_h[j], c0_s, lru_conv_w[j], cb, lru_gate_w[j], gb, lam,
                                     n_tiles=1, nseg=1, seg=ms, stride=bs, pad=pad_s)
            y = put_sample(y, to_batch_major(y_s))
            lru_h_p.append(h_p)
            lru_c_p.append(c_p[:, pad_p - (CONV_W - 1):, :])
            lru_h_s.append(h_s)
            lru_c_s.append(c_s.reshape(CONV_W - 1, bs, r).transpose(1, 0, 2))
            w_out, wi = lru_w_out, j
        else:
            proj = _mm(x, gdn_w_in, g0=j, n=cd + vw, tn=1024, out_dtype=_F32)[0]
            w_tail = gdn_w_in[j, :, cd + vw:]
            w_ba = jnp.zeros((1, d, 2 * LANES), _F32)
            w_ba = w_ba.at[0, :, :GDN_HEADS].set(w_tail[:, :GDN_HEADS])
            w_ba = w_ba.at[0, :, LANES:LANES + GDN_HEADS].set(w_tail[:, GDN_HEADS:])
            ba = _mm(x, w_ba, tn=2 * LANES, out_dtype=_F32)[0]
            alog = jnp.zeros((1, LANES), _F32).at[0, :GDN_HEADS].set(gdn_a_log[j])
            dtb = jnp.zeros((1, LANES), _F32).at[0, :GDN_HEADS].set(gdn_dt_bias[j])
            ng = gdn_norm_g[j].reshape(1, GDN_DV)
            y, s_p, c_p = _gdn_prompt(proj, ba, gdn_conv_w[j], alog, dtb, ng, n_tiles=nchunk, nseg=bp, seg=seg)
            c0_s = jnp.pad(state_gdn_conv[j], ((0, 0), (pad_p - (CONV_W - 1), 0), (0, 0)))
            y_s, s_s, c_s = _gdn_sample(pad_t(proj[mp:]), pad_t(ba[mp:]), state_gdn_s[j], c0_s,
                                        gdn_conv_w[j], alog, dtb, ng, steps=ts)
            y = put_sample(y, y_s[:, :ts, :].reshape(ms, vw))
            gdn_s_p.append(s_p)
            gdn_c_p.append(c_p[:, pad_p - (CONV_W - 1):, :])
            gdn_s_s.append(s_s)
            gdn_c_s.append(c_s[:, pad_p - (CONV_W - 1):, :])
            w_out, wi = gdn_w_out, j
        x = _mm_ln(y, w_out, wi, x, ln_g3, ln_b3, 3 * i, alpha=alpha)

        q = _mm(x, xa_wq, g0=i, tn=1024, out_dtype=_BF16)[0]
        o = _attn_prompt(q, mem_k_b[i], mem_v_b[i], n_tiles=nchunk, nseg=bp, seg=seg)
        o_s = _attn_sample(pad_t(q[mp:].astype(_F32)), cache_mem_k, cache_mem_v, i)
        o = put_sample(o, o_s[:, :ts, :].reshape(ms, d))
        x = _mm_ln(o, xa_wo, i, x, ln_g3, ln_b3, 3 * i + 1, alpha=alpha)

        if i % 2 == 0:
            hdn = _gate_up(x, ffn_w_gu, g0=j, ng=1, f=d_ff, tf=512)[0]
            x = _mm_ln(hdn, ffn_w_down, j, x, ln_g3, ln_b3, 3 * i + 2, alpha=alpha)
        else:
            router_pad = jnp.zeros((d, LANES), _F32).at[:, :N_EXPERTS].set(moe_router[j])
            gate = _router(x, router_pad)
            hdn = _gate_up(x, moe_w_gu3, g0=j * N_EXPERTS, ng=N_EXPERTS, f=d_fe, tf=512)
            x = _moe_down(hdn, moe_w_down[j].astype(_BF16), gate, x, ln_g3, ln_b3, 3 * i + 2, alpha=alpha)

    y_prompt = x[:mp].reshape(nchunk, bp, seg, d).transpose(1, 0, 2, 3).reshape(bp, seq, d)
    y_sample = x[mp:].reshape(bs, ts, d)
    hd = d // XA_HEADS
    return (y_prompt, y_sample,
            jnp.stack(lru_h_p), jnp.stack(lru_c_p), jnp.stack(gdn_s_p), jnp.stack(gdn_c_p),
            mem_k_p.reshape(depth, bp, mem, XA_HEADS, hd), mem_v_p.reshape(depth, bp, mem, XA_HEADS, hd),
            jnp.stack(lru_h_s), jnp.stack(lru_c_s), jnp.stack(gdn_s_s), jnp.stack(gdn_c_s))
```

```python
import functools
import math

import jax
import jax.numpy as jnp
from jax import lax
from jax.experimental import pallas as pl
from jax.experimental.pallas import tpu as pltpu

_F32 = jnp.float32
_BF16 = jnp.bfloat16
_HI = lax.Precision.HIGHEST

CONV_W = 4
LRU_BLOCKS = 4
LRU_C = 8.0
GDN_HEADS = 8
GDN_DK = 128
GDN_DV = 128
GDN_CHUNK = 64
GDN_SEQS_PER_STEP = 2
XA_HEADS = 4
N_EXPERTS = 8
IDX_SHIFT = 15
LN_EPS = 1e-5
RMS_EPS = 1e-6
LANES = 128
ROW_TILE = 512
VMEM_LIMIT_MB = 56


def _cparams(ndim, vmem_mb=VMEM_LIMIT_MB):
    return pltpu.CompilerParams(dimension_semantics=("arbitrary",) * ndim,
                                vmem_limit_bytes=vmem_mb * 1024 * 1024)


def _dot(a, b, precision=None):
    return jnp.dot(a, b, preferred_element_type=_F32, precision=precision)


def _bdot(a, b):
    return _dot(a.astype(_BF16), b.astype(_BF16))


def _dot_nt(a, b):
    return lax.dot_general(a, b, (((1,), (1,)), ((), ())), preferred_element_type=_F32)


def _dot_tn(a, b):
    return lax.dot_general(a, b, (((0,), (0,)), ((), ())), preferred_element_type=_F32)


def _layer_norm(v, g, b):
    mu = jnp.mean(v, axis=-1, keepdims=True)
    d = v - mu
    var = jnp.mean(d * d, axis=-1, keepdims=True)
    return d * lax.rsqrt(var + LN_EPS) * g + b


def _mm_body(x_ref, w_ref, o_ref, wb_ref):
    @pl.when(pl.program_id(2) == 0)
    def _():
        wb_ref[...] = w_ref[0].astype(_BF16)

    o_ref[0] = _dot(x_ref[...].astype(_BF16), wb_ref[...]).astype(o_ref.dtype)


def _mm(x, w, *, g0=0, ng=1, n0=0, n=None, tn, out_dtype):
    m, k = x.shape
    n = w.shape[2] if n is None else n
    tm = min(ROW_TILE, m)
    nb0 = n0 // tn
    return pl.pallas_call(
        _mm_body,
        grid=(ng, n // tn, m // tm),
        in_specs=[pl.BlockSpec((tm, k), lambda g, j, i: (i, 0)),
                  pl.BlockSpec((1, k, tn), lambda g, j, i: (g + g0, 0, j + nb0))],
        out_specs=pl.BlockSpec((1, tm, tn), lambda g, j, i: (g, i, j)),
        out_shape=jax.ShapeDtypeStruct((ng, m, n), out_dtype),
        scratch_shapes=[pltpu.VMEM((k, tn), _BF16)],
        compiler_params=_cparams(3),
        name="mm",
    )(x, w)


def _mm_ln_body(y_ref, y2_ref, w_ref, x_ref, g_ref, b_ref, o_ref, wb_ref, *, alpha, n1):
    i = pl.program_id(0)

    @pl.when(i == 0)
    def _():
        wb_ref[...] = w_ref[0].astype(_BF16)

    def emit(src_ref):
        mix = _dot(src_ref[...].astype(_BF16), wb_ref[...])
        o_ref[...] = _layer_norm(alpha * x_ref[...] + mix, g_ref[0], b_ref[0])

    if n1 is None:
        emit(y_ref)
    else:
        pl.when(i < n1)(lambda: emit(y_ref))
        pl.when(i >= n1)(lambda: emit(y2_ref))


def _mm_ln(y, y2, w, gi, x, ln_g, ln_b, li, *, alpha):
    m, d = x.shape
    k = y.shape[1]
    tm = min(ROW_TILE, m)
    n1 = y.shape[0] // tm
    if y2 is None:
        y2 = y
    assert n1 * tm == y.shape[0] and (n1 == m // tm or n1 * tm + y2.shape[0] == m)
    return pl.pallas_call(
        functools.partial(_mm_ln_body, alpha=alpha, n1=None if n1 == m // tm else n1),
        grid=(m // tm,),
        in_specs=[pl.BlockSpec((tm, k), lambda i: (jnp.minimum(i, n1 - 1), 0)),
                  pl.BlockSpec((tm, k), lambda i: (jnp.maximum(i - n1, 0), 0)),
                  pl.BlockSpec((1, k, d), lambda i: (gi, 0, 0), pipeline_mode=pl.Buffered(1)),
                  pl.BlockSpec((tm, d), lambda i: (i, 0)),
                  pl.BlockSpec((1, 1, d), lambda i: (li, 0, 0)),
                  pl.BlockSpec((1, 1, d), lambda i: (li, 0, 0))],
        out_specs=pl.BlockSpec((tm, d), lambda i: (i, 0)),
        out_shape=jax.ShapeDtypeStruct((m, d), _F32),
        scratch_shapes=[pltpu.VMEM((k, d), _BF16)],
        compiler_params=_cparams(1),
        name="mm_ln",
    )(y, y2, w, x, ln_g, ln_b)


def _gu_body(x_ref, wg_ref, wu_ref, o_ref, wgb_ref, wub_ref):
    @pl.when(pl.program_id(2) == 0)
    def _():
        wgb_ref[...] = wg_ref[0].astype(_BF16)
        wub_ref[...] = wu_ref[0].astype(_BF16)

    xb = x_ref[...].astype(_BF16)
    g = _dot(xb, wgb_ref[...])
    u = _dot(xb, wub_ref[...])
    o_ref[0] = (jax.nn.silu(g) * u).astype(o_ref.dtype)


def _gate_up(x, w_gu, *, g0, ng, f, tf):
    m, k = x.shape
    tm = min(ROW_TILE, m)
    nf = f // tf
    return pl.pallas_call(
        _gu_body,
        grid=(ng, nf, m // tm),
        in_specs=[pl.BlockSpec((tm, k), lambda g, j, i: (i, 0)),
                  pl.BlockSpec((1, k, tf), lambda g, j, i: (g + g0, 0, j)),
                  pl.BlockSpec((1, k, tf), lambda g, j, i: (g + g0, 0, j + nf))],
        out_specs=pl.BlockSpec((1, tm, tf), lambda g, j, i: (g, i, j)),
        out_shape=jax.ShapeDtypeStruct((ng, m, f), _BF16),
        scratch_shapes=[pltpu.VMEM((k, tf), _BF16), pltpu.VMEM((k, tf), _BF16)],
        compiler_params=_cparams(3),
        name="gate_up",
    )(x, w_gu, w_gu)


def _router_body(x_ref, r_ref, gate_ref):
    logits = _dot(x_ref[...], r_ref[...], precision=_HI)
    lane = lax.broadcasted_iota(jnp.int32, logits.shape, 1).astype(_F32)
    l1 = jnp.where(lane < N_EXPERTS, logits, -jnp.inf)
    m1 = jnp.max(l1, axis=-1, keepdims=True)
    i1 = jnp.min(jnp.where(l1 == m1, lane, float(LANES)), axis=-1, keepdims=True)
    l2 = jnp.where(lane == i1, -jnp.inf, l1)
    m2 = jnp.max(l2, axis=-1, keepdims=True)
    i2 = jnp.min(jnp.where(l2 == m2, lane, float(LANES)), axis=-1, keepdims=True)
    e2 = jnp.exp(m2 - m1)
    den = 1.0 + e2
    gate_ref[...] = (jnp.where(lane == 0.0, i1, 0.0) + jnp.where(lane == 1.0, i2, 0.0)
                     + jnp.where(lane == 2.0, 1.0 / den, 0.0) + jnp.where(lane == 3.0, e2 / den, 0.0))


def _router(x, router_pad):
    m, k = x.shape
    tm = min(ROW_TILE, m)
    return pl.pallas_call(
        _router_body,
        grid=(m // tm,),
        in_specs=[pl.BlockSpec((tm, k), lambda i: (i, 0)),
                  pl.BlockSpec((k, LANES), lambda i: (0, 0))],
        out_specs=pl.BlockSpec((tm, LANES), lambda i: (i, 0)),
        out_shape=jax.ShapeDtypeStruct((m, LANES), _F32),
        compiler_params=_cparams(1),
        name="router",
    )(x, router_pad)


def _route_plan(rt, tm):
    m = rt.shape[0]
    ef = rt[:, :2].astype(jnp.int32).reshape(-1)
    wf = rt[:, 2:4].reshape(-1)
    onehot = (ef[:, None] == jnp.arange(N_EXPERTS, dtype=jnp.int32)[None, :]).astype(jnp.int32)
    csum = jnp.cumsum(onehot, axis=0)
    cnt = csum[-1]
    rank = jnp.sum((csum - onehot) * onehot, axis=1)
    cnt_p = ((cnt + tm - 1) // tm) * tm
    end = jnp.cumsum(cnt_p)
    start = end - cnt_p
    dest = start[ef] + rank
    n_tiles = (2 * m) // tm + N_EXPERTS
    rows = n_tiles * tm
    pair = jnp.zeros((rows,), jnp.int32).at[dest].set(jnp.arange(2 * m, dtype=jnp.int32))
    tile_start = jnp.arange(n_tiles, dtype=jnp.int32) * tm
    texp = jnp.sum((tile_start[:, None] >= end[None, :]).astype(jnp.int32), axis=1)
    used = tile_start < end[-1]
    last = jnp.maximum(end[-1] // tm - 1, 0)
    texp = jnp.where(used, texp, texp[last])
    nval = jnp.where(used, jnp.clip(start[texp] + cnt[texp] - tile_start, 0, tm), 0)
    live = (jnp.arange(rows, dtype=jnp.int32) - jnp.repeat(tile_start, tm)) < jnp.repeat(nval, tm)
    tok, slot = pair // 2, pair % 2
    idx = jnp.where(live, tok | ((slot * m + tok) << IDX_SHIFT), 0)
    rw = jnp.where(live, wf[pair], 0.0).reshape(rows, 1)
    return texp, nval, idx, rw


def _moe_routed_body(texp_ref, nval_ref, idx_ref, x_hbm, rw_ref, wg_ref, wu_ref, wd_ref, out_hbm,
                     xs_ref, ys_ref, gsem, ssem):
    i = pl.program_id(0)
    tm = xs_ref.shape[0]
    n = nval_ref[i]
    base = i * tm

    @pl.when(i == 0)
    def _():
        xs_ref[...] = jnp.zeros(xs_ref.shape, _F32)

    def row_in(r):
        tok = idx_ref[base + r] & ((1 << IDX_SHIFT) - 1)
        return pltpu.make_async_copy(x_hbm.at[pl.ds(tok, 1)], xs_ref.at[pl.ds(r, 1)], gsem)

    def row_out(r):
        dst = idx_ref[base + r] >> IDX_SHIFT
        return pltpu.make_async_copy(ys_ref.at[pl.ds(r, 1)], out_hbm.at[pl.ds(dst, 1)], ssem)

    def start_all(copy):
        def body(r, c):
            copy(r).start()
            return c
        lax.fori_loop(0, n, body, 0)

    def wait_all(copy):
        def body(r, c):
            copy(r).wait()
            return c
        lax.fori_loop(0, n, body, 0)

    @pl.when(n > 0)
    def _():
        start_all(row_in)
        wait_all(row_in)
        xb = xs_ref[...].astype(_BF16)
        h = (jax.nn.silu(_dot(xb, wg_ref[0])) * _dot(xb, wu_ref[0])).astype(_BF16)
        ys_ref[...] = _dot(h, wd_ref[0]) * rw_ref[...]
        start_all(row_out)
        wait_all(row_out)


def _moe_routed(x, plan, w_gu_bf16, w_down_bf16, e0, *, tm):
    texp, nval, idx, rw = plan
    m, d = x.shape
    f = w_down_bf16.shape[1]
    n_tiles = texp.shape[0]
    return pl.pallas_call(
        _moe_routed_body,
        grid_spec=pltpu.PrefetchScalarGridSpec(
            num_scalar_prefetch=3,
            grid=(n_tiles,),
            in_specs=[pl.BlockSpec(memory_space=pl.ANY),
                      pl.BlockSpec((tm, 1), lambda i, te, nv, ix: (i, 0)),
                      pl.BlockSpec((1, d, f), lambda i, te, nv, ix: (te[i] + e0, 0, 0)),
                      pl.BlockSpec((1, d, f), lambda i, te, nv, ix: (te[i] + e0, 0, 1)),
                      pl.BlockSpec((1, f, d), lambda i, te, nv, ix: (te[i] + e0, 0, 0))],
            out_specs=pl.BlockSpec(memory_space=pl.ANY),
            scratch_shapes=[pltpu.VMEM((tm, d), _F32), pltpu.VMEM((tm, d), _F32),
                            pltpu.SemaphoreType.DMA, pltpu.SemaphoreType.DMA]),
        out_shape=jax.ShapeDtypeStruct((2 * m, d), _F32),
        compiler_params=_cparams(1),
        name="moe_routed",
    )(texp, nval, idx, x, rw, w_gu_bf16, w_gu_bf16, w_down_bf16)


def _combine_ln_body(a_ref, b_ref, x_ref, g_ref, bb_ref, o_ref, *, alpha):
    o_ref[...] = _layer_norm(alpha * x_ref[...] + (a_ref[...] + b_ref[...]), g_ref[0], bb_ref[0])


def _combine_ln(y2, x, ln_g, ln_b, li, *, alpha):
    m, d = x.shape
    tm = min(ROW_TILE, m)
    nt = m // tm
    return pl.pallas_call(
        functools.partial(_combine_ln_body, alpha=alpha),
        grid=(nt,),
        in_specs=[pl.BlockSpec((tm, d), lambda i: (i, 0)),
                  pl.BlockSpec((tm, d), lambda i: (i + nt, 0)),
                  pl.BlockSpec((tm, d), lambda i: (i, 0)),
                  pl.BlockSpec((1, 1, d), lambda i: (li, 0, 0)),
                  pl.BlockSpec((1, 1, d), lambda i: (li, 0, 0))],
        out_specs=pl.BlockSpec((tm, d), lambda i: (i, 0)),
        out_shape=jax.ShapeDtypeStruct((m, d), _F32),
        compiler_params=_cparams(1),
        name="combine_ln",
    )(y2, y2, x, ln_g, ln_b)


def _seg_scan(a, b, tstep, n_steps, stride):
    sh = stride
    for s in range(n_steps):
        keep = tstep >= (1 << s)
        a_sh = pltpu.roll(a, sh, 0)
        b_sh = pltpu.roll(b, sh, 0)
        b = jnp.where(keep, a * b_sh + b, b)
        a = jnp.where(keep, a * a_sh, a)
        sh *= 2
    return a, b


def _lru_body(gate_ref, rec_ref, h0_ref, c0_ref, cw_ref, cb_ref, gw_ref, gb_ref, lam_ref,
              y_ref, hout_ref, cout_ref, xp_ref, hc_ref, cv_ref, gwb_ref, *, nseg, seg, stride, pad):
    rows = nseg * seg
    bw = gw_ref.shape[1]

    @pl.when(pl.program_id(0) == 0)
    def _():
        xp_ref[:, 0:pad, :] = c0_ref[...]
        hc_ref[...] = h0_ref[...]
        gwb_ref[...] = gw_ref[...].astype(_BF16)

    for s in range(nseg):
        xp_ref[s, pad:pad + seg, :] = rec_ref[s * seg:(s + 1) * seg, :]
        acc = xp_ref[s, pad - 3 * stride:pad - 3 * stride + seg, :] * cw_ref[0:1, :]
        for j in range(1, CONV_W):
            off = pad - (CONV_W - 1 - j) * stride
            acc = acc + xp_ref[s, off:off + seg, :] * cw_ref[j:j + 1, :]
        cv_ref[s * seg:(s + 1) * seg, :] = acc + cb_ref[...]
        tail = xp_ref[s, seg:seg + pad, :]
        xp_ref[s, 0:pad, :] = tail
        cout_ref[s] = tail

    row = lax.broadcasted_iota(jnp.int32, (rows, bw), 0)
    if stride == 1:
        tstep = row & (seg - 1)
    else:
        tstep = row >> int(math.log2(stride))
    n_steps = int(math.log2(seg // stride))

    for n in range(LRU_BLOCKS):
        cs = slice(n * bw, (n + 1) * bw)
        cv = cv_ref[:, cs]
        gates = _dot(cv.astype(_BF16), gwb_ref[n]) + gb_ref[n]
        r = jax.nn.sigmoid(gates[:, :bw])
        ig = jax.nn.sigmoid(gates[:, bw:])
        log_a = -LRU_C * r * jax.nn.softplus(-lam_ref[:, cs])
        a = jnp.exp(log_a)
        mult = jnp.sqrt(-jnp.tanh(log_a) * (a * a + 1.0))
        a_cum, b_cum = _seg_scan(a, mult * ig * cv, tstep, n_steps, stride)
        hc = hc_ref[:, cs]
        if stride == 1:
            hprev = jnp.concatenate([jnp.broadcast_to(hc[s:s + 1, :], (seg, bw)) for s in range(nseg)], axis=0)
        else:
            hprev = jnp.concatenate([hc] * (seg // stride), axis=0)
        h = a_cum * hprev + b_cum
        y_ref[:, cs] = (h * jax.nn.gelu(gate_ref[:, cs])).astype(y_ref.dtype)
        if stride == 1:
            hc_ref[:, cs] = jnp.concatenate([h[(s + 1) * seg - 1:(s + 1) * seg, :] for s in range(nseg)], axis=0)
        else:
            hc_ref[:, cs] = h[rows - stride:, :]

    hout_ref[...] = hc_ref[...]


def _lru_seq(gu, h0, c0, conv_w, conv_b, gate_w, gate_b, lam, *, n_tiles, nseg, seg, stride, pad):
    rows = nseg * seg
    r = gu.shape[1] // 2
    bw = r // LRU_BLOCKS
    const2 = lambda i: (0, 0)
    const3 = lambda i: (0, 0, 0)
    return pl.pallas_call(
        functools.partial(_lru_body, nseg=nseg, seg=seg, stride=stride, pad=pad),
        grid=(n_tiles,),
        in_specs=[pl.BlockSpec((rows, r), lambda i: (i, 0)),
                  pl.BlockSpec((rows, r), lambda i: (i, 1)),
                  pl.BlockSpec((nseg * stride, r), const2),
                  pl.BlockSpec((nseg, pad, r), const3),
                  pl.BlockSpec((CONV_W, r), const2),
                  pl.BlockSpec((1, r), const2),
                  pl.BlockSpec((LRU_BLOCKS, bw, 2 * bw), const3),
                  pl.BlockSpec((LRU_BLOCKS, 1, 2 * bw), const3),
                  pl.BlockSpec((1, r), const2)],
        out_specs=[pl.BlockSpec((rows, r), lambda i: (i, 0)),
                   pl.BlockSpec((nseg * stride, r), const2),
                   pl.BlockSpec((nseg, pad, r), const3)],
        out_shape=[jax.ShapeDtypeStruct((n_tiles * rows, r), _BF16),
                   jax.ShapeDtypeStruct((nseg * stride, r), _F32),
                   jax.ShapeDtypeStruct((nseg, pad, r), _F32)],
        scratch_shapes=[pltpu.VMEM((nseg, pad + seg, r), _F32),
                        pltpu.VMEM((nseg * stride, r), _F32),
                        pltpu.VMEM((rows, r), _F32),
                        pltpu.VMEM((LRU_BLOCKS, bw, 2 * bw), _BF16)],
        compiler_params=_cparams(1),
        name="lru_seq",
    )(gu, gu, h0, c0, conv_w, conv_b, gate_w, gate_b, lam)


def _l2norm(t):
    return t * lax.rsqrt(jnp.sum(t * t, axis=-1, keepdims=True) + RMS_EPS)


def _gdn_chunks(seqs, norm_g, *, c, hg):
    dk = GDN_DK
    qk_w = GDN_HEADS * dk
    r4 = hg * c
    lg_c = int(math.log2(c))
    row = lax.broadcasted_iota(jnp.int32, (r4, r4), 0)
    col = lax.broadcasted_iota(jnp.int32, (r4, r4), 1)
    blk = {1 << s: (row >> s) == (col >> s) for s in range(3, lg_c + 1)}
    incl = blk[c] & (row >= col)
    strict = blk[c] & (row > col)
    units = [(si, list(range(g0, g0 + hg))) for si in range(len(seqs)) for g0 in range(0, GDN_HEADS, hg)]
    blocks = [slice(hh * c, (hh + 1) * c) for hh in range(hg)]

    def stack(fn, hs, axis=0):
        return jnp.concatenate([fn(h) for h in hs], axis=axis)

    tr = lax.broadcasted_iota(jnp.int32, (c, c), 0)
    tc = lax.broadcasted_iota(jnp.int32, (c, c), 1)
    tri = jnp.where(tr >= tc, 1.0, 0.0)
    gc = [_dot(tri, sq[3], precision=_HI) for sq in seqs]
    gct = [jnp.concatenate([x, jnp.zeros((LANES - c, LANES), _F32)], axis=0).T for x in gc]

    q4 = [stack(lambda h: _l2norm(seqs[si][0](h * dk)) * (dk ** -0.5), hs) for si, hs in units]
    k4 = [stack(lambda h: _l2norm(seqs[si][0](qk_w + h * dk)), hs) for si, hs in units]
    v4 = [stack(lambda h: seqs[si][0](2 * qk_w + h * dk), hs) for si, hs in units]
    beta4 = [stack(lambda h: jnp.broadcast_to(seqs[si][2][:, h:h + 1], (c, dk)), hs) for si, hs in units]
    gc4 = [stack(lambda h: jnp.broadcast_to(gc[si][:, h:h + 1], (c, dk)), hs) for si, hs in units]
    gcol = [stack(lambda h: jnp.broadcast_to(gc[si][:, h:h + 1], (c, r4)), hs) for si, hs in units]
    grow = [stack(lambda h: gct[si][h:h + 1, 0:c], hs, axis=1) for si, hs in units]
    decay = [jnp.exp(jnp.where(incl, gcl - grw, -jnp.inf)) for gcl, grw in zip(gcol, grow)]
    kb4 = [k * bt for k, bt in zip(k4, beta4)]
    aq = [_dot_nt(jnp.concatenate([kb, q], axis=0).astype(_BF16), k.astype(_BF16)) for kb, q, k in zip(kb4, q4, k4)]
    a4 = [jnp.where(strict, x[:r4] * dc, 0.0) for x, dc in zip(aq, decay)]
    attn4 = [x[r4:] * dc for x, dc in zip(aq, decay)]
    c0 = min(c, 16)
    np_ = [-jnp.where(blk[c0], x, 0.0) for x in a4]
    e = list(np_)
    for _ in range(int(math.log2(c0)) - 1):
        np_ = [_bdot(x, x) for x in np_]
        e = [ei + ni + _bdot(ni, ei) for ei, ni in zip(e, np_)]
    size = c0
    while size < c:
        lo = [jnp.where(blk[2 * size] & ~blk[size], x, 0.0) for x in a4]
        m = [li + _bdot(li, ei) for li, ei in zip(lo, e)]
        e = [ei - (mi + _bdot(ei, mi)) for ei, mi in zip(e, m)]
        size *= 2
    egc4 = [jnp.exp(x) for x in gc4]
    rhs = [jnp.concatenate([v * bt, kb * eg], axis=1) for v, bt, kb, eg in zip(v4, beta4, kb4, egc4)]
    uw = [r + _bdot(ei, r) for r, ei in zip(rhs, e)]
    qg4 = [q * eg for q, eg in zip(q4, egc4)]
    ws_qs = [[_dot(jnp.concatenate([uw[u][rs, dk:], qg4[u][rs]], axis=0).astype(_BF16),
                   seqs[si][4][h].astype(_BF16)) for rs, h in zip(blocks, hs)]
             for u, (si, hs) in enumerate(units)]
    vn = [[uw[u][rs, :dk] - ws_qs[u][hh][:c] for hh, rs in enumerate(blocks)] for u in range(len(units))]
    o4 = [jnp.concatenate([x[c:] for x in ws_qs[u]], axis=0)
          + _bdot(attn4[u], jnp.concatenate(vn[u], axis=0)) for u in range(len(units))]
    outs = [[None] * GDN_HEADS for _ in seqs]
    new_state = [[None] * GDN_HEADS for _ in seqs]
    for u, (si, hs) in enumerate(units):
        z, state = seqs[si][1], seqs[si][4]
        for hh, (rs, h) in enumerate(zip(blocks, hs)):
            glast = gc[si][c - 1:c, h:h + 1]
            kd = k4[u][rs] * jnp.exp(glast - gc4[u][rs])
            new_state[si][h] = state[h] * jnp.exp(glast) + _dot_tn(kd.astype(_BF16), vn[u][hh].astype(_BF16))
            oh = o4[u][rs]
            oh = oh * lax.rsqrt(jnp.mean(oh * oh, axis=-1, keepdims=True) + RMS_EPS) * norm_g
            outs[si][h] = oh * jax.nn.silu(z[:, h * dk:(h + 1) * dk])
    return outs, new_state


def _gdn_gates(ba, alog, dtb):
    beta = jax.nn.sigmoid(ba[:, :LANES])
    g = -jnp.exp(alog) * jax.nn.softplus(ba[:, LANES:] + dtb)
    return beta, g


def _gdn_conv(xp_ref, s, x, cw_ref, *, pad, seg):
    xp_ref[s, pad:pad + seg, :] = x

    def block(col):
        cs = slice(col, col + GDN_DK)
        acc = xp_ref[s, pl.ds(pad - 3, seg), cs] * cw_ref[0:1, cs]
        for j in range(1, CONV_W):
            acc = acc + xp_ref[s, pl.ds(pad - (CONV_W - 1 - j), seg), cs] * cw_ref[j:j + 1, cs]
        return jax.nn.silu(acc)

    return block


def _gdn_prompt_body(qkv_ref, z_ref, ba_ref, cw_ref, alog_ref, dtb_ref, ng_ref,
                     o_ref, sout_ref, cout_ref, xp_ref, s_ref, *, nseg, seg, pad):
    @pl.when(pl.program_id(0) == 0)
    def _():
        xp_ref[...] = jnp.zeros(xp_ref.shape, _F32)
        s_ref[...] = jnp.zeros(s_ref.shape, _F32)

    def per_group(i, carry):
        bs = [i * GDN_SEQS_PER_STEP + k for k in range(GDN_SEQS_PER_STEP)]
        r0 = [pl.multiple_of(b * seg, seg) for b in bs]
        seqs = []
        for b, r in zip(bs, r0):
            qkv = _gdn_conv(xp_ref, b, qkv_ref[pl.ds(r, seg), :], cw_ref, pad=pad, seg=seg)
            beta, g = _gdn_gates(ba_ref[pl.ds(r, seg), :], alog_ref[...], dtb_ref[...])
            seqs.append((qkv, z_ref[pl.ds(r, seg), :], beta, g, [s_ref[b, h] for h in range(GDN_HEADS)]))
        outs, new_state = _gdn_chunks(seqs, ng_ref[...], c=seg, hg=256 // seg)
        for k, (b, r) in enumerate(zip(bs, r0)):
            tail = xp_ref[b, pl.ds(seg, pad), :]
            xp_ref[b, 0:pad, :] = tail
            cout_ref[b] = tail
            for h in range(GDN_HEADS):
                s_ref[b, h] = new_state[k][h]
                o_ref[pl.ds(r, seg), h * GDN_DV:(h + 1) * GDN_DV] = outs[k][h].astype(o_ref.dtype)
        return carry

    lax.fori_loop(0, nseg // GDN_SEQS_PER_STEP, per_group, 0)

    @pl.when(pl.program_id(0) == pl.num_programs(0) - 1)
    def _():
        sout_ref[...] = s_ref[...]


def _gdn_prompt(proj, ba, conv_w, alog, dtb, norm_g, *, n_tiles, nseg, seg):
    rows = nseg * seg
    cd = conv_w.shape[1]
    vw = GDN_HEADS * GDN_DV
    pad = 8
    const2 = lambda i: (0, 0)
    return pl.pallas_call(
        functools.partial(_gdn_prompt_body, nseg=nseg, seg=seg, pad=pad),
        grid=(n_tiles,),
        in_specs=[pl.BlockSpec((rows, cd), lambda i: (i, 0)),
                  pl.BlockSpec((rows, vw), lambda i: (i, cd // vw)),
                  pl.BlockSpec((rows, 2 * LANES), lambda i: (i, 0)),
                  pl.BlockSpec((CONV_W, cd), const2),
                  pl.BlockSpec((1, LANES), const2),
                  pl.BlockSpec((1, LANES), const2),
                  pl.BlockSpec((1, GDN_DV), const2)],
        out_specs=[pl.BlockSpec((rows, vw), lambda i: (i, 0)),
                   pl.BlockSpec((nseg, GDN_HEADS, GDN_DK, GDN_DV), lambda i: (0, 0, 0, 0)),
                   pl.BlockSpec((nseg, pad, cd), lambda i: (0, 0, 0))],
        out_shape=[jax.ShapeDtypeStruct((n_tiles * rows, vw), _BF16),
                   jax.ShapeDtypeStruct((nseg, GDN_HEADS, GDN_DK, GDN_DV), _F32),
                   jax.ShapeDtypeStruct((nseg, pad, cd), _F32)],
        scratch_shapes=[pltpu.VMEM((nseg, pad + seg, cd), _F32),
                        pltpu.VMEM((nseg, GDN_HEADS, GDN_DK, GDN_DV), _F32)],
        compiler_params=_cparams(1),
        name="gdn_prompt",
    )(proj, proj, ba, conv_w, alog, dtb, norm_g)


def _gdn_sample_body(qkv_ref, z_ref, ba_ref, s0_ref, c0_ref, cw_ref, alog_ref, dtb_ref, ng_ref,
                     o_ref, sout_ref, cout_ref, xp_ref, *, nb, seg, steps, pad):
    def per_group(i, carry):
        bs = [i * GDN_SEQS_PER_STEP + k for k in range(GDN_SEQS_PER_STEP)]
        seqs = []
        for k, b in enumerate(bs):
            xp_ref[k, 0:pad, :] = c0_ref[b]
            qkv = _gdn_conv(xp_ref, k, qkv_ref[b], cw_ref, pad=pad, seg=seg)
            cout_ref[b] = xp_ref[k, pl.ds(steps, pad), :]
            beta, g = _gdn_gates(ba_ref[b], alog_ref[...], dtb_ref[...])
            live = lax.broadcasted_iota(jnp.int32, beta.shape, 0) < steps
            seqs.append((qkv, z_ref[b], jnp.where(live, beta, 0.0), jnp.where(live, g, 0.0),
                         [s0_ref[b, h] for h in range(GDN_HEADS)]))
        outs, new_state = _gdn_chunks(seqs, ng_ref[...], c=seg, hg=GDN_HEADS)
        for k, b in enumerate(bs):
            for h in range(GDN_HEADS):
                sout_ref[b, h] = new_state[k][h]
                o_ref[b, :, h * GDN_DV:(h + 1) * GDN_DV] = outs[k][h]
        return carry

    lax.fori_loop(0, nb // GDN_SEQS_PER_STEP, per_group, 0)


def _gdn_sample(proj, ba, s0, c0, conv_w, alog, dtb, norm_g, *, steps):
    nbatch, seg, _ = proj.shape
    cd = conv_w.shape[1]
    vw = GDN_HEADS * GDN_DV
    pad = c0.shape[1]
    nb = 8
    const2 = lambda i: (0, 0)
    return pl.pallas_call(
        functools.partial(_gdn_sample_body, nb=nb, seg=seg, steps=steps, pad=pad),
        grid=(nbatch // nb,),
        in_specs=[pl.BlockSpec((nb, seg, cd), lambda i: (i, 0, 0)),
                  pl.BlockSpec((nb, seg, vw), lambda i: (i, 0, cd // vw)),
                  pl.BlockSpec((nb, seg, 2 * LANES), lambda i: (i, 0, 0)),
                  pl.BlockSpec((nb, GDN_HEADS, GDN_DK, GDN_DV), lambda i: (i, 0, 0, 0)),
                  pl.BlockSpec((nb, pad, cd), lambda i: (i, 0, 0)),
                  pl.BlockSpec((CONV_W, cd), const2),
                  pl.BlockSpec((1, LANES), const2),
                  pl.BlockSpec((1, LANES), const2),
                  pl.BlockSpec((1, GDN_DV), const2)],
        out_specs=[pl.BlockSpec((nb, seg, vw), lambda i: (i, 0, 0)),
                   pl.BlockSpec((nb, GDN_HEADS, GDN_DK, GDN_DV), lambda i: (i, 0, 0, 0)),
                   pl.BlockSpec((nb, pad, cd), lambda i: (i, 0, 0))],
        out_shape=[jax.ShapeDtypeStruct((nbatch, seg, vw), _F32),
                   jax.ShapeDtypeStruct(s0.shape, _F32),
                   jax.ShapeDtypeStruct((nbatch, pad, cd), _F32)],
        scratch_shapes=[pltpu.VMEM((GDN_SEQS_PER_STEP, pad + seg, cd), _F32)],
        compiler_params=_cparams(1),
        name="gdn_sample",
    )(proj, proj, ba, s0, c0, conv_w, alog, dtb, norm_g)


def _softmax_rows(s):
    e = jnp.exp(s - jnp.max(s, axis=-1, keepdims=True))
    return e / jnp.sum(e, axis=-1, keepdims=True)


def _attn_prompt_body(q_ref, k_ref, v_ref, o_ref, *, nseg, seg, ntile):
    hd = q_ref.shape[1] // XA_HEADS
    scale = hd ** -0.5
    for b in range(nseg):
        runs = [slice((t * nseg + b) * seg, (t * nseg + b + 1) * seg) for t in range(ntile)]
        heads = [slice(h * hd, (h + 1) * hd) for h in range(XA_HEADS)]
        s = [_dot_nt(jnp.concatenate([q_ref[rs, cs] for rs in runs], axis=0), k_ref[b, :, cs]) * scale
             for cs in heads]
        p = [_softmax_rows(sh).astype(_BF16) for sh in s]
        o = [_dot(ph, v_ref[b, :, cs]).astype(o_ref.dtype) for ph, cs in zip(p, heads)]
        for oh, cs in zip(o, heads):
            for t, rs in enumerate(runs):
                o_ref[rs, cs] = oh[t * seg:(t + 1) * seg]


def _attn_prompt(q, mem_k, mem_v, *, n_tiles, nseg, seg):
    ntile = math.gcd(n_tiles, 4)
    rows = nseg * seg * ntile
    d = q.shape[1]
    mem = mem_k.shape[1]
    return pl.pallas_call(
        functools.partial(_attn_prompt_body, nseg=nseg, seg=seg, ntile=ntile),
        grid=(n_tiles // ntile,),
        in_specs=[pl.BlockSpec((rows, d), lambda i: (i, 0)),
                  pl.BlockSpec((nseg, mem, d), lambda i: (0, 0, 0)),
                  pl.BlockSpec((nseg, mem, d), lambda i: (0, 0, 0))],
        out_specs=pl.BlockSpec((rows, d), lambda i: (i, 0)),
        out_shape=jax.ShapeDtypeStruct((n_tiles * nseg * seg, d), _BF16),
        compiler_params=_cparams(1),
        name="attn_prompt",
    )(q, mem_k, mem_v)


def _attn_sample_body(q_ref, k_ref, v_ref, o_ref, *, nb):
    t8 = q_ref.shape[1]
    mem, nh, hd = k_ref.shape[2:]
    scale = hd ** -0.5
    qh = lax.broadcasted_iota(jnp.int32, (nh * t8, mem * nh), 0) // t8
    kh = lax.broadcasted_iota(jnp.int32, (nh * t8, mem * nh), 1) % nh
    own = qh == kh

    def per_seq(b, carry):
        q = q_ref[b]
        q2 = jnp.concatenate([q[:, h * hd:(h + 1) * hd] for h in range(nh)], axis=0).astype(_BF16)
        k2 = k_ref[0, b].reshape(mem * nh, hd).astype(_BF16)
        v2 = v_ref[0, b].reshape(mem * nh, hd).astype(_BF16)
        s = jnp.where(own, _dot_nt(q2, k2) * scale, -jnp.inf)
        o2 = _dot(_softmax_rows(s).astype(_BF16), v2)
        for h in range(nh):
            o_ref[b, :, h * hd:(h + 1) * hd] = o2[h * t8:(h + 1) * t8]
        return carry

    lax.fori_loop(0, nb, per_seq, 0)


def _attn_sample(q, cache_k, cache_v, layer):
    nbatch, t8, d = q.shape
    mem, nh, hd = cache_k.shape[2:]
    nb = 8
    return pl.pallas_call(
        functools.partial(_attn_sample_body, nb=nb),
        grid=(nbatch // nb,),
        in_specs=[pl.BlockSpec((nb, t8, d), lambda i: (i, 0, 0)),
                  pl.BlockSpec((1, nb, mem, nh, hd), lambda i: (layer, i, 0, 0, 0)),
                  pl.BlockSpec((1, nb, mem, nh, hd), lambda i: (layer, i, 0, 0, 0))],
        out_specs=pl.BlockSpec((nb, t8, d), lambda i: (i, 0, 0)),
        out_shape=jax.ShapeDtypeStruct(q.shape, _F32),
        compiler_params=_cparams(1),
        name="attn_sample",
    )(q, cache_k, cache_v)


def kernel(x_prompt, x_sample, mem_prompt, state_lru_h, state_lru_conv, state_gdn_s, state_gdn_conv,
           cache_mem_k, cache_mem_v, ln_g, ln_b, xa_wq, xa_wk, xa_wv, xa_wo,
           lru_w_in, lru_conv_w, lru_conv_b, lru_gate_w, lru_gate_b, lru_lambda, lru_w_out,
           gdn_w_in, gdn_conv_w, gdn_a_log, gdn_dt_bias, gdn_norm_g, gdn_w_out,
           ffn_w_gu, ffn_w_down, moe_router, moe_w_gu, moe_w_down):
    depth = ln_g.shape[0]
    bp, seq, d = x_prompt.shape
    bs, ts, _ = x_sample.shape
    mem = mem_prompt.shape[1]
    alpha = (2 * depth) ** 0.25
    seg = GDN_CHUNK
    nchunk = seq // seg
    mp, ms = bp * seq, bs * ts
    assert bp * seg == ROW_TILE and ms == ROW_TILE and seq % seg == 0 and ts >= CONV_W - 1
    t8 = 8
    pad_p = 8
    pad_s = (CONV_W - 1) * bs
    r = lru_w_in.shape[2] // 2
    cd = gdn_conv_w.shape[2]
    vw = GDN_HEADS * GDN_DV
    d_ff = ffn_w_gu.shape[2] // 2
    d_fe = moe_w_gu.shape[3] // 2

    xp = x_prompt.reshape(bp, nchunk, seg, d).transpose(1, 0, 2, 3).reshape(mp, d)
    x = jnp.concatenate([xp, x_sample.reshape(ms, d)], axis=0)

    memx = mem_prompt.reshape(bp * mem, d)
    mem_k_p = _mm(memx, xa_wk, ng=depth, tn=1024, out_dtype=_F32).reshape(depth, bp, mem, d)
    mem_v_p = _mm(memx, xa_wv, ng=depth, tn=1024, out_dtype=_F32).reshape(depth, bp, mem, d)
    mem_k_b = mem_k_p.astype(_BF16)
    mem_v_b = mem_v_p.astype(_BF16)

    ln_g3 = ln_g.reshape(depth * 3, 1, d)
    moe_w_gu3 = moe_w_gu.reshape((-1,) + moe_w_gu.shape[2:]).astype(_BF16)
    moe_w_down3 = moe_w_down.reshape((-1,) + moe_w_down.shape[2:]).astype(_BF16)
    assert mp + ms < (1 << IDX_SHIFT)
    ln_b3 = ln_b.reshape(depth * 3, 1, d)

    def to_time_major(a):
        return a.reshape(bs, ts, -1).transpose(1, 0, 2).reshape(ms, -1)

    def to_batch_major(a):
        return a.reshape(ts, bs, -1).transpose(1, 0, 2).reshape(ms, -1)

    def pad_t(a):
        a = a.reshape(bs, ts, -1)
        return jnp.pad(a, ((0, 0), (0, t8 - ts), (0, 0)))

    lru_h_p, lru_c_p, lru_h_s, lru_c_s = [], [], [], []
    gdn_s_p, gdn_c_p, gdn_s_s, gdn_c_s = [], [], [], []

    for i in range(depth):
        j = i // 2
        if i % 2 == 0:
            gu = _mm(x, lru_w_in, g0=j, tn=1024, out_dtype=_F32)[0]
            lam = lru_lambda[j].reshape(1, r)
            cb = lru_conv_b[j].reshape(1, r)
            gb = lru_gate_b[j].reshape(LRU_BLOCKS, 1, -1)
            y, h_p, c_p = _lru_seq(gu, jnp.zeros((bp, r), _F32), jnp.zeros((bp, pad_p, r), _F32),
                                   lru_conv_w[j], cb, lru_gate_w[j], gb, lam,
                                   n_tiles=nchunk, nseg=bp, seg=seg, stride=1, pad=pad_p)
            gu_s = to_time_major(gu[mp:])
            c0_s = state_lru_conv[j].transpose(1, 0, 2).reshape(1, pad_s, r)
            y_s, h_s, c_s = _lru_seq(gu_s, state_lru_h[j], c0_s, lru_conv_w[j], cb, lru_gate_w[j], gb, lam,
                                     n_tiles=1, nseg=1, seg=ms, stride=bs, pad=pad_s)
            y_s = to_batch_major(y_s)
            lru_h_p.append(h_p)
            lru_c_p.append(c_p[:, pad_p - (CONV_W - 1):, :])
            lru_h_s.append(h_s)
            lru_c_s.append(c_s.reshape(CONV_W - 1, bs, r).transpose(1, 0, 2))
            w_out, wi = lru_w_out, j
        else:
            proj = _mm(x, gdn_w_in, g0=j, n=cd + vw, tn=1024, out_dtype=_F32)[0]
            w_tail = gdn_w_in[j, :, cd + vw:]
            w_ba = jnp.zeros((1, d, 2 * LANES), _F32)
            w_ba = w_ba.at[0, :, :GDN_HEADS].set(w_tail[:, :GDN_HEADS])
            w_ba = w_ba.at[0, :, LANES:LANES + GDN_HEADS].set(w_tail[:, GDN_HEADS:])
            ba = _mm(x, w_ba, tn=2 * LANES, out_dtype=_F32)[0]
            alog = jnp.zeros((1, LANES), _F32).at[0, :GDN_HEADS].set(gdn_a_log[j])
            dtb = jnp.zeros((1, LANES), _F32).at[0, :GDN_HEADS].set(gdn_dt_bias[j])
            ng = gdn_norm_g[j].reshape(1, GDN_DV)
            y, s_p, c_p = _gdn_prompt(proj, ba, gdn_conv_w[j], alog, dtb, ng, n_tiles=nchunk, nseg=bp, seg=seg)
            c0_s = jnp.pad(state_gdn_conv[j], ((0, 0), (pad_p - (CONV_W - 1), 0), (0, 0)))
            y_s, s_s, c_s = _gdn_sample(pad_t(proj[mp:]), pad_t(ba[mp:]), state_gdn_s[j], c0_s,
                                        gdn_conv_w[j], alog, dtb, ng, steps=ts)
            y_s = y_s[:, :ts, :].reshape(ms, vw)
            gdn_s_p.append(s_p)
            gdn_c_p.append(c_p[:, pad_p - (CONV_W - 1):, :])
            gdn_s_s.append(s_s)
            gdn_c_s.append(c_s[:, pad_p - (CONV_W - 1):, :])
            w_out, wi = gdn_w_out, j
        x = _mm_ln(y, y_s, w_out, wi, x, ln_g3, ln_b3, 3 * i, alpha=alpha)

        q = _mm(x, xa_wq, g0=i, tn=1024, out_dtype=_BF16)[0]
        o = _attn_prompt(q, mem_k_b[i], mem_v_b[i], n_tiles=nchunk, nseg=bp, seg=seg)
        o_s = _attn_sample(pad_t(q[mp:].astype(_F32)), cache_mem_k, cache_mem_v, i)
        x = _mm_ln(o, o_s[:, :ts, :].reshape(ms, d), xa_wo, i, x, ln_g3, ln_b3, 3 * i + 1, alpha=alpha)

        if i % 2 == 0:
            hdn = _gate_up(x, ffn_w_gu, g0=j, ng=1, f=d_ff, tf=512)[0]
            x = _mm_ln(hdn, None, ffn_w_down, j, x, ln_g3, ln_b3, 3 * i + 2, alpha=alpha)
        else:
            router_pad = jnp.zeros((d, LANES), _F32).at[:, :N_EXPERTS].set(moe_router[j])
            plan = _route_plan(_router(x, router_pad), ROW_TILE)
            y2 = _moe_routed(x, plan, moe_w_gu3, moe_w_down3, j * N_EXPERTS, tm=ROW_TILE)
            x = _combine_ln(y2, x, ln_g3, ln_b3, 3 * i + 2, alpha=alpha)

    y_prompt = x[:mp].reshape(nchunk, bp, seg, d).transpose(1, 0, 2, 3).reshape(bp, seq, d)
    y_sample = x[mp:].reshape(bs, ts, d)
    hd = d // XA_HEADS
    return (y_prompt, y_sample,
            jnp.stack(lru_h_p), jnp.stack(lru_c_p), jnp.stack(gdn_s_p), jnp.stack(gdn_c_p),
            mem_k_p.reshape(depth, bp, mem, XA_HEADS, hd), mem_v_p.reshape(depth, bp, mem, XA_HEADS, hd),
            jnp.stack(lru_h_s), jnp.stack(lru_c_s), jnp.stack(gdn_s_s), jnp.stack(gdn_c_s))
```

```python
import functools
import math

import jax
import jax.numpy as jnp
from jax import lax
from jax.experimental import pallas as pl
from jax.experimental.pallas import tpu as pltpu

_F32 = jnp.float32
_BF16 = jnp.bfloat16
_HI = lax.Precision.HIGHEST

CONV_W = 4
LRU_BLOCKS = 4
LRU_C = 8.0
GDN_HEADS = 8
GDN_DK = 128
GDN_DV = 128
GDN_CHUNK = 64
GDN_SEQS_PER_STEP = 2
XA_HEADS = 4
N_EXPERTS = 8
IDX_SHIFT = 15
LN_EPS = 1e-5
RMS_EPS = 1e-6
LANES = 128
SUBLANES = 8
ROW_TILE = 512
VMEM_LIMIT_MB = 56


def _cparams(ndim, vmem_mb=VMEM_LIMIT_MB):
    return pltpu.CompilerParams(dimension_semantics=("arbitrary",) * ndim,
                                vmem_limit_bytes=vmem_mb * 1024 * 1024)


def _dot(a, b, precision=None):
    return jnp.dot(a, b, preferred_element_type=_F32, precision=precision)


def _bdot(a, b):
    return _dot(a.astype(_BF16), b.astype(_BF16))


def _dot_nt(a, b):
    return lax.dot_general(a, b, (((1,), (1,)), ((), ())), preferred_element_type=_F32)


def _dot_tn(a, b):
    return lax.dot_general(a, b, (((0,), (0,)), ((), ())), preferred_element_type=_F32)


def _layer_norm(v, g, b):
    mu = jnp.mean(v, axis=-1, keepdims=True)
    d = v - mu
    var = jnp.mean(d * d, axis=-1, keepdims=True)
    return d * lax.rsqrt(var + LN_EPS) * g + b


def _mm_body(x_ref, w_ref, o_ref, wb_ref):
    @pl.when(pl.program_id(2) == 0)
    def _():
        wb_ref[...] = w_ref[0].astype(_BF16)

    o_ref[0] = _dot(x_ref[...].astype(_BF16), wb_ref[...]).astype(o_ref.dtype)


def _mm(x, w, *, g0=0, ng=1, n0=0, n=None, tn, out_dtype):
    m, k = x.shape
    n = w.shape[2] if n is None else n
    tm = min(ROW_TILE, m)
    nb0 = n0 // tn
    return pl.pallas_call(
        _mm_body,
        grid=(ng, n // tn, m // tm),
        in_specs=[pl.BlockSpec((tm, k), lambda g, j, i: (i, 0)),
                  pl.BlockSpec((1, k, tn), lambda g, j, i: (g + g0, 0, j + nb0))],
        out_specs=pl.BlockSpec((1, tm, tn), lambda g, j, i: (g, i, j)),
        out_shape=jax.ShapeDtypeStruct((ng, m, n), out_dtype),
        scratch_shapes=[pltpu.VMEM((k, tn), _BF16)],
        compiler_params=_cparams(3),
        name="mm",
    )(x, w)


def _mm_ln_body(y_ref, y2_ref, w_ref, x_ref, g_ref, b_ref, o_ref, wb_ref, *, alpha, n1):
    i = pl.program_id(0)

    @pl.when(i == 0)
    def _():
        wb_ref[...] = w_ref[0].astype(_BF16)

    def emit(src_ref):
        mix = _dot(src_ref[...].astype(_BF16), wb_ref[...])
        o_ref[...] = _layer_norm(alpha * x_ref[...] + mix, g_ref[0], b_ref[0])

    if n1 is None:
        emit(y_ref)
    else:
        pl.when(i < n1)(lambda: emit(y_ref))
        pl.when(i >= n1)(lambda: emit(y2_ref))


def _mm_ln(y, y2, w, gi, x, ln_g, ln_b, li, *, alpha):
    m, d = x.shape
    k = y.shape[1]
    tm = min(ROW_TILE, m)
    n1 = y.shape[0] // tm
    if y2 is None:
        y2 = y
    assert n1 * tm == y.shape[0] and (n1 == m // tm or n1 * tm + y2.shape[0] == m)
    return pl.pallas_call(
        functools.partial(_mm_ln_body, alpha=alpha, n1=None if n1 == m // tm else n1),
        grid=(m // tm,),
        in_specs=[pl.BlockSpec((tm, k), lambda i: (jnp.minimum(i, n1 - 1), 0)),
                  pl.BlockSpec((tm, k), lambda i: (jnp.maximum(i - n1, 0), 0)),
                  pl.BlockSpec((1, k, d), lambda i: (gi, 0, 0), pipeline_mode=pl.Buffered(1)),
                  pl.BlockSpec((tm, d), lambda i: (i, 0)),
                  pl.BlockSpec((1, 1, d), lambda i: (li, 0, 0)),
                  pl.BlockSpec((1, 1, d), lambda i: (li, 0, 0))],
        out_specs=pl.BlockSpec((tm, d), lambda i: (i, 0)),
        out_shape=jax.ShapeDtypeStruct((m, d), _F32),
        scratch_shapes=[pltpu.VMEM((k, d), _BF16)],
        compiler_params=_cparams(1),
        name="mm_ln",
    )(y, y2, w, x, ln_g, ln_b)


def _gu_body(x_ref, wg_ref, wu_ref, o_ref, wgb_ref, wub_ref):
    @pl.when(pl.program_id(2) == 0)
    def _():
        wgb_ref[...] = wg_ref[0].astype(_BF16)
        wub_ref[...] = wu_ref[0].astype(_BF16)

    xb = x_ref[...].astype(_BF16)
    g = _dot(xb, wgb_ref[...])
    u = _dot(xb, wub_ref[...])
    o_ref[0] = (jax.nn.silu(g) * u).astype(o_ref.dtype)


def _gate_up(x, w_gu, *, g0, ng, f, tf):
    m, k = x.shape
    tm = min(ROW_TILE, m)
    nf = f // tf
    return pl.pallas_call(
        _gu_body,
        grid=(ng, nf, m // tm),
        in_specs=[pl.BlockSpec((tm, k), lambda g, j, i: (i, 0)),
                  pl.BlockSpec((1, k, tf), lambda g, j, i: (g + g0, 0, j)),
                  pl.BlockSpec((1, k, tf), lambda g, j, i: (g + g0, 0, j + nf))],
        out_specs=pl.BlockSpec((1, tm, tf), lambda g, j, i: (g, i, j)),
        out_shape=jax.ShapeDtypeStruct((ng, m, f), _BF16),
        scratch_shapes=[pltpu.VMEM((k, tf), _BF16), pltpu.VMEM((k, tf), _BF16)],
        compiler_params=_cparams(3),
        name="gate_up",
    )(x, w_gu, w_gu)


def _router_body(x_ref, r_ref, gate_ref):
    logits = _dot(x_ref[...], r_ref[...], precision=_HI)
    lane = lax.broadcasted_iota(jnp.int32, logits.shape, 1).astype(_F32)
    l1 = jnp.where(lane < N_EXPERTS, logits, -jnp.inf)
    m1 = jnp.max(l1, axis=-1, keepdims=True)
    i1 = jnp.min(jnp.where(l1 == m1, lane, float(LANES)), axis=-1, keepdims=True)
    l2 = jnp.where(lane == i1, -jnp.inf, l1)
    m2 = jnp.max(l2, axis=-1, keepdims=True)
    i2 = jnp.min(jnp.where(l2 == m2, lane, float(LANES)), axis=-1, keepdims=True)
    e2 = jnp.exp(m2 - m1)
    den = 1.0 + e2
    gate_ref[...] = (jnp.where(lane == 0.0, i1, 0.0) + jnp.where(lane == 1.0, i2, 0.0)
                     + jnp.where(lane == 2.0, 1.0 / den, 0.0) + jnp.where(lane == 3.0, e2 / den, 0.0))


def _router(x, router_pad):
    m, k = x.shape
    tm = min(ROW_TILE, m)
    return pl.pallas_call(
        _router_body,
        grid=(m // tm,),
        in_specs=[pl.BlockSpec((tm, k), lambda i: (i, 0)),
                  pl.BlockSpec((k, LANES), lambda i: (0, 0))],
        out_specs=pl.BlockSpec((tm, LANES), lambda i: (i, 0)),
        out_shape=jax.ShapeDtypeStruct((m, LANES), _F32),
        compiler_params=_cparams(1),
        name="router",
    )(x, router_pad)


def _route_plan(rt, tm):
    m = rt.shape[0]
    ef = rt[:, :2].astype(jnp.int32).reshape(-1)
    wf = rt[:, 2:4].reshape(-1)
    onehot = (ef[:, None] == jnp.arange(N_EXPERTS, dtype=jnp.int32)[None, :]).astype(jnp.int32)
    csum = jnp.cumsum(onehot, axis=0)
    cnt = csum[-1]
    rank = jnp.sum((csum - onehot) * onehot, axis=1)
    cnt_p = ((cnt + tm - 1) // tm) * tm
    end = jnp.cumsum(cnt_p)
    start = end - cnt_p
    dest = start[ef] + rank
    n_tiles = (2 * m) // tm + N_EXPERTS
    rows = n_tiles * tm
    pair = jnp.zeros((rows,), jnp.int32).at[dest].set(jnp.arange(2 * m, dtype=jnp.int32))
    tile_start = jnp.arange(n_tiles, dtype=jnp.int32) * tm
    texp = jnp.sum((tile_start[:, None] >= end[None, :]).astype(jnp.int32), axis=1)
    used = tile_start < end[-1]
    last = jnp.maximum(end[-1] // tm - 1, 0)
    texp = jnp.where(used, texp, texp[last])
    nval = jnp.where(used, jnp.clip(start[texp] + cnt[texp] - tile_start, 0, tm), 0)
    tile = jnp.repeat(jnp.arange(n_tiles, dtype=jnp.int32), tm)
    in_tile = jnp.arange(rows, dtype=jnp.int32) - tile * tm
    live = in_tile < jnp.repeat(nval, tm)
    tok, slot = pair // 2, pair % 2
    spare = 2 * m + (tile % 2) * tm + in_tile
    idx = jnp.where(live, tok, 0) | (jnp.where(live, slot * m + tok, spare) << IDX_SHIFT)
    rw = jnp.where(live, wf[pair], 0.0).reshape(rows, 1)
    n_used = (end[-1] // tm).reshape(1)
    return texp, n_used, idx, rw


def _moe_routed_body(texp_ref, nused_ref, idx_ref, x_hbm, rw_ref, wg_ref, wu_ref, wd_ref, out_hbm,
                     xs_ref, ys_ref, gsem, ssem):
    i = pl.program_id(0)
    ng = xs_ref.shape[1]
    tm = ng * SUBLANES
    d = xs_ref.shape[3]
    n_used = nused_ref[0]
    slot = i % 2

    def gather(tile, s):
        def body(g, c):
            for u in range(SUBLANES):
                tok = idx_ref[tile * tm + g * SUBLANES + u] & ((1 << IDX_SHIFT) - 1)
                pltpu.make_async_copy(x_hbm.at[tok >> 3, tok & 7], xs_ref.at[s, g, u], gsem.at[s]).start()
            return c
        lax.fori_loop(0, ng, body, 0)

    def scatter(tile, s):
        def body(g, c):
            for u in range(SUBLANES):
                dst = idx_ref[tile * tm + g * SUBLANES + u] >> IDX_SHIFT
                pltpu.make_async_copy(ys_ref.at[s, g, u], out_hbm.at[dst >> 3, dst & 7], ssem.at[s]).start()
            return c
        lax.fori_loop(0, ng, body, 0)

    def wait_gather(s):
        pltpu.make_async_copy(x_hbm.at[pl.ds(0, ng)], xs_ref.at[s], gsem.at[s]).wait()

    def wait_scatter(s):
        pltpu.make_async_copy(ys_ref.at[s], out_hbm.at[pl.ds(0, ng)], ssem.at[s]).wait()

    @pl.when((i == 0) & (n_used > 0))
    def _():
        gather(0, 0)

    @pl.when(i + 1 < n_used)
    def _():
        gather(i + 1, 1 - slot)

    @pl.when(i < n_used)
    def _():
        wait_gather(slot)

        @pl.when(i >= 2)
        def _():
            wait_scatter(slot)

        xb = xs_ref[slot].reshape(tm, d).astype(_BF16)
        h = (jax.nn.silu(_dot(xb, wg_ref[0])) * _dot(xb, wu_ref[0])).astype(_BF16)
        ys_ref[slot] = (_dot(h, wd_ref[0]) * rw_ref[...]).reshape(ng, SUBLANES, d)
        scatter(i, slot)

    @pl.when(i == pl.num_programs(0) - 1)
    def _():
        @pl.when(n_used >= 1)
        def _():
            wait_scatter((n_used - 1) % 2)

        @pl.when(n_used >= 2)
        def _():
            wait_scatter(n_used % 2)

        ys_ref[0] = jnp.zeros(ys_ref.shape[1:], _F32)
        spare0 = out_hbm.shape[0] - 2 * ng
        fills = [pltpu.make_async_copy(ys_ref.at[0], out_hbm.at[pl.ds(spare0 + k * ng, ng)], ssem.at[0])
                 for k in range(2)]
        for cp in fills:
            cp.start()
        for cp in fills:
            cp.wait()


def _moe_routed(x, plan, w_gu_bf16, w_down_bf16, e0, *, tm):
    texp, n_used, idx, rw = plan
    m, d = x.shape
    f = w_down_bf16.shape[1]
    n_tiles = texp.shape[0]
    return pl.pallas_call(
        _moe_routed_body,
        grid_spec=pltpu.PrefetchScalarGridSpec(
            num_scalar_prefetch=3,
            grid=(n_tiles,),
            in_specs=[pl.BlockSpec(memory_space=pl.ANY),
                      pl.BlockSpec((tm, 1), lambda i, te, nu, ix: (i, 0)),
                      pl.BlockSpec((1, d, f), lambda i, te, nu, ix: (te[i] + e0, 0, 0)),
                      pl.BlockSpec((1, d, f), lambda i, te, nu, ix: (te[i] + e0, 0, 1)),
                      pl.BlockSpec((1, f, d), lambda i, te, nu, ix: (te[i] + e0, 0, 0))],
            out_specs=pl.BlockSpec(memory_space=pl.ANY),
            scratch_shapes=[pltpu.VMEM((2, tm // SUBLANES, SUBLANES, d), _F32),
                            pltpu.VMEM((2, tm // SUBLANES, SUBLANES, d), _F32),
                            pltpu.SemaphoreType.DMA((2,)), pltpu.SemaphoreType.DMA((2,))]),
        out_shape=jax.ShapeDtypeStruct(((2 * m + 2 * tm) // SUBLANES, SUBLANES, d), _F32),
        compiler_params=_cparams(1),
        name="moe_routed",
    )(texp, n_used, idx, x.reshape(m // SUBLANES, SUBLANES, d), rw, w_gu_bf16, w_gu_bf16,
      w_down_bf16).reshape(2 * m + 2 * tm, d)


def _combine_ln_body(a_ref, b_ref, x_ref, g_ref, bb_ref, *o_refs, alpha, n_prompt):
    res = _layer_norm(alpha * x_ref[...] + (a_ref[...] + b_ref[...]), g_ref[0], bb_ref[0])
    if n_prompt is None:
        o_refs[0][...] = res
    else:
        i = pl.program_id(0)
        op_ref, os_ref = o_refs

        @pl.when(i < n_prompt)
        def _():
            op_ref[...] = res.reshape(op_ref.shape)

        @pl.when(i >= n_prompt)
        def _():
            os_ref[...] = res


def _combine_ln(y2, x, ln_g, ln_b, li, *, alpha, prompt_shape=None):
    m, d = x.shape
    tm = min(ROW_TILE, m)
    nt = m // tm
    if prompt_shape is None:
        n_prompt = None
        out_specs = pl.BlockSpec((tm, d), lambda i: (i, 0))
        out_shape = jax.ShapeDtypeStruct((m, d), _F32)
    else:
        bp, n_prompt, seg = prompt_shape
        assert bp * seg == tm and n_prompt == nt - 1
        out_specs = [pl.BlockSpec((bp, 1, seg, d), lambda i: (0, jnp.minimum(i, n_prompt - 1), 0, 0)),
                     pl.BlockSpec((tm, d), lambda i: (0, 0))]
        out_shape = [jax.ShapeDtypeStruct((bp, n_prompt, seg, d), _F32), jax.ShapeDtypeStruct((tm, d), _F32)]
    return pl.pallas_call(
        functools.partial(_combine_ln_body, alpha=alpha, n_prompt=n_prompt),
        grid=(nt,),
        in_specs=[pl.BlockSpec((tm, d), lambda i: (i, 0)),
                  pl.BlockSpec((tm, d), lambda i: (i + nt, 0)),
                  pl.BlockSpec((tm, d), lambda i: (i, 0)),
                  pl.BlockSpec((1, 1, d), lambda i: (li, 0, 0)),
                  pl.BlockSpec((1, 1, d), lambda i: (li, 0, 0))],
        out_specs=out_specs,
        out_shape=out_shape,
        compiler_params=_cparams(1),
        name="combine_ln",
    )(y2, y2, x, ln_g, ln_b)


def _pack_body(xp_ref, xs_ref, o_ref, *, n_prompt):
    i = pl.program_id(0)

    @pl.when(i < n_prompt)
    def _():
        o_ref[...] = xp_ref[...].reshape(o_ref.shape)

    @pl.when(i >= n_prompt)
    def _():
        o_ref[...] = xs_ref[...]


def _pack_tokens(x_prompt4, x_sample2):
    bp, n_prompt, seg, d = x_prompt4.shape
    tm = bp * seg
    assert x_sample2.shape == (tm, d)
    return pl.pallas_call(
        functools.partial(_pack_body, n_prompt=n_prompt),
        grid=(n_prompt + 1,),
        in_specs=[pl.BlockSpec((bp, 1, seg, d), lambda i: (0, jnp.minimum(i, n_prompt - 1), 0, 0)),
                  pl.BlockSpec((tm, d), lambda i: (0, 0))],
        out_specs=pl.BlockSpec((tm, d), lambda i: (i, 0)),
        out_shape=jax.ShapeDtypeStruct(((n_prompt + 1) * tm, d), _F32),
        compiler_params=_cparams(1),
        name="pack_tokens",
    )(x_prompt4, x_sample2)


def _seg_scan(a, b, tstep, n_steps, stride):
    sh = stride
    for s in range(n_steps):
        keep = tstep >= (1 << s)
        a_sh = pltpu.roll(a, sh, 0)
        b_sh = pltpu.roll(b, sh, 0)
        b = jnp.where(keep, a * b_sh + b, b)
        a = jnp.where(keep, a * a_sh, a)
        sh *= 2
    return a, b


def _lru_body(gate_ref, rec_ref, h0_ref, c0_ref, cw_ref, cb_ref, gw_ref, gb_ref, lam_ref,
              y_ref, hout_ref, cout_ref, xp_ref, hc_ref, cv_ref, gwb_ref, *, nseg, seg, stride, pad):
    rows = nseg * seg
    bw = gw_ref.shape[1]

    @pl.when(pl.program_id(0) == 0)
    def _():
        xp_ref[:, 0:pad, :] = c0_ref[...]
        hc_ref[...] = h0_ref[...]
        gwb_ref[...] = gw_ref[...].astype(_BF16)

    for s in range(nseg):
        xp_ref[s, pad:pad + seg, :] = rec_ref[s * seg:(s + 1) * seg, :]
        acc = xp_ref[s, pad - 3 * stride:pad - 3 * stride + seg, :] * cw_ref[0:1, :]
        for j in range(1, CONV_W):
            off = pad - (CONV_W - 1 - j) * stride
            acc = acc + xp_ref[s, off:off + seg, :] * cw_ref[j:j + 1, :]
        cv_ref[s * seg:(s + 1) * seg, :] = acc + cb_ref[...]
        tail = xp_ref[s, seg:seg + pad, :]
        xp_ref[s, 0:pad, :] = tail
        cout_ref[s] = tail

    row = lax.broadcasted_iota(jnp.int32, (rows, bw), 0)
    if stride == 1:
        tstep = row & (seg - 1)
    else:
        tstep = row >> int(math.log2(stride))
    n_steps = int(math.log2(seg // stride))

    for n in range(LRU_BLOCKS):
        cs = slice(n * bw, (n + 1) * bw)
        cv = cv_ref[:, cs]
        gates = _dot(cv.astype(_BF16), gwb_ref[n]) + gb_ref[n]
        r = jax.nn.sigmoid(gates[:, :bw])
        ig = jax.nn.sigmoid(gates[:, bw:])
        log_a = -LRU_C * r * jax.nn.softplus(-lam_ref[:, cs])
        a = jnp.exp(log_a)
        mult = jnp.sqrt(-jnp.tanh(log_a) * (a * a + 1.0))
        a_cum, b_cum = _seg_scan(a, mult * ig * cv, tstep, n_steps, stride)
        hc = hc_ref[:, cs]
        if stride == 1:
            hprev = jnp.concatenate([jnp.broadcast_to(hc[s:s + 1, :], (seg, bw)) for s in range(nseg)], axis=0)
        else:
            hprev = jnp.concatenate([hc] * (seg // stride), axis=0)
        h = a_cum * hprev + b_cum
        y_ref[:, cs] = (h * jax.nn.gelu(gate_ref[:, cs])).astype(y_ref.dtype)
        if stride == 1:
            hc_ref[:, cs] = jnp.concatenate([h[(s + 1) * seg - 1:(s + 1) * seg, :] for s in range(nseg)], axis=0)
        else:
            hc_ref[:, cs] = h[rows - stride:, :]

    hout_ref[...] = hc_ref[...]


def _lru_seq(gu, h0, c0, conv_w, conv_b, gate_w, gate_b, lam, *, n_tiles, nseg, seg, stride, pad):
    rows = nseg * seg
    r = gu.shape[1] // 2
    bw = r // LRU_BLOCKS
    const2 = lambda i: (0, 0)
    const3 = lambda i: (0, 0, 0)
    return pl.pallas_call(
        functools.partial(_lru_body, nseg=nseg, seg=seg, stride=stride, pad=pad),
        grid=(n_tiles,),
        in_specs=[pl.BlockSpec((rows, r), lambda i: (i, 0)),
                  pl.BlockSpec((rows, r), lambda i: (i, 1)),
                  pl.BlockSpec((nseg * stride, r), const2),
                  pl.BlockSpec((nseg, pad, r), const3),
                  pl.BlockSpec((CONV_W, r), const2),
                  pl.BlockSpec((1, r), const2),
                  pl.BlockSpec((LRU_BLOCKS, bw, 2 * bw), const3),
                  pl.BlockSpec((LRU_BLOCKS, 1, 2 * bw), const3),
                  pl.BlockSpec((1, r), const2)],
        out_specs=[pl.BlockSpec((rows, r), lambda i: (i, 0)),
                   pl.BlockSpec((nseg * stride, r), const2),
                   pl.BlockSpec((nseg, pad, r), const3)],
        out_shape=[jax.ShapeDtypeStruct((n_tiles * rows, r), _BF16),
                   jax.ShapeDtypeStruct((nseg * stride, r), _F32),
                   jax.ShapeDtypeStruct((nseg, pad, r), _F32)],
        scratch_shapes=[pltpu.VMEM((nseg, pad + seg, r), _F32),
                        pltpu.VMEM((nseg * stride, r), _F32),
                        pltpu.VMEM((rows, r), _F32),
                        pltpu.VMEM((LRU_BLOCKS, bw, 2 * bw), _BF16)],
        compiler_params=_cparams(1),
        name="lru_seq",
    )(gu, gu, h0, c0, conv_w, conv_b, gate_w, gate_b, lam)


def _l2norm(t):
    return t * lax.rsqrt(jnp.sum(t * t, axis=-1, keepdims=True) + RMS_EPS)


def _gdn_chunks(seqs, norm_g, *, c, hg):
    dk = GDN_DK
    qk_w = GDN_HEADS * dk
    r4 = hg * c
    lg_c = int(math.log2(c))
    row = lax.broadcasted_iota(jnp.int32, (r4, r4), 0)
    col = lax.broadcasted_iota(jnp.int32, (r4, r4), 1)
    blk = {1 << s: (row >> s) == (col >> s) for s in range(3, lg_c + 1)}
    incl = blk[c] & (row >= col)
    strict = blk[c] & (row > col)
    units = [(si, list(range(g0, g0 + hg))) for si in range(len(seqs)) for g0 in range(0, GDN_HEADS, hg)]
    blocks = [slice(hh * c, (hh + 1) * c) for hh in range(hg)]

    def stack(fn, hs, axis=0):
        return jnp.concatenate([fn(h) for h in hs], axis=axis)

    tr = lax.broadcasted_iota(jnp.int32, (c, c), 0)
    tc = lax.broadcasted_iota(jnp.int32, (c, c), 1)
    tri = jnp.where(tr >= tc, 1.0, 0.0)
    gc = [_dot(tri, sq[3], precision=_HI) for sq in seqs]
    gct = [jnp.concatenate([x, jnp.zeros((LANES - c, LANES), _F32)], axis=0).T for x in gc]

    q4 = [stack(lambda h: _l2norm(seqs[si][0](h * dk)) * (dk ** -0.5), hs) for si, hs in units]
    k4 = [stack(lambda h: _l2norm(seqs[si][0](qk_w + h * dk)), hs) for si, hs in units]
    v4 = [stack(lambda h: seqs[si][0](2 * qk_w + h * dk), hs) for si, hs in units]
    beta4 = [stack(lambda h: jnp.broadcast_to(seqs[si][2][:, h:h + 1], (c, dk)), hs) for si, hs in units]
    gc4 = [stack(lambda h: jnp.broadcast_to(gc[si][:, h:h + 1], (c, dk)), hs) for si, hs in units]
    gcol = [stack(lambda h: jnp.broadcast_to(gc[si][:, h:h + 1], (c, r4)), hs) for si, hs in units]
    grow = [stack(lambda h: gct[si][h:h + 1, 0:c], hs, axis=1) for si, hs in units]
    decay = [jnp.exp(jnp.where(incl, gcl - grw, -jnp.inf)) for gcl, grw in zip(gcol, grow)]
    kb4 = [k * bt for k, bt in zip(k4, beta4)]
    aq = [_dot_nt(jnp.concatenate([kb, q], axis=0).astype(_BF16), k.astype(_BF16)) for kb, q, k in zip(kb4, q4, k4)]
    a4 = [jnp.where(strict, x[:r4] * dc, 0.0) for x, dc in zip(aq, decay)]
    attn4 = [x[r4:] * dc for x, dc in zip(aq, decay)]
    c0 = min(c, 16)
    np_ = [-jnp.where(blk[c0], x, 0.0) for x in a4]
    e = list(np_)
    for _ in range(int(math.log2(c0)) - 1):
        np_ = [_bdot(x, x) for x in np_]
        e = [ei + ni + _bdot(ni, ei) for ei, ni in zip(e, np_)]
    size = c0
    while size < c:
        lo = [jnp.where(blk[2 * size] & ~blk[size], x, 0.0) for x in a4]
        m = [li + _bdot(li, ei) for li, ei in zip(lo, e)]
        e = [ei - (mi + _bdot(ei, mi)) for ei, mi in zip(e, m)]
        size *= 2
    egc4 = [jnp.exp(x) for x in gc4]
    rhs = [jnp.concatenate([v * bt, kb * eg], axis=1) for v, bt, kb, eg in zip(v4, beta4, kb4, egc4)]
    uw = [r + _bdot(ei, r) for r, ei in zip(rhs, e)]
    qg4 = [q * eg for q, eg in zip(q4, egc4)]
    ws_qs = [[_dot(jnp.concatenate([uw[u][rs, dk:], qg4[u][rs]], axis=0).astype(_BF16),
                   seqs[si][4][h].astype(_BF16)) for rs, h in zip(blocks, hs)]
             for u, (si, hs) in enumerate(units)]
    vn = [[uw[u][rs, :dk] - ws_qs[u][hh][:c] for hh, rs in enumerate(blocks)] for u in range(len(units))]
    o4 = [jnp.concatenate([x[c:] for x in ws_qs[u]], axis=0)
          + _bdot(attn4[u], jnp.concatenate(vn[u], axis=0)) for u in range(len(units))]
    outs = [[None] * GDN_HEADS for _ in seqs]
    new_state = [[None] * GDN_HEADS for _ in seqs]
    for u, (si, hs) in enumerate(units):
        z, state = seqs[si][1], seqs[si][4]
        for hh, (rs, h) in enumerate(zip(blocks, hs)):
            glast = gc[si][c - 1:c, h:h + 1]
            kd = k4[u][rs] * jnp.exp(glast - gc4[u][rs])
            new_state[si][h] = state[h] * jnp.exp(glast) + _dot_tn(kd.astype(_BF16), vn[u][hh].astype(_BF16))
            oh = o4[u][rs]
            oh = oh * lax.rsqrt(jnp.mean(oh * oh, axis=-1, keepdims=True) + RMS_EPS) * norm_g
            outs[si][h] = oh * jax.nn.silu(z[:, h * dk:(h + 1) * dk])
    return outs, new_state


def _gdn_gates(ba, alog, dtb):
    beta = jax.nn.sigmoid(ba[:, :LANES])
    g = -jnp.exp(alog) * jax.nn.softplus(ba[:, LANES:] + dtb)
    return beta, g


def _gdn_conv(xp_ref, s, x, cw_ref, *, pad, seg):
    xp_ref[s, pad:pad + seg, :] = x

    def block(col):
        cs = slice(col, col + GDN_DK)
        acc = xp_ref[s, pl.ds(pad - 3, seg), cs] * cw_ref[0:1, cs]
        for j in range(1, CONV_W):
            acc = acc + xp_ref[s, pl.ds(pad - (CONV_W - 1 - j), seg), cs] * cw_ref[j:j + 1, cs]
        return jax.nn.silu(acc)

    return block


def _gdn_prompt_body(qkv_ref, z_ref, ba_ref, cw_ref, alog_ref, dtb_ref, ng_ref,
                     o_ref, sout_ref, cout_ref, xp_ref, s_ref, *, nseg, seg, pad):
    @pl.when(pl.program_id(0) == 0)
    def _():
        xp_ref[...] = jnp.zeros(xp_ref.shape, _F32)
        s_ref[...] = jnp.zeros(s_ref.shape, _F32)

    def per_group(i, carry):
        bs = [i * GDN_SEQS_PER_STEP + k for k in range(GDN_SEQS_PER_STEP)]
        r0 = [pl.multiple_of(b * seg, seg) for b in bs]
        seqs = []
        for b, r in zip(bs, r0):
            qkv = _gdn_conv(xp_ref, b, qkv_ref[pl.ds(r, seg), :], cw_ref, pad=pad, seg=seg)
            beta, g = _gdn_gates(ba_ref[pl.ds(r, seg), :], alog_ref[...], dtb_ref[...])
            seqs.append((qkv, z_ref[pl.ds(r, seg), :], beta, g, [s_ref[b, h] for h in range(GDN_HEADS)]))
        outs, new_state = _gdn_chunks(seqs, ng_ref[...], c=seg, hg=256 // seg)
        for k, (b, r) in enumerate(zip(bs, r0)):
            tail = xp_ref[b, pl.ds(seg, pad), :]
            xp_ref[b, 0:pad, :] = tail
            cout_ref[b] = tail
            for h in range(GDN_HEADS):
                s_ref[b, h] = new_state[k][h]
                o_ref[pl.ds(r, seg), h * GDN_DV:(h + 1) * GDN_DV] = outs[k][h].astype(o_ref.dtype)
        return carry

    lax.fori_loop(0, nseg // GDN_SEQS_PER_STEP, per_group, 0)

    @pl.when(pl.program_id(0) == pl.num_programs(0) - 1)
    def _():
        sout_ref[...] = s_ref[...]


def _gdn_prompt(proj, ba, conv_w, alog, dtb, norm_g, *, n_tiles, nseg, seg):
    rows = nseg * seg
    cd = conv_w.shape[1]
    vw = GDN_HEADS * GDN_DV
    pad = 8
    const2 = lambda i: (0, 0)
    return pl.pallas_call(
        functools.partial(_gdn_prompt_body, nseg=nseg, seg=seg, pad=pad),
        grid=(n_tiles,),
        in_specs=[pl.BlockSpec((rows, cd), lambda i: (i, 0)),
                  pl.BlockSpec((rows, vw), lambda i: (i, cd // vw)),
                  pl.BlockSpec((rows, 2 * LANES), lambda i: (i, 0)),
                  pl.BlockSpec((CONV_W, cd), const2),
                  pl.BlockSpec((1, LANES), const2),
                  pl.BlockSpec((1, LANES), const2),
                  pl.BlockSpec((1, GDN_DV), const2)],
        out_specs=[pl.BlockSpec((rows, vw), lambda i: (i, 0)),
                   pl.BlockSpec((nseg, GDN_HEADS, GDN_DK, GDN_DV), lambda i: (0, 0, 0, 0)),
                   pl.BlockSpec((nseg, pad, cd), lambda i: (0, 0, 0))],
        out_shape=[jax.ShapeDtypeStruct((n_tiles * rows, vw), _BF16),
                   jax.ShapeDtypeStruct((nseg, GDN_HEADS, GDN_DK, GDN_DV), _F32),
                   jax.ShapeDtypeStruct((nseg, pad, cd), _F32)],
        scratch_shapes=[pltpu.VMEM((nseg, pad + seg, cd), _F32),
                        pltpu.VMEM((nseg, GDN_HEADS, GDN_DK, GDN_DV), _F32)],
        compiler_params=_cparams(1),
        name="gdn_prompt",
    )(proj, proj, ba, conv_w, alog, dtb, norm_g)


def _gdn_sample_body(qkv_ref, z_ref, ba_ref, s0_ref, c0_ref, cw_ref, alog_ref, dtb_ref, ng_ref,
                     o_ref, sout_ref, cout_ref, xp_ref, *, nb, seg, steps, pad):
    def per_group(i, carry):
        bs = [i * GDN_SEQS_PER_STEP + k for k in range(GDN_SEQS_PER_STEP)]
        seqs = []
        for k, b in enumerate(bs):
            xp_ref[k, 0:pad, :] = c0_ref[b]
            qkv = _gdn_conv(xp_ref, k, qkv_ref[b], cw_ref, pad=pad, seg=seg)
            cout_ref[b] = xp_ref[k, pl.ds(steps, pad), :]
            beta, g = _gdn_gates(ba_ref[b], alog_ref[...], dtb_ref[...])
            live = lax.broadcasted_iota(jnp.int32, beta.shape, 0) < steps
            seqs.append((qkv, z_ref[b], jnp.where(live, beta, 0.0), jnp.where(live, g, 0.0),
                         [s0_ref[b, h] for h in range(GDN_HEADS)]))
        outs, new_state = _gdn_chunks(seqs, ng_ref[...], c=seg, hg=GDN_HEADS)
        for k, b in enumerate(bs):
            for h in range(GDN_HEADS):
                sout_ref[b, h] = new_state[k][h]
                o_ref[b, :, h * GDN_DV:(h + 1) * GDN_DV] = outs[k][h]
        return carry

    lax.fori_loop(0, nb // GDN_SEQS_PER_STEP, per_group, 0)


def _gdn_sample(proj, ba, s0, c0, conv_w, alog, dtb, norm_g, *, steps):
    nbatch, seg, _ = proj.shape
    cd = conv_w.shape[1]
    vw = GDN_HEADS * GDN_DV
    pad = c0.shape[1]
    nb = 8
    const2 = lambda i: (0, 0)
    return pl.pallas_call(
        functools.partial(_gdn_sample_body, nb=nb, seg=seg, steps=steps, pad=pad),
        grid=(nbatch // nb,),
        in_specs=[pl.BlockSpec((nb, seg, cd), lambda i: (i, 0, 0)),
                  pl.BlockSpec((nb, seg, vw), lambda i: (i, 0, cd // vw)),
                  pl.BlockSpec((nb, seg, 2 * LANES), lambda i: (i, 0, 0)),
                  pl.BlockSpec((nb, GDN_HEADS, GDN_DK, GDN_DV), lambda i: (i, 0, 0, 0)),
                  pl.BlockSpec((nb, pad, cd), lambda i: (i, 0, 0)),
                  pl.BlockSpec((CONV_W, cd), const2),
                  pl.BlockSpec((1, LANES), const2),
                  pl.BlockSpec((1, LANES), const2),
                  pl.BlockSpec((1, GDN_DV), const2)],
        out_specs=[pl.BlockSpec((nb, seg, vw), lambda i: (i, 0, 0)),
                   pl.BlockSpec((nb, GDN_HEADS, GDN_DK, GDN_DV), lambda i: (i, 0, 0, 0)),
                   pl.BlockSpec((nb, pad, cd), lambda i: (i, 0, 0))],
        out_shape=[jax.ShapeDtypeStruct((nbatch, seg, vw), _F32),
                   jax.ShapeDtypeStruct(s0.shape, _F32),
                   jax.ShapeDtypeStruct((nbatch, pad, cd), _F32)],
        scratch_shapes=[pltpu.VMEM((GDN_SEQS_PER_STEP, pad + seg, cd), _F32)],
        compiler_params=_cparams(1),
        name="gdn_sample",
    )(proj, proj, ba, s0, c0, conv_w, alog, dtb, norm_g)


def _softmax_rows(s):
    e = jnp.exp(s - jnp.max(s, axis=-1, keepdims=True))
    return e / jnp.sum(e, axis=-1, keepdims=True)


def _attn_prompt_body(q_ref, k_ref, v_ref, o_ref, *, nseg, seg, ntile):
    hd = q_ref.shape[1] // XA_HEADS
    scale = hd ** -0.5
    for b in range(nseg):
        runs = [slice((t * nseg + b) * seg, (t * nseg + b + 1) * seg) for t in range(ntile)]
        heads = [slice(h * hd, (h + 1) * hd) for h in range(XA_HEADS)]
        s = [_dot_nt(jnp.concatenate([q_ref[rs, cs] for rs in runs], axis=0), k_ref[b, :, cs]) * scale
             for cs in heads]
        p = [_softmax_rows(sh).astype(_BF16) for sh in s]
        o = [_dot(ph, v_ref[b, :, cs]).astype(o_ref.dtype) for ph, cs in zip(p, heads)]
        for oh, cs in zip(o, heads):
            for t, rs in enumerate(runs):
                o_ref[rs, cs] = oh[t * seg:(t + 1) * seg]


def _attn_prompt(q, mem_k, mem_v, *, n_tiles, nseg, seg):
    ntile = math.gcd(n_tiles, 4)
    rows = nseg * seg * ntile
    d = q.shape[1]
    mem = mem_k.shape[1]
    return pl.pallas_call(
        functools.partial(_attn_prompt_body, nseg=nseg, seg=seg, ntile=ntile),
        grid=(n_tiles // ntile,),
        in_specs=[pl.BlockSpec((rows, d), lambda i: (i, 0)),
                  pl.BlockSpec((nseg, mem, d), lambda i: (0, 0, 0)),
                  pl.BlockSpec((nseg, mem, d), lambda i: (0, 0, 0))],
        out_specs=pl.BlockSpec((rows, d), lambda i: (i, 0)),
        out_shape=jax.ShapeDtypeStruct((n_tiles * nseg * seg, d), _BF16),
        compiler_params=_cparams(1),
        name="attn_prompt",
    )(q, mem_k, mem_v)


def _attn_sample_body(q_ref, k_ref, v_ref, o_ref, *, nb):
    t8 = q_ref.shape[1]
    mem, nh, hd = k_ref.shape[2:]
    scale = hd ** -0.5
    qh = lax.broadcasted_iota(jnp.int32, (nh * t8, mem * nh), 0) // t8
    kh = lax.broadcasted_iota(jnp.int32, (nh * t8, mem * nh), 1) % nh
    own = qh == kh

    def per_seq(b, carry):
        q = q_ref[b]
        q2 = jnp.concatenate([q[:, h * hd:(h + 1) * hd] for h in range(nh)], axis=0).astype(_BF16)
        k2 = k_ref[0, b].reshape(mem * nh, hd).astype(_BF16)
        v2 = v_ref[0, b].reshape(mem * nh, hd).astype(_BF16)
        s = jnp.where(own, _dot_nt(q2, k2) * scale, -jnp.inf)
        o2 = _dot(_softmax_rows(s).astype(_BF16), v2)
        for h in range(nh):
            o_ref[b, :, h * hd:(h + 1) * hd] = o2[h * t8:(h + 1) * t8]
        return carry

    lax.fori_loop(0, nb, per_seq, 0)


def _attn_sample(q, cache_k, cache_v, layer):
    nbatch, t8, d = q.shape
    mem, nh, hd = cache_k.shape[2:]
    nb = 8
    return pl.pallas_call(
        functools.partial(_attn_sample_body, nb=nb),
        grid=(nbatch // nb,),
        in_specs=[pl.BlockSpec((nb, t8, d), lambda i: (i, 0, 0)),
                  pl.BlockSpec((1, nb, mem, nh, hd), lambda i: (layer, i, 0, 0, 0)),
                  pl.BlockSpec((1, nb, mem, nh, hd), lambda i: (layer, i, 0, 0, 0))],
        out_specs=pl.BlockSpec((nb, t8, d), lambda i: (i, 0, 0)),
        out_shape=jax.ShapeDtypeStruct(q.shape, _F32),
        compiler_params=_cparams(1),
        name="attn_sample",
    )(q, cache_k, cache_v)


def kernel(x_prompt, x_sample, mem_prompt, state_lru_h, state_lru_conv, state_gdn_s, state_gdn_conv,
           cache_mem_k, cache_mem_v, ln_g, ln_b, xa_wq, xa_wk, xa_wv, xa_wo,
           lru_w_in, lru_conv_w, lru_conv_b, lru_gate_w, lru_gate_b, lru_lambda, lru_w_out,
           gdn_w_in, gdn_conv_w, gdn_a_log, gdn_dt_bias, gdn_norm_g, gdn_w_out,
           ffn_w_gu, ffn_w_down, moe_router, moe_w_gu, moe_w_down):
    depth = ln_g.shape[0]
    bp, seq, d = x_prompt.shape
    bs, ts, _ = x_sample.shape
    mem = mem_prompt.shape[1]
    alpha = (2 * depth) ** 0.25
    seg = GDN_CHUNK
    nchunk = seq // seg
    mp, ms = bp * seq, bs * ts
    assert bp * seg == ROW_TILE and ms == ROW_TILE and seq % seg == 0 and ts >= CONV_W - 1
    assert depth % 2 == 0
    t8 = 8
    pad_p = 8
    pad_s = (CONV_W - 1) * bs
    r = lru_w_in.shape[2] // 2
    cd = gdn_conv_w.shape[2]
    vw = GDN_HEADS * GDN_DV
    d_ff = ffn_w_gu.shape[2] // 2
    d_fe = moe_w_gu.shape[3] // 2

    x = _pack_tokens(x_prompt.reshape(bp, nchunk, seg, d), x_sample.reshape(ms, d))

    memx = mem_prompt.reshape(bp * mem, d)
    mem_k_p = _mm(memx, xa_wk, ng=depth, tn=1024, out_dtype=_F32).reshape(depth, bp, mem, d)
    mem_v_p = _mm(memx, xa_wv, ng=depth, tn=1024, out_dtype=_F32).reshape(depth, bp, mem, d)
    mem_k_b = mem_k_p.astype(_BF16)
    mem_v_b = mem_v_p.astype(_BF16)

    ln_g3 = ln_g.reshape(depth * 3, 1, d)
    moe_w_gu3 = moe_w_gu.reshape((-1,) + moe_w_gu.shape[2:]).astype(_BF16)
    moe_w_down3 = moe_w_down.reshape((-1,) + moe_w_down.shape[2:]).astype(_BF16)
    assert mp + ms < (1 << IDX_SHIFT)
    ln_b3 = ln_b.reshape(depth * 3, 1, d)

    def to_time_major(a):
        return a.reshape(bs, ts, -1).transpose(1, 0, 2).reshape(ms, -1)

    def to_batch_major(a):
        return a.reshape(ts, bs, -1).transpose(1, 0, 2).reshape(ms, -1)

    def pad_t(a):
        a = a.reshape(bs, ts, -1)
        return jnp.pad(a, ((0, 0), (0, t8 - ts), (0, 0)))

    lru_h_p, lru_c_p, lru_h_s, lru_c_s = [], [], [], []
    gdn_s_p, gdn_c_p, gdn_s_s, gdn_c_s = [], [], [], []

    for i in range(depth):
        j = i // 2
        if i % 2 == 0:
            gu = _mm(x, lru_w_in, g0=j, tn=1024, out_dtype=_F32)[0]
            lam = lru_lambda[j].reshape(1, r)
            cb = lru_conv_b[j].reshape(1, r)
            gb = lru_gate_b[j].reshape(LRU_BLOCKS, 1, -1)
            y, h_p, c_p = _lru_seq(gu, jnp.zeros((bp, r), _F32), jnp.zeros((bp, pad_p, r), _F32),
                                   lru_conv_w[j], cb, lru_gate_w[j], gb, lam,
                                   n_tiles=nchunk, nseg=bp, seg=seg, stride=1, pad=pad_p)
            gu_s = to_time_major(gu[mp:])
            c0_s = state_lru_conv[j].transpose(1, 0, 2).reshape(1, pad_s, r)
            y_s, h_s, c_s = _lru_seq(gu_s, state_lru_h[j], c0_s, lru_conv_w[j], cb, lru_gate_w[j], gb, lam,
                                     n_tiles=1, nseg=1, seg=ms, stride=bs, pad=pad_s)
            y_s = to_batch_major(y_s)
            lru_h_p.append(h_p)
            lru_c_p.append(c_p[:, pad_p - (CONV_W - 1):, :])
            lru_h_s.append(h_s)
            lru_c_s.append(c_s.reshape(CONV_W - 1, bs, r).transpose(1, 0, 2))
            w_out, wi = lru_w_out, j
        else:
            proj = _mm(x, gdn_w_in, g0=j, n=cd + vw, tn=1024, out_dtype=_F32)[0]
            w_tail = gdn_w_in[j, :, cd + vw:]
            w_ba = jnp.zeros((1, d, 2 * LANES), _F32)
            w_ba = w_ba.at[0, :, :GDN_HEADS].set(w_tail[:, :GDN_HEADS])
            w_ba = w_ba.at[0, :, LANES:LANES + GDN_HEADS].set(w_tail[:, GDN_HEADS:])
            ba = _mm(x, w_ba, tn=2 * LANES, out_dtype=_F32)[0]
            alog = jnp.zeros((1, LANES), _F32).at[0, :GDN_HEADS].set(gdn_a_log[j])
            dtb = jnp.zeros((1, LANES), _F32).at[0, :GDN_HEADS].set(gdn_dt_bias[j])
            ng = gdn_norm_g[j].reshape(1, GDN_DV)
            y, s_p, c_p = _gdn_prompt(proj, ba, gdn_conv_w[j], alog, dtb, ng, n_tiles=nchunk, nseg=bp, seg=seg)
            c0_s = jnp.pad(state_gdn_conv[j], ((0, 0), (pad_p - (CONV_W - 1), 0), (0, 0)))
            y_s, s_s, c_s = _gdn_sample(pad_t(proj[mp:]), pad_t(ba[mp:]), state_gdn_s[j], c0_s,
                                        gdn_conv_w[j], alog, dtb, ng, steps=ts)
            y_s = y_s[:, :ts, :].reshape(ms, vw)
            gdn_s_p.append(s_p)
            gdn_c_p.append(c_p[:, pad_p - (CONV_W - 1):, :])
            gdn_s_s.append(s_s)
            gdn_c_s.append(c_s[:, pad_p - (CONV_W - 1):, :])
            w_out, wi = gdn_w_out, j
        x = _mm_ln(y, y_s, w_out, wi, x, ln_g3, ln_b3, 3 * i, alpha=alpha)

        q = _mm(x, xa_wq, g0=i, tn=1024, out_dtype=_BF16)[0]
        o = _attn_prompt(q, mem_k_b[i], mem_v_b[i], n_tiles=nchunk, nseg=bp, seg=seg)
        o_s = _attn_sample(pad_t(q[mp:].astype(_F32)), cache_mem_k, cache_mem_v, i)
        x = _mm_ln(o, o_s[:, :ts, :].reshape(ms, d), xa_wo, i, x, ln_g3, ln_b3, 3 * i + 1, alpha=alpha)

        if i % 2 == 0:
            hdn = _gate_up(x, ffn_w_gu, g0=j, ng=1, f=d_ff, tf=512)[0]
            x = _mm_ln(hdn, None, ffn_w_down, j, x, ln_g3, ln_b3, 3 * i + 2, alpha=alpha)
        else:
            router_pad = jnp.zeros((d, LANES), _F32).at[:, :N_EXPERTS].set(moe_router[j])
            plan = _route_plan(_router(x, router_pad), ROW_TILE)
            y2 = _moe_routed(x, plan, moe_w_gu3, moe_w_down3, j * N_EXPERTS, tm=ROW_TILE)
            x = _combine_ln(y2, x, ln_g3, ln_b3, 3 * i + 2, alpha=alpha,
                            prompt_shape=(bp, nchunk, seg) if i == depth - 1 else None)

    y_prompt = x[0].reshape(bp, seq, d)
    y_sample = x[1].reshape(bs, ts, d)
    hd = d // XA_HEADS
    return (y_prompt, y_sample,
            jnp.stack(lru_h_p), jnp.stack(lru_c_p), jnp.stack(gdn_s_p), jnp.stack(gdn_c_p),
            mem_k_p.reshape(depth, bp, mem, XA_HEADS, hd), mem_v_p.reshape(depth, bp, mem, XA_HEADS, hd),
            jnp.stack(lru_h_s), jnp.stack(lru_c_s), jnp.stack(gdn_s_s), jnp.stack(gdn_c_s))
```

```python
import functools
import math

import jax
import jax.numpy as jnp
from jax import lax
from jax.experimental import pallas as pl
from jax.experimental.pallas import tpu as pltpu

_F32 = jnp.float32
_BF16 = jnp.bfloat16
_HI = lax.Precision.HIGHEST

CONV_W = 4
LRU_BLOCKS = 4
LRU_C = 8.0
GDN_HEADS = 8
GDN_DK = 128
GDN_DV = 128
GDN_CHUNK = 64
GDN_SEQS_PER_STEP = 2
XA_HEADS = 4
N_EXPERTS = 8
IDX_SHIFT = 15
LN_EPS = 1e-5
RMS_EPS = 1e-6
LANES = 128
SUBLANES = 8
ROW_TILE = 512
VMEM_LIMIT_MB = 56


def _big_row_tile(m):
    return 3 * ROW_TILE if m % (3 * ROW_TILE) == 0 else min(ROW_TILE, m)


def _cparams(ndim, vmem_mb=VMEM_LIMIT_MB):
    return pltpu.CompilerParams(dimension_semantics=("arbitrary",) * ndim,
                                vmem_limit_bytes=vmem_mb * 1024 * 1024)


def _dot(a, b, precision=None):
    return jnp.dot(a, b, preferred_element_type=_F32, precision=precision)


def _bdot(a, b):
    return _dot(a.astype(_BF16), b.astype(_BF16))


def _dot_nt(a, b):
    return lax.dot_general(a, b, (((1,), (1,)), ((), ())), preferred_element_type=_F32)


def _dot_tn(a, b):
    return lax.dot_general(a, b, (((0,), (0,)), ((), ())), preferred_element_type=_F32)


def _layer_norm(v, g, b):
    mu = jnp.mean(v, axis=-1, keepdims=True)
    d = v - mu
    var = jnp.mean(d * d, axis=-1, keepdims=True)
    return d * lax.rsqrt(var + LN_EPS) * g + b


def _mm_body(x_ref, w_ref, o_ref, wb_ref):
    @pl.when(pl.program_id(2) == 0)
    def _():
        wb_ref[...] = w_ref[0].astype(_BF16)

    o_ref[0] = _dot(x_ref[...].astype(_BF16), wb_ref[...]).astype(o_ref.dtype)


def _mm(x, w, *, g0=0, ng=1, n0=0, n=None, tn, out_dtype):
    m, k = x.shape
    n = w.shape[2] if n is None else n
    tm = _big_row_tile(m)
    nb0 = n0 // tn
    return pl.pallas_call(
        _mm_body,
        grid=(ng, n // tn, m // tm),
        in_specs=[pl.BlockSpec((tm, k), lambda g, j, i: (i, 0)),
                  pl.BlockSpec((1, k, tn), lambda g, j, i: (g + g0, 0, j + nb0))],
        out_specs=pl.BlockSpec((1, tm, tn), lambda g, j, i: (g, i, j)),
        out_shape=jax.ShapeDtypeStruct((ng, m, n), out_dtype),
        scratch_shapes=[pltpu.VMEM((k, tn), _BF16)],
        compiler_params=_cparams(3),
        name="mm",
    )(x, w)


def _mm_ln_body(y_ref, y2_ref, w_ref, x_ref, g_ref, b_ref, o_ref, ob_ref, wb_ref, *, alpha, n1):
    i = pl.program_id(0)

    @pl.when(i == 0)
    def _():
        wb_ref[...] = w_ref[0].astype(_BF16)

    def emit(src_ref):
        mix = _dot(src_ref[...].astype(_BF16), wb_ref[...])
        res = _layer_norm(alpha * x_ref[...] + mix, g_ref[0], b_ref[0])
        o_ref[...] = res
        ob_ref[...] = res.astype(_BF16)

    if n1 is None:
        emit(y_ref)
    else:
        pl.when(i < n1)(lambda: emit(y_ref))
        pl.when(i >= n1)(lambda: emit(y2_ref))


def _mm_ln(y, y2, w, gi, x, ln_g, ln_b, li, *, alpha):
    m, d = x.shape
    k = y.shape[1]
    tm = min(ROW_TILE, m)
    n1 = y.shape[0] // tm
    if y2 is None:
        y2 = y
    assert n1 * tm == y.shape[0] and (n1 == m // tm or n1 * tm + y2.shape[0] == m)
    return pl.pallas_call(
        functools.partial(_mm_ln_body, alpha=alpha, n1=None if n1 == m // tm else n1),
        grid=(m // tm,),
        in_specs=[pl.BlockSpec((tm, k), lambda i: (jnp.minimum(i, n1 - 1), 0)),
                  pl.BlockSpec((tm, k), lambda i: (jnp.maximum(i - n1, 0), 0)),
                  pl.BlockSpec((1, k, d), lambda i: (gi, 0, 0), pipeline_mode=pl.Buffered(1)),
                  pl.BlockSpec((tm, d), lambda i: (i, 0)),
                  pl.BlockSpec((1, 1, d), lambda i: (li, 0, 0)),
                  pl.BlockSpec((1, 1, d), lambda i: (li, 0, 0))],
        out_specs=[pl.BlockSpec((tm, d), lambda i: (i, 0)), pl.BlockSpec((tm, d), lambda i: (i, 0))],
        out_shape=[jax.ShapeDtypeStruct((m, d), _F32), jax.ShapeDtypeStruct((m, d), _BF16)],
        scratch_shapes=[pltpu.VMEM((k, d), _BF16)],
        compiler_params=_cparams(1),
        name="mm_ln",
    )(y, y2, w, x, ln_g, ln_b)


def _gu_body(x_ref, wg_ref, wu_ref, o_ref, wgb_ref, wub_ref):
    @pl.when(pl.program_id(2) == 0)
    def _():
        wgb_ref[...] = wg_ref[0].astype(_BF16)
        wub_ref[...] = wu_ref[0].astype(_BF16)

    xb = x_ref[...].astype(_BF16)
    g = _dot(xb, wgb_ref[...])
    u = _dot(xb, wub_ref[...])
    o_ref[0] = (jax.nn.silu(g) * u).astype(o_ref.dtype)


def _gate_up(x, w_gu, *, g0, ng, f, tf):
    m, k = x.shape
    tm = _big_row_tile(m)
    nf = f // tf
    return pl.pallas_call(
        _gu_body,
        grid=(ng, nf, m // tm),
        in_specs=[pl.BlockSpec((tm, k), lambda g, j, i: (i, 0)),
                  pl.BlockSpec((1, k, tf), lambda g, j, i: (g + g0, 0, j)),
                  pl.BlockSpec((1, k, tf), lambda g, j, i: (g + g0, 0, j + nf))],
        out_specs=pl.BlockSpec((1, tm, tf), lambda g, j, i: (g, i, j)),
        out_shape=jax.ShapeDtypeStruct((ng, m, f), _BF16),
        scratch_shapes=[pltpu.VMEM((k, tf), _BF16), pltpu.VMEM((k, tf), _BF16)],
        compiler_params=_cparams(3),
        name="gate_up",
    )(x, w_gu, w_gu)


def _router_body(x_ref, r_ref, gate_ref):
    logits = _dot(x_ref[...], r_ref[...], precision=_HI)
    lane = lax.broadcasted_iota(jnp.int32, logits.shape, 1).astype(_F32)
    l1 = jnp.where(lane < N_EXPERTS, logits, -jnp.inf)
    m1 = jnp.max(l1, axis=-1, keepdims=True)
    i1 = jnp.min(jnp.where(l1 == m1, lane, float(LANES)), axis=-1, keepdims=True)
    l2 = jnp.where(lane == i1, -jnp.inf, l1)
    m2 = jnp.max(l2, axis=-1, keepdims=True)
    i2 = jnp.min(jnp.where(l2 == m2, lane, float(LANES)), axis=-1, keepdims=True)
    e2 = jnp.exp(m2 - m1)
    den = 1.0 + e2
    gate_ref[...] = (jnp.where(lane == 0.0, i1, 0.0) + jnp.where(lane == 1.0, i2, 0.0)
                     + jnp.where(lane == 2.0, 1.0 / den, 0.0) + jnp.where(lane == 3.0, e2 / den, 0.0))


def _router(x, router_pad):
    m, k = x.shape
    tm = min(ROW_TILE, m)
    return pl.pallas_call(
        _router_body,
        grid=(m // tm,),
        in_specs=[pl.BlockSpec((tm, k), lambda i: (i, 0)),
                  pl.BlockSpec((k, LANES), lambda i: (0, 0))],
        out_specs=pl.BlockSpec((tm, LANES), lambda i: (i, 0)),
        out_shape=jax.ShapeDtypeStruct((m, LANES), _F32),
        compiler_params=_cparams(1),
        name="router",
    )(x, router_pad)


def _route_plan(rt, tm):
    m = rt.shape[0]
    ef = rt[:, :2].astype(jnp.int32).reshape(-1)
    wf = rt[:, 2:4].reshape(-1)
    onehot = (ef[:, None] == jnp.arange(N_EXPERTS, dtype=jnp.int32)[None, :]).astype(jnp.int32)
    csum = jnp.cumsum(onehot, axis=0)
    cnt = csum[-1]
    rank = jnp.sum((csum - onehot) * onehot, axis=1)
    cnt_p = ((cnt + tm - 1) // tm) * tm
    end = jnp.cumsum(cnt_p)
    start = end - cnt_p
    dest = start[ef] + rank
    n_tiles = (2 * m) // tm + N_EXPERTS
    rows = n_tiles * tm
    pair = jnp.zeros((rows,), jnp.int32).at[dest].set(jnp.arange(2 * m, dtype=jnp.int32))
    tile_start = jnp.arange(n_tiles, dtype=jnp.int32) * tm
    texp = jnp.sum((tile_start[:, None] >= end[None, :]).astype(jnp.int32), axis=1)
    used = tile_start < end[-1]
    last = jnp.maximum(end[-1] // tm - 1, 0)
    texp = jnp.where(used, texp, texp[last])
    nval = jnp.where(used, jnp.clip(start[texp] + cnt[texp] - tile_start, 0, tm), 0)
    tile = jnp.repeat(jnp.arange(n_tiles, dtype=jnp.int32), tm)
    in_tile = jnp.arange(rows, dtype=jnp.int32) - tile * tm
    live = in_tile < jnp.repeat(nval, tm)
    tok, slot = pair // 2, pair % 2
    spare = 2 * m + (tile % 2) * tm + in_tile
    idx = jnp.where(live, tok, 0) | (jnp.where(live, slot * m + tok, spare) << IDX_SHIFT)
    rw = jnp.where(live, wf[pair], 0.0).reshape(rows, 1)
    n_used = (end[-1] // tm).reshape(1)
    return texp, n_used, idx, rw


def _moe_routed_body(texp_ref, nused_ref, idx_ref, x_hbm, rw_ref, wg_ref, wu_ref, wd_ref, out_hbm,
                     xs_ref, ys_ref, gsem, ssem):
    i = pl.program_id(0)
    ng = xs_ref.shape[1]
    tm = ng * SUBLANES
    d = xs_ref.shape[3]
    n_used = nused_ref[0]
    slot = i % 2

    def gather(tile, s):
        def body(g, c):
            for u in range(SUBLANES):
                tok = idx_ref[tile * tm + g * SUBLANES + u] & ((1 << IDX_SHIFT) - 1)
                pltpu.make_async_copy(x_hbm.at[tok >> 3, tok & 7], xs_ref.at[s, g, u], gsem.at[s]).start()
            return c
        lax.fori_loop(0, ng, body, 0)

    def scatter(tile, s):
        def body(g, c):
            for u in range(SUBLANES):
                dst = idx_ref[tile * tm + g * SUBLANES + u] >> IDX_SHIFT
                pltpu.make_async_copy(ys_ref.at[s, g, u], out_hbm.at[dst >> 3, dst & 7], ssem.at[s]).start()
            return c
        lax.fori_loop(0, ng, body, 0)

    def wait_gather(s):
        pltpu.make_async_copy(x_hbm.at[pl.ds(0, ng)], xs_ref.at[s], gsem.at[s]).wait()

    def wait_scatter(s):
        pltpu.make_async_copy(ys_ref.at[s], out_hbm.at[pl.ds(0, ng)], ssem.at[s]).wait()

    @pl.when((i == 0) & (n_used > 0))
    def _():
        gather(0, 0)

    @pl.when(i + 1 < n_used)
    def _():
        gather(i + 1, 1 - slot)

    @pl.when(i < n_used)
    def _():
        wait_gather(slot)

        @pl.when(i >= 2)
        def _():
            wait_scatter(slot)

        xb = xs_ref[slot].reshape(tm, d).astype(_BF16)
        h = (jax.nn.silu(_dot(xb, wg_ref[0])) * _dot(xb, wu_ref[0])).astype(_BF16)
        ys_ref[slot] = (_dot(h, wd_ref[0]) * rw_ref[...]).reshape(ng, SUBLANES, d)
        scatter(i, slot)

    @pl.when(i == pl.num_programs(0) - 1)
    def _():
        @pl.when(n_used >= 1)
        def _():
            wait_scatter((n_used - 1) % 2)

        @pl.when(n_used >= 2)
        def _():
            wait_scatter(n_used % 2)

        ys_ref[0] = jnp.zeros(ys_ref.shape[1:], _F32)
        spare0 = out_hbm.shape[0] - 2 * ng
        fills = [pltpu.make_async_copy(ys_ref.at[0], out_hbm.at[pl.ds(spare0 + k * ng, ng)], ssem.at[0])
                 for k in range(2)]
        for cp in fills:
            cp.start()
        for cp in fills:
            cp.wait()


def _moe_routed(x, plan, w_gu_bf16, w_down_bf16, e0, *, tm):
    texp, n_used, idx, rw = plan
    m, d = x.shape
    f = w_down_bf16.shape[1]
    n_tiles = texp.shape[0]
    return pl.pallas_call(
        _moe_routed_body,
        grid_spec=pltpu.PrefetchScalarGridSpec(
            num_scalar_prefetch=3,
            grid=(n_tiles,),
            in_specs=[pl.BlockSpec(memory_space=pl.ANY),
                      pl.BlockSpec((tm, 1), lambda i, te, nu, ix: (i, 0)),
                      pl.BlockSpec((1, d, f), lambda i, te, nu, ix: (te[i] + e0, 0, 0)),
                      pl.BlockSpec((1, d, f), lambda i, te, nu, ix: (te[i] + e0, 0, 1)),
                      pl.BlockSpec((1, f, d), lambda i, te, nu, ix: (te[i] + e0, 0, 0))],
            out_specs=pl.BlockSpec(memory_space=pl.ANY),
            scratch_shapes=[pltpu.VMEM((2, tm // SUBLANES, SUBLANES, d), _F32),
                            pltpu.VMEM((2, tm // SUBLANES, SUBLANES, d), _F32),
                            pltpu.SemaphoreType.DMA((2,)), pltpu.SemaphoreType.DMA((2,))]),
        out_shape=jax.ShapeDtypeStruct(((2 * m + 2 * tm) // SUBLANES, SUBLANES, d), _F32),
        compiler_params=_cparams(1),
        name="moe_routed",
    )(texp, n_used, idx, x.reshape(m // SUBLANES, SUBLANES, d), rw, w_gu_bf16, w_gu_bf16,
      w_down_bf16).reshape(2 * m + 2 * tm, d)


def _combine_ln_body(a_ref, b_ref, x_ref, g_ref, bb_ref, *o_refs, alpha, n_prompt):
    res = _layer_norm(alpha * x_ref[...] + (a_ref[...] + b_ref[...]), g_ref[0], bb_ref[0])
    if n_prompt is None:
        o_refs[0][...] = res
        o_refs[1][...] = res.astype(_BF16)
    else:
        i = pl.program_id(0)
        op_ref, os_ref = o_refs

        @pl.when(i < n_prompt)
        def _():
            op_ref[...] = res.reshape(op_ref.shape)

        @pl.when(i >= n_prompt)
        def _():
            os_ref[...] = res


def _combine_ln(y2, x, ln_g, ln_b, li, *, alpha, prompt_shape=None):
    m, d = x.shape
    tm = min(ROW_TILE, m)
    nt = m // tm
    if prompt_shape is None:
        n_prompt = None
        out_specs = [pl.BlockSpec((tm, d), lambda i: (i, 0)), pl.BlockSpec((tm, d), lambda i: (i, 0))]
        out_shape = [jax.ShapeDtypeStruct((m, d), _F32), jax.ShapeDtypeStruct((m, d), _BF16)]
    else:
        bp, n_prompt, seg = prompt_shape
        assert bp * seg == tm and n_prompt == nt - 1
        out_specs = [pl.BlockSpec((bp, 1, seg, d), lambda i: (0, jnp.minimum(i, n_prompt - 1), 0, 0)),
                     pl.BlockSpec((tm, d), lambda i: (0, 0))]
        out_shape = [jax.ShapeDtypeStruct((bp, n_prompt, seg, d), _F32), jax.ShapeDtypeStruct((tm, d), _F32)]
    return pl.pallas_call(
        functools.partial(_combine_ln_body, alpha=alpha, n_prompt=n_prompt),
        grid=(nt,),
        in_specs=[pl.BlockSpec((tm, d), lambda i: (i, 0)),
                  pl.BlockSpec((tm, d), lambda i: (i + nt, 0)),
                  pl.BlockSpec((tm, d), lambda i: (i, 0)),
                  pl.BlockSpec((1, 1, d), lambda i: (li, 0, 0)),
                  pl.BlockSpec((1, 1, d), lambda i: (li, 0, 0))],
        out_specs=out_specs,
        out_shape=out_shape,
        compiler_params=_cparams(1),
        name="combine_ln",
    )(y2, y2, x, ln_g, ln_b)


def _pack_body(xp_ref, xs_ref, o_ref, ob_ref, *, n_prompt):
    i = pl.program_id(0)

    @pl.when(i < n_prompt)
    def _():
        o_ref[...] = xp_ref[...].reshape(o_ref.shape)
        ob_ref[...] = xp_ref[...].reshape(o_ref.shape).astype(_BF16)

    @pl.when(i >= n_prompt)
    def _():
        o_ref[...] = xs_ref[...]
        ob_ref[...] = xs_ref[...].astype(_BF16)


def _pack_tokens(x_prompt4, x_sample2):
    bp, n_prompt, seg, d = x_prompt4.shape
    tm = bp * seg
    assert x_sample2.shape == (tm, d)
    return pl.pallas_call(
        functools.partial(_pack_body, n_prompt=n_prompt),
        grid=(n_prompt + 1,),
        in_specs=[pl.BlockSpec((bp, 1, seg, d), lambda i: (0, jnp.minimum(i, n_prompt - 1), 0, 0)),
                  pl.BlockSpec((tm, d), lambda i: (0, 0))],
        out_specs=[pl.BlockSpec((tm, d), lambda i: (i, 0)), pl.BlockSpec((tm, d), lambda i: (i, 0))],
        out_shape=[jax.ShapeDtypeStruct(((n_prompt + 1) * tm, d), _F32),
                   jax.ShapeDtypeStruct(((n_prompt + 1) * tm, d), _BF16)],
        compiler_params=_cparams(1),
        name="pack_tokens",
    )(x_prompt4, x_sample2)


def _seg_scan(a, b, tstep, n_steps, stride):
    sh = stride
    for s in range(n_steps):
        keep = tstep >= (1 << s)
        a_sh = pltpu.roll(a, sh, 0)
        b_sh = pltpu.roll(b, sh, 0)
        b = jnp.where(keep, a * b_sh + b, b)
        a = jnp.where(keep, a * a_sh, a)
        sh *= 2
    return a, b


def _lru_body(gate_ref, rec_ref, h0_ref, c0_ref, cw_ref, cb_ref, gw_ref, gb_ref, lam_ref,
              y_ref, hout_ref, cout_ref, xp_ref, hc_ref, cv_ref, gwb_ref, *, nseg, seg, stride, pad):
    rows = nseg * seg
    bw = gw_ref.shape[1]

    @pl.when(pl.program_id(0) == 0)
    def _():
        xp_ref[:, 0:pad, :] = c0_ref[...]
        hc_ref[...] = h0_ref[...]
        gwb_ref[...] = gw_ref[...].astype(_BF16)

    for s in range(nseg):
        xp_ref[s, pad:pad + seg, :] = rec_ref[s * seg:(s + 1) * seg, :]
        acc = xp_ref[s, pad - 3 * stride:pad - 3 * stride + seg, :] * cw_ref[0:1, :]
        for j in range(1, CONV_W):
            off = pad - (CONV_W - 1 - j) * stride
            acc = acc + xp_ref[s, off:off + seg, :] * cw_ref[j:j + 1, :]
        cv_ref[s * seg:(s + 1) * seg, :] = acc + cb_ref[...]
        tail = xp_ref[s, seg:seg + pad, :]
        xp_ref[s, 0:pad, :] = tail
        cout_ref[s] = tail

    row = lax.broadcasted_iota(jnp.int32, (rows, bw), 0)
    if stride == 1:
        tstep = row & (seg - 1)
    else:
        tstep = row >> int(math.log2(stride))
    n_steps = int(math.log2(seg // stride))

    for n in range(LRU_BLOCKS):
        cs = slice(n * bw, (n + 1) * bw)
        cv = cv_ref[:, cs]
        gates = _dot(cv.astype(_BF16), gwb_ref[n]) + gb_ref[n]
        r = jax.nn.sigmoid(gates[:, :bw])
        ig = jax.nn.sigmoid(gates[:, bw:])
        log_a = -LRU_C * r * jax.nn.softplus(-lam_ref[:, cs])
        a = jnp.exp(log_a)
        s = -jnp.tanh(log_a) * (a * a + 1.0)
        mult = jnp.where(s > 0.0, s * lax.rsqrt(s), 0.0)
        a_cum, b_cum = _seg_scan(a, mult * ig * cv, tstep, n_steps, stride)
        hc = hc_ref[:, cs]
        if stride == 1:
            hprev = jnp.concatenate([jnp.broadcast_to(hc[s:s + 1, :], (seg, bw)) for s in range(nseg)], axis=0)
        else:
            hprev = jnp.concatenate([hc] * (seg // stride), axis=0)
        h = a_cum * hprev + b_cum
        y_ref[:, cs] = (h * jax.nn.gelu(gate_ref[:, cs])).astype(y_ref.dtype)
        if stride == 1:
            hc_ref[:, cs] = jnp.concatenate([h[(s + 1) * seg - 1:(s + 1) * seg, :] for s in range(nseg)], axis=0)
        else:
            hc_ref[:, cs] = h[rows - stride:, :]

    hout_ref[...] = hc_ref[...]


def _lru_seq(gu, h0, c0, conv_w, conv_b, gate_w, gate_b, lam, *, n_tiles, nseg, seg, stride, pad):
    rows = nseg * seg
    r = gu.shape[1] // 2
    bw = r // LRU_BLOCKS
    const2 = lambda i: (0, 0)
    const3 = lambda i: (0, 0, 0)
    return pl.pallas_call(
        functools.partial(_lru_body, nseg=nseg, seg=seg, stride=stride, pad=pad),
        grid=(n_tiles,),
        in_specs=[pl.BlockSpec((rows, r), lambda i: (i, 0)),
                  pl.BlockSpec((rows, r), lambda i: (i, 1)),
                  pl.BlockSpec((nseg * stride, r), const2),
                  pl.BlockSpec((nseg, pad, r), const3),
                  pl.BlockSpec((CONV_W, r), const2),
                  pl.BlockSpec((1, r), const2),
                  pl.BlockSpec((LRU_BLOCKS, bw, 2 * bw), const3),
                  pl.BlockSpec((LRU_BLOCKS, 1, 2 * bw), const3),
                  pl.BlockSpec((1, r), const2)],
        out_specs=[pl.BlockSpec((rows, r), lambda i: (i, 0)),
                   pl.BlockSpec((nseg * stride, r), const2),
                   pl.BlockSpec((nseg, pad, r), const3)],
        out_shape=[jax.ShapeDtypeStruct((n_tiles * rows, r), _BF16),
                   jax.ShapeDtypeStruct((nseg * stride, r), _F32),
                   jax.ShapeDtypeStruct((nseg, pad, r), _F32)],
        scratch_shapes=[pltpu.VMEM((nseg, pad + seg, r), _F32),
                        pltpu.VMEM((nseg * stride, r), _F32),
                        pltpu.VMEM((rows, r), _F32),
                        pltpu.VMEM((LRU_BLOCKS, bw, 2 * bw), _BF16)],
        compiler_params=_cparams(1),
        name="lru_seq",
    )(gu, gu, h0, c0, conv_w, conv_b, gate_w, gate_b, lam)


def _l2norm(t):
    return t * lax.rsqrt(jnp.sum(t * t, axis=-1, keepdims=True) + RMS_EPS)


def _gdn_chunks(seqs, norm_g, *, c, hg):
    dk = GDN_DK
    qk_w = GDN_HEADS * dk
    r4 = hg * c
    lg_c = int(math.log2(c))
    row = lax.broadcasted_iota(jnp.int32, (r4, r4), 0)
    col = lax.broadcasted_iota(jnp.int32, (r4, r4), 1)
    blk = {1 << s: (row >> s) == (col >> s) for s in range(3, lg_c + 1)}
    incl = blk[c] & (row >= col)
    strict = blk[c] & (row > col)
    units = [(si, list(range(g0, g0 + hg))) for si in range(len(seqs)) for g0 in range(0, GDN_HEADS, hg)]
    blocks = [slice(hh * c, (hh + 1) * c) for hh in range(hg)]

    def stack(fn, hs, axis=0):
        return jnp.concatenate([fn(h) for h in hs], axis=axis)

    tr = lax.broadcasted_iota(jnp.int32, (c, c), 0)
    tc = lax.broadcasted_iota(jnp.int32, (c, c), 1)
    tri = jnp.where(tr >= tc, 1.0, 0.0)
    gc = [_dot(tri, sq[3], precision=_HI) for sq in seqs]
    gct = [jnp.concatenate([x, jnp.zeros((LANES - c, LANES), _F32)], axis=0).T for x in gc]

    q4 = [stack(lambda h: _l2norm(seqs[si][0](h * dk)) * (dk ** -0.5), hs) for si, hs in units]
    k4 = [stack(lambda h: _l2norm(seqs[si][0](qk_w + h * dk)), hs) for si, hs in units]
    v4 = [stack(lambda h: seqs[si][0](2 * qk_w + h * dk), hs) for si, hs in units]
    beta4 = [stack(lambda h: jnp.broadcast_to(seqs[si][2][:, h:h + 1], (c, dk)), hs) for si, hs in units]
    gc4 = [stack(lambda h: jnp.broadcast_to(gc[si][:, h:h + 1], (c, dk)), hs) for si, hs in units]
    gcol = [stack(lambda h: jnp.broadcast_to(gc[si][:, h:h + 1], (c, r4)), hs) for si, hs in units]
    grow = [stack(lambda h: gct[si][h:h + 1, 0:c], hs, axis=1) for si, hs in units]
    decay = [jnp.exp(jnp.where(incl, gcl - grw, -jnp.inf)) for gcl, grw in zip(gcol, grow)]
    kb4 = [k * bt for k, bt in zip(k4, beta4)]
    aq = [_dot_nt(jnp.concatenate([kb, q], axis=0).astype(_BF16), k.astype(_BF16)) for kb, q, k in zip(kb4, q4, k4)]
    a4 = [jnp.where(strict, x[:r4] * dc, 0.0) for x, dc in zip(aq, decay)]
    attn4 = [x[r4:] * dc for x, dc in zip(aq, decay)]
    c0 = min(c, 16)
    np_ = [-jnp.where(blk[c0], x, 0.0) for x in a4]
    e = list(np_)
    for _ in range(int(math.log2(c0)) - 1):
        np_ = [_bdot(x, x) for x in np_]
        e = [ei + ni + _bdot(ni, ei) for ei, ni in zip(e, np_)]
    size = c0
    while size < c:
        lo = [jnp.where(blk[2 * size] & ~blk[size], x, 0.0) for x in a4]
        m = [li + _bdot(li, ei) for li, ei in zip(lo, e)]
        e = [ei - (mi + _bdot(ei, mi)) for ei, mi in zip(e, m)]
        size *= 2
    egc4 = [jnp.exp(x) for x in gc4]
    rhs = [jnp.concatenate([v * bt, kb * eg], axis=1) for v, bt, kb, eg in zip(v4, beta4, kb4, egc4)]
    uw = [r + _bdot(ei, r) for r, ei in zip(rhs, e)]
    qg4 = [q * eg for q, eg in zip(q4, egc4)]
    ws_qs = [[_dot(jnp.concatenate([uw[u][rs, dk:], qg4[u][rs]], axis=0).astype(_BF16),
                   seqs[si][4][h].astype(_BF16)) for rs, h in zip(blocks, hs)]
             for u, (si, hs) in enumerate(units)]
    vn = [[uw[u][rs, :dk] - ws_qs[u][hh][:c] for hh, rs in enumerate(blocks)] for u in range(len(units))]
    o4 = [jnp.concatenate([x[c:] for x in ws_qs[u]], axis=0)
          + _bdot(attn4[u], jnp.concatenate(vn[u], axis=0)) for u in range(len(units))]
    outs = [[None] * GDN_HEADS for _ in seqs]
    new_state = [[None] * GDN_HEADS for _ in seqs]
    for u, (si, hs) in enumerate(units):
        z, state = seqs[si][1], seqs[si][4]
        for hh, (rs, h) in enumerate(zip(blocks, hs)):
            glast = gc[si][c - 1:c, h:h + 1]
            kd = k4[u][rs] * jnp.exp(glast - gc4[u][rs])
            new_state[si][h] = state[h] * jnp.exp(glast) + _dot_tn(kd.astype(_BF16), vn[u][hh].astype(_BF16))
            oh = o4[u][rs]
            oh = oh * lax.rsqrt(jnp.mean(oh * oh, axis=-1, keepdims=True) + RMS_EPS) * norm_g
            outs[si][h] = oh * jax.nn.silu(z[:, h * dk:(h + 1) * dk])
    return outs, new_state


def _gdn_gates(ba, alog, dtb):
    beta = jax.nn.sigmoid(ba[:, :LANES])
    g = -jnp.exp(alog) * jax.nn.softplus(ba[:, LANES:] + dtb)
    return beta, g


def _gdn_conv(xp_ref, s, x, cw_ref, *, pad, seg):
    xp_ref[s, pad:pad + seg, :] = x

    def block(col):
        cs = slice(col, col + GDN_DK)
        acc = xp_ref[s, pl.ds(pad - 3, seg), cs] * cw_ref[0:1, cs]
        for j in range(1, CONV_W):
            acc = acc + xp_ref[s, pl.ds(pad - (CONV_W - 1 - j), seg), cs] * cw_ref[j:j + 1, cs]
        return jax.nn.silu(acc)

    return block


def _gdn_prompt_body(qkv_ref, z_ref, ba_ref, cw_ref, alog_ref, dtb_ref, ng_ref,
                     o_ref, sout_ref, cout_ref, xp_ref, s_ref, *, nseg, seg, pad):
    @pl.when(pl.program_id(0) == 0)
    def _():
        xp_ref[...] = jnp.zeros(xp_ref.shape, _F32)
        s_ref[...] = jnp.zeros(s_ref.shape, _F32)

    def per_group(i, carry):
        bs = [i * GDN_SEQS_PER_STEP + k for k in range(GDN_SEQS_PER_STEP)]
        r0 = [pl.multiple_of(b * seg, seg) for b in bs]
        seqs = []
        for b, r in zip(bs, r0):
            qkv = _gdn_conv(xp_ref, b, qkv_ref[pl.ds(r, seg), :], cw_ref, pad=pad, seg=seg)
            beta, g = _gdn_gates(ba_ref[pl.ds(r, seg), :], alog_ref[...], dtb_ref[...])
            seqs.append((qkv, z_ref[pl.ds(r, seg), :], beta, g, [s_ref[b, h] for h in range(GDN_HEADS)]))
        outs, new_state = _gdn_chunks(seqs, ng_ref[...], c=seg, hg=256 // seg)
        for k, (b, r) in enumerate(zip(bs, r0)):
            tail = xp_ref[b, pl.ds(seg, pad), :]
            xp_ref[b, 0:pad, :] = tail
            cout_ref[b] = tail
            for h in range(GDN_HEADS):
                s_ref[b, h] = new_state[k][h]
                o_ref[pl.ds(r, seg), h * GDN_DV:(h + 1) * GDN_DV] = outs[k][h].astype(o_ref.dtype)
        return carry

    lax.fori_loop(0, nseg // GDN_SEQS_PER_STEP, per_group, 0)

    @pl.when(pl.program_id(0) == pl.num_programs(0) - 1)
    def _():
        sout_ref[...] = s_ref[...]


def _gdn_prompt(proj, ba, conv_w, alog, dtb, norm_g, *, n_tiles, nseg, seg):
    rows = nseg * seg
    cd = conv_w.shape[1]
    vw = GDN_HEADS * GDN_DV
    pad = 8
    const2 = lambda i: (0, 0)
    return pl.pallas_call(
        functools.partial(_gdn_prompt_body, nseg=nseg, seg=seg, pad=pad),
        grid=(n_tiles,),
        in_specs=[pl.BlockSpec((rows, cd), lambda i: (i, 0)),
                  pl.BlockSpec((rows, vw), lambda i: (i, cd // vw)),
                  pl.BlockSpec((rows, 2 * LANES), lambda i: (i, 0)),
                  pl.BlockSpec((CONV_W, cd), const2),
                  pl.BlockSpec((1, LANES), const2),
                  pl.BlockSpec((1, LANES), const2),
                  pl.BlockSpec((1, GDN_DV), const2)],
        out_specs=[pl.BlockSpec((rows, vw), lambda i: (i, 0)),
                   pl.BlockSpec((nseg, GDN_HEADS, GDN_DK, GDN_DV), lambda i: (0, 0, 0, 0)),
                   pl.BlockSpec((nseg, pad, cd), lambda i: (0, 0, 0))],
        out_shape=[jax.ShapeDtypeStruct((n_tiles * rows, vw), _BF16),
                   jax.ShapeDtypeStruct((nseg, GDN_HEADS, GDN_DK, GDN_DV), _F32),
                   jax.ShapeDtypeStruct((nseg, pad, cd), _F32)],
        scratch_shapes=[pltpu.VMEM((nseg, pad + seg, cd), _F32),
                        pltpu.VMEM((nseg, GDN_HEADS, GDN_DK, GDN_DV), _F32)],
        compiler_params=_cparams(1),
        name="gdn_prompt",
    )(proj, proj, ba, conv_w, alog, dtb, norm_g)


def _gdn_sample_body(qkv_ref, z_ref, ba_ref, s0_ref, c0_ref, cw_ref, alog_ref, dtb_ref, ng_ref, sall_ref,
                     o_ref, sout_ref, cout_ref, xp_ref, *, nb, seg, steps, pad):
    del sall_ref
    s0_ref, sout_ref = s0_ref.at[0], sout_ref.at[0]
    def per_group(i, carry):
        bs = [i * GDN_SEQS_PER_STEP + k for k in range(GDN_SEQS_PER_STEP)]
        seqs = []
        for k, b in enumerate(bs):
            xp_ref[k, 0:pad, :] = c0_ref[b]
            qkv = _gdn_conv(xp_ref, k, qkv_ref[b], cw_ref, pad=pad, seg=seg)
            cout_ref[b] = xp_ref[k, pl.ds(steps, pad), :]
            beta, g = _gdn_gates(ba_ref[b], alog_ref[...], dtb_ref[...])
            live = lax.broadcasted_iota(jnp.int32, beta.shape, 0) < steps
            seqs.append((qkv, z_ref[b], jnp.where(live, beta, 0.0), jnp.where(live, g, 0.0),
                         [s0_ref[b, h] for h in range(GDN_HEADS)]))
        outs, new_state = _gdn_chunks(seqs, ng_ref[...], c=seg, hg=GDN_HEADS)
        for k, b in enumerate(bs):
            for h in range(GDN_HEADS):
                sout_ref[b, h] = new_state[k][h]
                o_ref[b, :, h * GDN_DV:(h + 1) * GDN_DV] = outs[k][h]
        return carry

    lax.fori_loop(0, nb // GDN_SEQS_PER_STEP, per_group, 0)


def _gdn_sample(proj, ba, s0_all, layer, s_new_all, c0, conv_w, alog, dtb, norm_g, *, steps):
    nbatch, seg, _ = proj.shape
    cd = conv_w.shape[1]
    vw = GDN_HEADS * GDN_DV
    pad = c0.shape[1]
    nb = 8
    const2 = lambda i: (0, 0)
    return pl.pallas_call(
        functools.partial(_gdn_sample_body, nb=nb, seg=seg, steps=steps, pad=pad),
        grid=(nbatch // nb,),
        in_specs=[pl.BlockSpec((nb, seg, cd), lambda i: (i, 0, 0)),
                  pl.BlockSpec((nb, seg, vw), lambda i: (i, 0, cd // vw)),
                  pl.BlockSpec((nb, seg, 2 * LANES), lambda i: (i, 0, 0)),
                  pl.BlockSpec((1, nb, GDN_HEADS, GDN_DK, GDN_DV), lambda i: (layer, i, 0, 0, 0)),
                  pl.BlockSpec((nb, pad, cd), lambda i: (i, 0, 0)),
                  pl.BlockSpec((CONV_W, cd), const2),
                  pl.BlockSpec((1, LANES), const2),
                  pl.BlockSpec((1, LANES), const2),
                  pl.BlockSpec((1, GDN_DV), const2),
                  pl.BlockSpec(memory_space=pl.ANY)],
        out_specs=[pl.BlockSpec((nb, seg, vw), lambda i: (i, 0, 0)),
                   pl.BlockSpec((1, nb, GDN_HEADS, GDN_DK, GDN_DV), lambda i: (layer, i, 0, 0, 0)),
                   pl.BlockSpec((nb, pad, cd), lambda i: (i, 0, 0))],
        out_shape=[jax.ShapeDtypeStruct((nbatch, seg, vw), _F32),
                   jax.ShapeDtypeStruct(s_new_all.shape, _F32),
                   jax.ShapeDtypeStruct((nbatch, pad, cd), _F32)],
        scratch_shapes=[pltpu.VMEM((GDN_SEQS_PER_STEP, pad + seg, cd), _F32)],
        input_output_aliases={9: 1},
        compiler_params=_cparams(1),
        name="gdn_sample",
    )(proj, proj, ba, s0_all, c0, conv_w, alog, dtb, norm_g, s_new_all)


def _softmax_rows(s):
    e = jnp.exp(s - jnp.max(s, axis=-1, keepdims=True))
    return e / jnp.sum(e, axis=-1, keepdims=True)


def _attn_prompt_body(q_ref, k_ref, v_ref, o_ref, *, nseg, seg, ntile):
    hd = q_ref.shape[1] // XA_HEADS
    scale = hd ** -0.5
    for b in range(nseg):
        runs = [slice((t * nseg + b) * seg, (t * nseg + b + 1) * seg) for t in range(ntile)]
        heads = [slice(h * hd, (h + 1) * hd) for h in range(XA_HEADS)]
        s = [_dot_nt(jnp.concatenate([q_ref[rs, cs] for rs in runs], axis=0), k_ref[b, :, cs]) * scale
             for cs in heads]
        p = [_softmax_rows(sh).astype(_BF16) for sh in s]
        o = [_dot(ph, v_ref[b, :, cs]).astype(o_ref.dtype) for ph, cs in zip(p, heads)]
        for oh, cs in zip(o, heads):
            for t, rs in enumerate(runs):
                o_ref[rs, cs] = oh[t * seg:(t + 1) * seg]


def _attn_prompt(q, mem_k, mem_v, *, n_tiles, nseg, seg):
    ntile = math.gcd(n_tiles, 4)
    rows = nseg * seg * ntile
    d = q.shape[1]
    mem = mem_k.shape[1]
    return pl.pallas_call(
        functools.partial(_attn_prompt_body, nseg=nseg, seg=seg, ntile=ntile),
        grid=(n_tiles // ntile,),
        in_specs=[pl.BlockSpec((rows, d), lambda i: (i, 0)),
                  pl.BlockSpec((nseg, mem, d), lambda i: (0, 0, 0)),
                  pl.BlockSpec((nseg, mem, d), lambda i: (0, 0, 0))],
        out_specs=pl.BlockSpec((rows, d), lambda i: (i, 0)),
        out_shape=jax.ShapeDtypeStruct((n_tiles * nseg * seg, d), _BF16),
        compiler_params=_cparams(1),
        name="attn_prompt",
    )(q, mem_k, mem_v)


def _attn_sample_body(q_ref, k_ref, v_ref, o_ref, *, nb):
    t8 = q_ref.shape[1]
    mem, nh, hd = k_ref.shape[2:]
    scale = hd ** -0.5
    qh = lax.broadcasted_iota(jnp.int32, (nh * t8, mem * nh), 0) // t8
    kh = lax.broadcasted_iota(jnp.int32, (nh * t8, mem * nh), 1) % nh
    own = qh == kh

    def per_seq(b, carry):
        q = q_ref[b]
        q2 = jnp.concatenate([q[:, h * hd:(h + 1) * hd] for h in range(nh)], axis=0).astype(_BF16)
        k2 = k_ref[0, b].reshape(mem * nh, hd).astype(_BF16)
        v2 = v_ref[0, b].reshape(mem * nh, hd).astype(_BF16)
        s = jnp.where(own, _dot_nt(q2, k2) * scale, -jnp.inf)
        o2 = _dot(_softmax_rows(s).astype(_BF16), v2)
        for h in range(nh):
            o_ref[b, :, h * hd:(h + 1) * hd] = o2[h * t8:(h + 1) * t8]
        return carry

    lax.fori_loop(0, nb, per_seq, 0)


def _attn_sample(q, cache_k, cache_v, layer):
    nbatch, t8, d = q.shape
    mem, nh, hd = cache_k.shape[2:]
    nb = 8
    return pl.pallas_call(
        functools.partial(_attn_sample_body, nb=nb),
        grid=(nbatch // nb,),
        in_specs=[pl.BlockSpec((nb, t8, d), lambda i: (i, 0, 0)),
                  pl.BlockSpec((1, nb, mem, nh, hd), lambda i: (layer, i, 0, 0, 0)),
                  pl.BlockSpec((1, nb, mem, nh, hd), lambda i: (layer, i, 0, 0, 0))],
        out_specs=pl.BlockSpec((nb, t8, d), lambda i: (i, 0, 0)),
        out_shape=jax.ShapeDtypeStruct(q.shape, _F32),
        compiler_params=_cparams(1),
        name="attn_sample",
    )(q, cache_k, cache_v)


def kernel(x_prompt, x_sample, mem_prompt, state_lru_h, state_lru_conv, state_gdn_s, state_gdn_conv,
           cache_mem_k, cache_mem_v, ln_g, ln_b, xa_wq, xa_wk, xa_wv, xa_wo,
           lru_w_in, lru_conv_w, lru_conv_b, lru_gate_w, lru_gate_b, lru_lambda, lru_w_out,
           gdn_w_in, gdn_conv_w, gdn_a_log, gdn_dt_bias, gdn_norm_g, gdn_w_out,
           ffn_w_gu, ffn_w_down, moe_router, moe_w_gu, moe_w_down):
    depth = ln_g.shape[0]
    bp, seq, d = x_prompt.shape
    bs, ts, _ = x_sample.shape
    mem = mem_prompt.shape[1]
    alpha = (2 * depth) ** 0.25
    seg = GDN_CHUNK
    nchunk = seq // seg
    mp, ms = bp * seq, bs * ts
    assert bp * seg == ROW_TILE and ms == ROW_TILE and seq % seg == 0 and ts >= CONV_W - 1
    assert depth % 2 == 0
    t8 = 8
    pad_p = 8
    pad_s = (CONV_W - 1) * bs
    r = lru_w_in.shape[2] // 2
    cd = gdn_conv_w.shape[2]
    vw = GDN_HEADS * GDN_DV
    d_ff = ffn_w_gu.shape[2] // 2
    d_fe = moe_w_gu.shape[3] // 2

    x, xb = _pack_tokens(x_prompt.reshape(bp, nchunk, seg, d), x_sample.reshape(ms, d))

    memx = mem_prompt.reshape(bp * mem, d)
    mem_k_p = _mm(memx, xa_wk, ng=depth, tn=1024, out_dtype=_F32).reshape(depth, bp, mem, d)
    mem_v_p = _mm(memx, xa_wv, ng=depth, tn=1024, out_dtype=_F32).reshape(depth, bp, mem, d)
    mem_k_b = mem_k_p.astype(_BF16)
    mem_v_b = mem_v_p.astype(_BF16)

    ln_g3 = ln_g.reshape(depth * 3, 1, d)
    moe_w_gu3 = moe_w_gu.reshape((-1,) + moe_w_gu.shape[2:]).astype(_BF16)
    moe_w_down3 = moe_w_down.reshape((-1,) + moe_w_down.shape[2:]).astype(_BF16)
    assert mp + ms < (1 << IDX_SHIFT)
    ln_b3 = ln_b.reshape(depth * 3, 1, d)

    def to_time_major(a):
        return a.reshape(bs, ts, -1).transpose(1, 0, 2).reshape(ms, -1)

    def to_batch_major(a):
        return a.reshape(ts, bs, -1).transpose(1, 0, 2).reshape(ms, -1)

    def pad_t(a):
        a = a.reshape(bs, ts, -1)
        return jnp.pad(a, ((0, 0), (0, t8 - ts), (0, 0)))

    lru_h_p, lru_c_p, lru_h_s, lru_c_s = [], [], [], []
    gdn_s_p, gdn_c_p, gdn_c_s = [], [], []
    gdn_s_s = jnp.zeros(state_gdn_s.shape, _F32)

    for i in range(depth):
        j = i // 2
        if i % 2 == 0:
            gu = _mm(xb, lru_w_in, g0=j, tn=1024, out_dtype=_F32)[0]
            lam = lru_lambda[j].reshape(1, r)
            cb = lru_conv_b[j].reshape(1, r)
            gb = lru_gate_b[j].reshape(LRU_BLOCKS, 1, -1)
            y, h_p, c_p = _lru_seq(gu, jnp.zeros((bp, r), _F32), jnp.zeros((bp, pad_p, r), _F32),
                                   lru_conv_w[j], cb, lru_gate_w[j], gb, lam,
                                   n_tiles=nchunk, nseg=bp, seg=seg, stride=1, pad=pad_p)
            gu_s = to_time_major(gu[mp:])
            c0_s = state_lru_conv[j].transpose(1, 0, 2).reshape(1, pad_s, r)
            y_s, h_s, c_s = _lru_seq(gu_s, state_lru_h[j], c0_s, lru_conv_w[j], cb, lru_gate_w[j], gb, lam,
                                     n_tiles=1, nseg=1, seg=ms, stride=bs, pad=pad_s)
            y_s = to_batch_major(y_s)
            lru_h_p.append(h_p)
            lru_c_p.append(c_p[:, pad_p - (CONV_W - 1):, :])
            lru_h_s.append(h_s)
            lru_c_s.append(c_s.reshape(CONV_W - 1, bs, r).transpose(1, 0, 2))
            w_out, wi = lru_w_out, j
        else:
            proj = _mm(xb, gdn_w_in, g0=j, n=cd + vw, tn=1024, out_dtype=_F32)[0]
            w_tail = gdn_w_in[j, :, cd + vw:]
            w_ba = jnp.zeros((1, d, 2 * LANES), _F32)
            w_ba = w_ba.at[0, :, :GDN_HEADS].set(w_tail[:, :GDN_HEADS])
            w_ba = w_ba.at[0, :, LANES:LANES + GDN_HEADS].set(w_tail[:, GDN_HEADS:])
            ba = _mm(xb, w_ba, tn=2 * LANES, out_dtype=_F32)[0]
            alog = jnp.zeros((1, LANES), _F32).at[0, :GDN_HEADS].set(gdn_a_log[j])
            dtb = jnp.zeros((1, LANES), _F32).at[0, :GDN_HEADS].set(gdn_dt_bias[j])
            ng = gdn_norm_g[j].reshape(1, GDN_DV)
            y, s_p, c_p = _gdn_prompt(proj, ba, gdn_conv_w[j], alog, dtb, ng, n_tiles=nchunk, nseg=bp, seg=seg)
            c0_s = jnp.pad(state_gdn_conv[j], ((0, 0), (pad_p - (CONV_W - 1), 0), (0, 0)))
            y_s, gdn_s_s, c_s = _gdn_sample(pad_t(proj[mp:]), pad_t(ba[mp:]), state_gdn_s, j, gdn_s_s, c0_s,
                                            gdn_conv_w[j], alog, dtb, ng, steps=ts)
            y_s = y_s[:, :ts, :].reshape(ms, vw)
            gdn_s_p.append(s_p)
            gdn_c_p.append(c_p[:, pad_p - (CONV_W - 1):, :])
            gdn_c_s.append(c_s[:, pad_p - (CONV_W - 1):, :])
            w_out, wi = gdn_w_out, j
        x, xb = _mm_ln(y, y_s, w_out, wi, x, ln_g3, ln_b3, 3 * i, alpha=alpha)

        q = _mm(xb, xa_wq, g0=i, tn=1024, out_dtype=_BF16)[0]
        o = _attn_prompt(q, mem_k_b[i], mem_v_b[i], n_tiles=nchunk, nseg=bp, seg=seg)
        o_s = _attn_sample(pad_t(q[mp:].astype(_F32)), cache_mem_k, cache_mem_v, i)
        x, xb = _mm_ln(o, o_s[:, :ts, :].reshape(ms, d), xa_wo, i, x, ln_g3, ln_b3, 3 * i + 1, alpha=alpha)

        if i % 2 == 0:
            hdn = _gate_up(xb, ffn_w_gu, g0=j, ng=1, f=d_ff, tf=512)[0]
            x, xb = _mm_ln(hdn, None, ffn_w_down, j, x, ln_g3, ln_b3, 3 * i + 2, alpha=alpha)
        else:
            router_pad = jnp.zeros((d, LANES), _F32).at[:, :N_EXPERTS].set(moe_router[j])
            plan = _route_plan(_router(x, router_pad), ROW_TILE)
            y2 = _moe_routed(x, plan, moe_w_gu3, moe_w_down3, j * N_EXPERTS, tm=ROW_TILE)
            if i < depth - 1:
                x, xb = _combine_ln(y2, x, ln_g3, ln_b3, 3 * i + 2, alpha=alpha)
            else:
                y_prompt, y_sample = _combine_ln(y2, x, ln_g3, ln_b3, 3 * i + 2, alpha=alpha,
                                                 prompt_shape=(bp, nchunk, seg))

    y_prompt = y_prompt.reshape(bp, seq, d)
    y_sample = y_sample.reshape(bs, ts, d)
    hd = d // XA_HEADS
    return (y_prompt, y_sample,
            jnp.stack(lru_h_p), jnp.stack(lru_c_p), jnp.stack(gdn_s_p), jnp.stack(gdn_c_p),
            mem_k_p.reshape(depth, bp, mem, XA_HEADS, hd), mem_v_p.reshape(depth, bp, mem, XA_HEADS, hd),
            jnp.stack(lru_h_s), jnp.stack(lru_c_s), gdn_s_s, jnp.stack(gdn_c_s))
```

```python
import functools
import math

import jax
import jax.numpy as jnp
from jax import lax
from jax.experimental import pallas as pl
from jax.experimental.pallas import tpu as pltpu

_F32 = jnp.float32
_BF16 = jnp.bfloat16
_HI = lax.Precision.HIGHEST

CONV_W = 4
LRU_BLOCKS = 4
LRU_C = 8.0
GDN_HEADS = 8
GDN_DK = 128
GDN_DV = 128
GDN_CHUNK = 64
GDN_SEQS_PER_STEP = 4
GDN_STACK_ROWS = 128
XA_HEADS = 4
N_EXPERTS = 8
IDX_SHIFT = 15
LN_EPS = 1e-5
RMS_EPS = 1e-6
LANES = 128
SUBLANES = 8
ROW_TILE = 512
VMEM_LIMIT_MB = 56


def _big_row_tile(m):
    return 3 * ROW_TILE if m % (3 * ROW_TILE) == 0 else min(ROW_TILE, m)


def _cparams(ndim, vmem_mb=VMEM_LIMIT_MB):
    return pltpu.CompilerParams(dimension_semantics=("arbitrary",) * ndim,
                                vmem_limit_bytes=vmem_mb * 1024 * 1024)


def _dot(a, b, precision=None):
    return jnp.dot(a, b, preferred_element_type=_F32, precision=precision)


def _bdot(a, b):
    return _dot(a.astype(_BF16), b.astype(_BF16))


def _dot_nt(a, b):
    return lax.dot_general(a, b, (((1,), (1,)), ((), ())), preferred_element_type=_F32)


def _dot_tn(a, b):
    return lax.dot_general(a, b, (((0,), (0,)), ((), ())), preferred_element_type=_F32)


def _layer_norm(v, g, b):
    mu = jnp.mean(v, axis=-1, keepdims=True)
    d = v - mu
    var = jnp.mean(d * d, axis=-1, keepdims=True)
    return d * lax.rsqrt(var + LN_EPS) * g + b


def _mm_body(x_ref, w_ref, o_ref, wb_ref):
    @pl.when(pl.program_id(2) == 0)
    def _():
        wb_ref[...] = w_ref[0].astype(_BF16)

    o_ref[0] = _dot(x_ref[...].astype(_BF16), wb_ref[...]).astype(o_ref.dtype)


def _mm(x, w, *, g0=0, ng=1, n0=0, n=None, tn, out_dtype):
    m, k = x.shape
    n = w.shape[2] if n is None else n
    tm = _big_row_tile(m)
    nb0 = n0 // tn
    return pl.pallas_call(
        _mm_body,
        grid=(ng, n // tn, m // tm),
        in_specs=[pl.BlockSpec((tm, k), lambda g, j, i: (i, 0)),
                  pl.BlockSpec((1, k, tn), lambda g, j, i: (g + g0, 0, j + nb0))],
        out_specs=pl.BlockSpec((1, tm, tn), lambda g, j, i: (g, i, j)),
        out_shape=jax.ShapeDtypeStruct((ng, m, n), out_dtype),
        scratch_shapes=[pltpu.VMEM((k, tn), _BF16)],
        compiler_params=_cparams(3),
        name="mm",
    )(x, w)


def _mm_ln_body(*refs, alpha, n1, post):
    if post is None:
        y_ref, y2_ref, w_ref, x_ref, g_ref, b_ref, o_ref, ob_ref, wb_ref = refs
    elif post == "proj":
        y_ref, y2_ref, w_ref, x_ref, g_ref, b_ref, w2_ref, o_ref, ob_ref, p_ref, wb_ref, wb2_ref = refs
    else:
        y_ref, y2_ref, w_ref, x_ref, g_ref, b_ref, r_ref, o_ref, ob_ref, p_ref, wb_ref = refs
    i = pl.program_id(0)

    @pl.when(i == 0)
    def _():
        wb_ref[...] = w_ref[0].astype(_BF16)
        if post == "proj":
            wb2_ref[...] = w2_ref[0].astype(_BF16)

    def emit(src_ref):
        mix = _dot(src_ref[...].astype(_BF16), wb_ref[...])
        res = _layer_norm(alpha * x_ref[...] + mix, g_ref[0], b_ref[0])
        o_ref[...] = res
        ob_ref[...] = res.astype(_BF16)
        if post == "proj":
            p_ref[...] = _dot(res.astype(_BF16), wb2_ref[...]).astype(p_ref.dtype)
        elif post == "route":
            p_ref[...] = _route_top2(res, r_ref[...])

    if n1 is None:
        emit(y_ref)
    else:
        pl.when(i < n1)(lambda: emit(y_ref))
        pl.when(i >= n1)(lambda: emit(y2_ref))


def _mm_ln(y, y2, w, gi, x, ln_g, ln_b, li, *, alpha, proj=None, route=None):
    m, d = x.shape
    k = y.shape[1]
    tm = min(ROW_TILE, m)
    n1 = y.shape[0] // tm
    if y2 is None:
        y2 = y
    assert n1 * tm == y.shape[0] and (n1 == m // tm or n1 * tm + y2.shape[0] == m)
    assert proj is None or route is None
    row = lambda i: (i, 0)
    in_specs = [pl.BlockSpec((tm, k), lambda i: (jnp.minimum(i, n1 - 1), 0)),
                pl.BlockSpec((tm, k), lambda i: (jnp.maximum(i - n1, 0), 0)),
                pl.BlockSpec((1, k, d), lambda i: (gi, 0, 0), pipeline_mode=pl.Buffered(1)),
                pl.BlockSpec((tm, d), row),
                pl.BlockSpec((1, 1, d), lambda i: (li, 0, 0)),
                pl.BlockSpec((1, 1, d), lambda i: (li, 0, 0))]
    args = [y, y2, w, x, ln_g, ln_b]
    out_specs = [pl.BlockSpec((tm, d), row), pl.BlockSpec((tm, d), row)]
    out_shape = [jax.ShapeDtypeStruct((m, d), _F32), jax.ShapeDtypeStruct((m, d), _BF16)]
    scratch = [pltpu.VMEM((k, d), _BF16)]
    post = None
    if proj is not None:
        post = "proj"
        w2, g2 = proj
        n2 = w2.shape[2]
        in_specs.append(pl.BlockSpec((1, d, n2), lambda i: (g2, 0, 0), pipeline_mode=pl.Buffered(1)))
        args.append(w2)
        out_specs.append(pl.BlockSpec((tm, n2), row))
        out_shape.append(jax.ShapeDtypeStruct((m, n2), _BF16))
        scratch.append(pltpu.VMEM((d, n2), _BF16))
    elif route is not None:
        post = "route"
        in_specs.append(pl.BlockSpec((d, LANES), lambda i: (0, 0)))
        args.append(route)
        out_specs.append(pl.BlockSpec((tm, LANES), row))
        out_shape.append(jax.ShapeDtypeStruct((m, LANES), _F32))
    return pl.pallas_call(
        functools.partial(_mm_ln_body, alpha=alpha, n1=None if n1 == m // tm else n1, post=post),
        grid=(m // tm,),
        in_specs=in_specs,
        out_specs=out_specs,
        out_shape=out_shape,
        scratch_shapes=scratch,
        compiler_params=_cparams(1),
        name="mm_ln",
    )(*args)


def _gu_body(x_ref, wg_ref, wu_ref, o_ref, wgb_ref, wub_ref):
    @pl.when(pl.program_id(2) == 0)
    def _():
        wgb_ref[...] = wg_ref[0].astype(_BF16)
        wub_ref[...] = wu_ref[0].astype(_BF16)

    xb = x_ref[...].astype(_BF16)
    g = _dot(xb, wgb_ref[...])
    u = _dot(xb, wub_ref[...])
    o_ref[0] = (jax.nn.silu(g) * u).astype(o_ref.dtype)


def _gate_up(x, w_gu, *, g0, ng, f, tf):
    m, k = x.shape
    tm = _big_row_tile(m)
    nf = f // tf
    return pl.pallas_call(
        _gu_body,
        grid=(ng, nf, m // tm),
        in_specs=[pl.BlockSpec((tm, k), lambda g, j, i: (i, 0)),
                  pl.BlockSpec((1, k, tf), lambda g, j, i: (g + g0, 0, j)),
                  pl.BlockSpec((1, k, tf), lambda g, j, i: (g + g0, 0, j + nf))],
        out_specs=pl.BlockSpec((1, tm, tf), lambda g, j, i: (g, i, j)),
        out_shape=jax.ShapeDtypeStruct((ng, m, f), _BF16),
        scratch_shapes=[pltpu.VMEM((k, tf), _BF16), pltpu.VMEM((k, tf), _BF16)],
        compiler_params=_cparams(3),
        name="gate_up",
    )(x, w_gu, w_gu)


def _route_top2(x, router_pad):
    logits = _dot(x, router_pad, precision=_HI)
    lane = lax.broadcasted_iota(jnp.int32, logits.shape, 1).astype(_F32)
    l1 = jnp.where(lane < N_EXPERTS, logits, -jnp.inf)
    m1 = jnp.max(l1, axis=-1, keepdims=True)
    i1 = jnp.min(jnp.where(l1 == m1, lane, float(LANES)), axis=-1, keepdims=True)
    l2 = jnp.where(lane == i1, -jnp.inf, l1)
    m2 = jnp.max(l2, axis=-1, keepdims=True)
    i2 = jnp.min(jnp.where(l2 == m2, lane, float(LANES)), axis=-1, keepdims=True)
    e2 = jnp.exp(m2 - m1)
    den = 1.0 + e2
    return (jnp.where(lane == 0.0, i1, 0.0) + jnp.where(lane == 1.0, i2, 0.0)
            + jnp.where(lane == 2.0, 1.0 / den, 0.0) + jnp.where(lane == 3.0, e2 / den, 0.0))


def _route_plan(rt, tm):
    m = rt.shape[0]
    ef = rt[:, :2].astype(jnp.int32).reshape(-1)
    onehot = (ef[:, None] == jnp.arange(N_EXPERTS, dtype=jnp.int32)[None, :]).astype(jnp.int32)
    csum = jnp.cumsum(onehot, axis=0)
    cnt = csum[-1]
    rank = jnp.sum((csum - onehot) * onehot, axis=1)
    cnt_p = ((cnt + tm - 1) // tm) * tm
    end = jnp.cumsum(cnt_p)
    start = end - cnt_p
    dest = start[ef] + rank
    n_tiles = (2 * m) // tm + N_EXPERTS
    rows = n_tiles * tm
    pair = jnp.zeros((rows,), jnp.int32).at[dest].set(jnp.arange(2 * m, dtype=jnp.int32))
    tile_start = jnp.arange(n_tiles, dtype=jnp.int32) * tm
    texp = jnp.sum((tile_start[:, None] >= end[None, :]).astype(jnp.int32), axis=1)
    used = tile_start < end[-1]
    last = jnp.maximum(end[-1] // tm - 1, 0)
    texp = jnp.where(used, texp, texp[last])
    nval = jnp.where(used, jnp.clip(start[texp] + cnt[texp] - tile_start, 0, tm), 0)
    tile = jnp.repeat(jnp.arange(n_tiles, dtype=jnp.int32), tm)
    in_tile = jnp.arange(rows, dtype=jnp.int32) - tile * tm
    live = in_tile < jnp.repeat(nval, tm)
    tok, slot = pair // 2, pair % 2
    spare = 2 * m + (tile % 2) * tm + in_tile
    idx = jnp.where(live, tok, 0) | (jnp.where(live, slot * m + tok, spare) << IDX_SHIFT)
    n_used = (end[-1] // tm).reshape(1)
    return texp, n_used, idx


def _moe_routed_body(texp_ref, nused_ref, idx_ref, x_hbm, wg_ref, wu_ref, wd_ref, out_hbm,
                     xs_ref, ys_ref, gsem, ssem):
    i = pl.program_id(0)
    ng = xs_ref.shape[1]
    tm = ng * SUBLANES
    d = xs_ref.shape[3]
    n_used = nused_ref[0]
    slot = i % 2

    def gather(tile, s):
        def body(g, c):
            for u in range(SUBLANES):
                tok = idx_ref[tile * tm + g * SUBLANES + u] & ((1 << IDX_SHIFT) - 1)
                pltpu.make_async_copy(x_hbm.at[tok >> 3, tok & 7], xs_ref.at[s, g, u], gsem.at[s]).start()
            return c
        lax.fori_loop(0, ng, body, 0)

    def scatter(tile, s):
        def body(g, c):
            for u in range(SUBLANES):
                dst = idx_ref[tile * tm + g * SUBLANES + u] >> IDX_SHIFT
                pltpu.make_async_copy(ys_ref.at[s, g, u], out_hbm.at[dst >> 3, dst & 7], ssem.at[s]).start()
            return c
        lax.fori_loop(0, ng, body, 0)

    def wait_gather(s):
        pltpu.make_async_copy(x_hbm.at[pl.ds(0, ng)], xs_ref.at[s], gsem.at[s]).wait()

    def wait_scatter(s):
        pltpu.make_async_copy(ys_ref.at[s], out_hbm.at[pl.ds(0, ng)], ssem.at[s]).wait()

    @pl.when((i == 0) & (n_used > 0))
    def _():
        gather(0, 0)

    @pl.when(i + 1 < n_used)
    def _():
        gather(i + 1, 1 - slot)

    @pl.when(i < n_used)
    def _():
        wait_gather(slot)

        @pl.when(i >= 2)
        def _():
            wait_scatter(slot)

        xb = xs_ref[slot].reshape(tm, d).astype(_BF16)
        h = (jax.nn.silu(_dot(xb, wg_ref[0])) * _dot(xb, wu_ref[0])).astype(_BF16)
        ys_ref[slot] = _dot(h, wd_ref[0]).reshape(ng, SUBLANES, d)
        scatter(i, slot)

    @pl.when(i == pl.num_programs(0) - 1)
    def _():
        @pl.when(n_used >= 1)
        def _():
            wait_scatter((n_used - 1) % 2)

        @pl.when(n_used >= 2)
        def _():
            wait_scatter(n_used % 2)

        ys_ref[0] = jnp.zeros(ys_ref.shape[1:], _F32)
        spare0 = out_hbm.shape[0] - 2 * ng
        fills = [pltpu.make_async_copy(ys_ref.at[0], out_hbm.at[pl.ds(spare0 + k * ng, ng)], ssem.at[0])
                 for k in range(2)]
        for cp in fills:
            cp.start()
        for cp in fills:
            cp.wait()


def _moe_routed(x, plan, w_gu_bf16, w_down_bf16, e0, *, tm):
    texp, n_used, idx = plan
    m, d = x.shape
    f = w_down_bf16.shape[1]
    n_tiles = texp.shape[0]
    return pl.pallas_call(
        _moe_routed_body,
        grid_spec=pltpu.PrefetchScalarGridSpec(
            num_scalar_prefetch=3,
            grid=(n_tiles,),
            in_specs=[pl.BlockSpec(memory_space=pl.ANY),
                      pl.BlockSpec((1, d, f), lambda i, te, nu, ix: (te[i] + e0, 0, 0)),
                      pl.BlockSpec((1, d, f), lambda i, te, nu, ix: (te[i] + e0, 0, 1)),
                      pl.BlockSpec((1, f, d), lambda i, te, nu, ix: (te[i] + e0, 0, 0))],
            out_specs=pl.BlockSpec(memory_space=pl.ANY),
            scratch_shapes=[pltpu.VMEM((2, tm // SUBLANES, SUBLANES, d), _F32),
                            pltpu.VMEM((2, tm // SUBLANES, SUBLANES, d), _F32),
                            pltpu.SemaphoreType.DMA((2,)), pltpu.SemaphoreType.DMA((2,))]),
        out_shape=jax.ShapeDtypeStruct(((2 * m + 2 * tm) // SUBLANES, SUBLANES, d), _F32),
        compiler_params=_cparams(1),
        name="moe_routed",
    )(texp, n_used, idx, x.reshape(m // SUBLANES, SUBLANES, d), w_gu_bf16, w_gu_bf16,
      w_down_bf16).reshape(2 * m + 2 * tm, d)


def _combine_ln_body(a_ref, b_ref, rt_ref, x_ref, g_ref, bb_ref, *o_refs, alpha, n_prompt):
    ff = rt_ref[:, 2:3] * a_ref[...] + rt_ref[:, 3:4] * b_ref[...]
    res = _layer_norm(alpha * x_ref[...] + ff, g_ref[0], bb_ref[0])
    if n_prompt is None:
        o_refs[0][...] = res
        o_refs[1][...] = res.astype(_BF16)
    else:
        i = pl.program_id(0)
        op_ref, os_ref = o_refs

        @pl.when(i < n_prompt)
        def _():
            op_ref[...] = res.reshape(op_ref.shape)

        @pl.when(i >= n_prompt)
        def _():
            os_ref[...] = res


def _combine_ln(y2, rt, x, ln_g, ln_b, li, *, alpha, prompt_shape=None):
    m, d = x.shape
    tm = min(ROW_TILE, m)
    nt = m // tm
    if prompt_shape is None:
        n_prompt = None
        out_specs = [pl.BlockSpec((tm, d), lambda i: (i, 0)), pl.BlockSpec((tm, d), lambda i: (i, 0))]
        out_shape = [jax.ShapeDtypeStruct((m, d), _F32), jax.ShapeDtypeStruct((m, d), _BF16)]
    else:
        bp, n_prompt, seg = prompt_shape
        assert bp * seg == tm and n_prompt == nt - 1
        out_specs = [pl.BlockSpec((bp, 1, seg, d), lambda i: (0, jnp.minimum(i, n_prompt - 1), 0, 0)),
                     pl.BlockSpec((tm, d), lambda i: (0, 0))]
        out_shape = [jax.ShapeDtypeStruct((bp, n_prompt, seg, d), _F32), jax.ShapeDtypeStruct((tm, d), _F32)]
    return pl.pallas_call(
        functools.partial(_combine_ln_body, alpha=alpha, n_prompt=n_prompt),
        grid=(nt,),
        in_specs=[pl.BlockSpec((tm, d), lambda i: (i, 0)),
                  pl.BlockSpec((tm, d), lambda i: (i + nt, 0)),
                  pl.BlockSpec((tm, LANES), lambda i: (i, 0)),
                  pl.BlockSpec((tm, d), lambda i: (i, 0)),
                  pl.BlockSpec((1, 1, d), lambda i: (li, 0, 0)),
                  pl.BlockSpec((1, 1, d), lambda i: (li, 0, 0))],
        out_specs=out_specs,
        out_shape=out_shape,
        compiler_params=_cparams(1),
        name="combine_ln",
    )(y2, y2, rt, x, ln_g, ln_b)


def _pack_body(xp_ref, xs_ref, o_ref, ob_ref, *, n_prompt):
    i = pl.program_id(0)

    @pl.when(i < n_prompt)
    def _():
        o_ref[...] = xp_ref[...].reshape(o_ref.shape)
        ob_ref[...] = xp_ref[...].reshape(o_ref.shape).astype(_BF16)

    @pl.when(i >= n_prompt)
    def _():
        o_ref[...] = xs_ref[...]
        ob_ref[...] = xs_ref[...].astype(_BF16)


def _pack_tokens(x_prompt4, x_sample2):
    bp, n_prompt, seg, d = x_prompt4.shape
    tm = bp * seg
    assert x_sample2.shape == (tm, d)
    return pl.pallas_call(
        functools.partial(_pack_body, n_prompt=n_prompt),
        grid=(n_prompt + 1,),
        in_specs=[pl.BlockSpec((bp, 1, seg, d), lambda i: (0, jnp.minimum(i, n_prompt - 1), 0, 0)),
                  pl.BlockSpec((tm, d), lambda i: (0, 0))],
        out_specs=[pl.BlockSpec((tm, d), lambda i: (i, 0)), pl.BlockSpec((tm, d), lambda i: (i, 0))],
        out_shape=[jax.ShapeDtypeStruct(((n_prompt + 1) * tm, d), _F32),
                   jax.ShapeDtypeStruct(((n_prompt + 1) * tm, d), _BF16)],
        compiler_params=_cparams(1),
        name="pack_tokens",
    )(x_prompt4, x_sample2)


def _seg_scan(a, b, tstep, n_steps, stride):
    sh = stride
    for s in range(n_steps):
        keep = tstep >= (1 << s)
        a_sh = pltpu.roll(a, sh, 0)
        b_sh = pltpu.roll(b, sh, 0)
        b = jnp.where(keep, a * b_sh + b, b)
        a = jnp.where(keep, a * a_sh, a)
        sh *= 2
    return a, b


def _lru_body(gate_ref, rec_ref, h0_ref, c0_ref, cw_ref, cb_ref, gw_ref, gb_ref, lam_ref,
              y_ref, hout_ref, cout_ref, xp_ref, hc_ref, cv_ref, gwb_ref, *, nseg, seg, stride, pad):
    rows = nseg * seg
    bw = gw_ref.shape[1]

    @pl.when(pl.program_id(0) == 0)
    def _():
        xp_ref[:, 0:pad, :] = c0_ref[...]
        hc_ref[...] = h0_ref[...]
        gwb_ref[...] = gw_ref[...].astype(_BF16)

    for s in range(nseg):
        xp_ref[s, pad:pad + seg, :] = rec_ref[s * seg:(s + 1) * seg, :]
        acc = xp_ref[s, pad - 3 * stride:pad - 3 * stride + seg, :] * cw_ref[0:1, :]
        for j in range(1, CONV_W):
            off = pad - (CONV_W - 1 - j) * stride
            acc = acc + xp_ref[s, off:off + seg, :] * cw_ref[j:j + 1, :]
        cv_ref[s * seg:(s + 1) * seg, :] = acc + cb_ref[...]
        tail = xp_ref[s, seg:seg + pad, :]
        xp_ref[s, 0:pad, :] = tail
        cout_ref[s] = tail

    row = lax.broadcasted_iota(jnp.int32, (rows, bw), 0)
    if stride == 1:
        tstep = row & (seg - 1)
    else:
        tstep = row >> int(math.log2(stride))
    n_steps = int(math.log2(seg // stride))

    for n in range(LRU_BLOCKS):
        cs = slice(n * bw, (n + 1) * bw)
        cv = cv_ref[:, cs]
        gates = _dot(cv.astype(_BF16), gwb_ref[n]) + gb_ref[n]
        r = jax.nn.sigmoid(gates[:, :bw])
        ig = jax.nn.sigmoid(gates[:, bw:])
        log_a = -LRU_C * r * jax.nn.softplus(-lam_ref[:, cs])
        a = jnp.exp(log_a)
        s = -jnp.tanh(log_a) * (a * a + 1.0)
        mult = jnp.where(s > 0.0, s * lax.rsqrt(s), 0.0)
        a_cum, b_cum = _seg_scan(a, mult * ig * cv, tstep, n_steps, stride)
        hc = hc_ref[:, cs]
        if stride == 1:
            hprev = jnp.concatenate([jnp.broadcast_to(hc[s:s + 1, :], (seg, bw)) for s in range(nseg)], axis=0)
        else:
            hprev = jnp.concatenate([hc] * (seg // stride), axis=0)
        h = a_cum * hprev + b_cum
        y_ref[:, cs] = (h * jax.nn.gelu(gate_ref[:, cs])).astype(y_ref.dtype)
        if stride == 1:
            hc_ref[:, cs] = jnp.concatenate([h[(s + 1) * seg - 1:(s + 1) * seg, :] for s in range(nseg)], axis=0)
        else:
            hc_ref[:, cs] = h[rows - stride:, :]

    hout_ref[...] = hc_ref[...]


def _lru_seq(gu, h0, c0, conv_w, conv_b, gate_w, gate_b, lam, *, n_tiles, nseg, seg, stride, pad):
    rows = nseg * seg
    r = gu.shape[1] // 2
    bw = r // LRU_BLOCKS
    const2 = lambda i: (0, 0)
    const3 = lambda i: (0, 0, 0)
    return pl.pallas_call(
        functools.partial(_lru_body, nseg=nseg, seg=seg, stride=stride, pad=pad),
        grid=(n_tiles,),
        in_specs=[pl.BlockSpec((rows, r), lambda i: (i, 0)),
                  pl.BlockSpec((rows, r), lambda i: (i, 1)),
                  pl.BlockSpec((nseg * stride, r), const2),
                  pl.BlockSpec((nseg, pad, r), const3),
                  pl.BlockSpec((CONV_W, r), const2),
                  pl.BlockSpec((1, r), const2),
                  pl.BlockSpec((LRU_BLOCKS, bw, 2 * bw), const3),
                  pl.BlockSpec((LRU_BLOCKS, 1, 2 * bw), const3),
                  pl.BlockSpec((1, r), const2)],
        out_specs=[pl.BlockSpec((rows, r), lambda i: (i, 0)),
                   pl.BlockSpec((nseg * stride, r), const2),
                   pl.BlockSpec((nseg, pad, r), const3)],
        out_shape=[jax.ShapeDtypeStruct((n_tiles * rows, r), _BF16),
                   jax.ShapeDtypeStruct((nseg * stride, r), _F32),
                   jax.ShapeDtypeStruct((nseg, pad, r), _F32)],
        scratch_shapes=[pltpu.VMEM((nseg, pad + seg, r), _F32),
                        pltpu.VMEM((nseg * stride, r), _F32),
                        pltpu.VMEM((rows, r), _F32),
                        pltpu.VMEM((LRU_BLOCKS, bw, 2 * bw), _BF16)],
        compiler_params=_cparams(1),
        name="lru_seq",
    )(gu, gu, h0, c0, conv_w, conv_b, gate_w, gate_b, lam)


def _l2norm(t):
    return t * lax.rsqrt(jnp.sum(t * t, axis=-1, keepdims=True) + RMS_EPS)


def _gdn_chunks(seqs, norm_g, *, c, hg):
    dk = GDN_DK
    qk_w = GDN_HEADS * dk
    r4 = hg * c
    lg_c = int(math.log2(c))
    row = lax.broadcasted_iota(jnp.int32, (r4, r4), 0)
    col = lax.broadcasted_iota(jnp.int32, (r4, r4), 1)
    blk = {1 << s: (row >> s) == (col >> s) for s in range(3, lg_c + 1)}
    incl = blk[c] & (row >= col)
    strict = blk[c] & (row > col)
    units = [(si, list(range(g0, g0 + hg))) for si in range(len(seqs)) for g0 in range(0, GDN_HEADS, hg)]
    blocks = [slice(hh * c, (hh + 1) * c) for hh in range(hg)]

    def stack(fn, hs, axis=0):
        return jnp.concatenate([fn(h) for h in hs], axis=axis)

    tr = lax.broadcasted_iota(jnp.int32, (c, c), 0)
    tc = lax.broadcasted_iota(jnp.int32, (c, c), 1)
    tri = jnp.where(tr >= tc, 1.0, 0.0)
    gc = [_dot(tri, sq[3], precision=_HI) for sq in seqs]
    gct = [jnp.concatenate([x, jnp.zeros((LANES - c, LANES), _F32)], axis=0).T for x in gc]

    q4 = [stack(lambda h: _l2norm(seqs[si][0](h * dk)) * (dk ** -0.5), hs) for si, hs in units]
    k4 = [stack(lambda h: _l2norm(seqs[si][0](qk_w + h * dk)), hs) for si, hs in units]
    v4 = [stack(lambda h: seqs[si][0](2 * qk_w + h * dk), hs) for si, hs in units]
    beta4 = [stack(lambda h: jnp.broadcast_to(seqs[si][2][:, h:h + 1], (c, dk)), hs) for si, hs in units]
    gc4 = [stack(lambda h: jnp.broadcast_to(gc[si][:, h:h + 1], (c, dk)), hs) for si, hs in units]
    gcol = [stack(lambda h: jnp.broadcast_to(gc[si][:, h:h + 1], (c, r4)), hs) for si, hs in units]
    grow = [stack(lambda h: gct[si][h:h + 1, 0:c], hs, axis=1) for si, hs in units]
    decay = [jnp.exp(jnp.where(incl, gcl - grw, -jnp.inf)) for gcl, grw in zip(gcol, grow)]
    kb4 = [k * bt for k, bt in zip(k4, beta4)]
    aq = [_dot_nt(jnp.concatenate([kb, q], axis=0).astype(_BF16), k.astype(_BF16)) for kb, q, k in zip(kb4, q4, k4)]
    a4 = [jnp.where(strict, x[:r4] * dc, 0.0) for x, dc in zip(aq, decay)]
    attn4 = [x[r4:] * dc for x, dc in zip(aq, decay)]
    c0 = min(c, 16)
    np_ = [-jnp.where(blk[c0], x, 0.0) for x in a4]
    e = list(np_)
    for _ in range(int(math.log2(c0)) - 1):
        np_ = [_bdot(x, x) for x in np_]
        e = [ei + ni + _bdot(ni, ei) for ei, ni in zip(e, np_)]
    size = c0
    while size < c:
        lo = [jnp.where(blk[2 * size] & ~blk[size], x, 0.0) for x in a4]
        m = [li + _bdot(li, ei) for li, ei in zip(lo, e)]
        e = [ei - (mi + _bdot(ei, mi)) for ei, mi in zip(e, m)]
        size *= 2
    egc4 = [jnp.exp(x) for x in gc4]
    rhs = [jnp.concatenate([v * bt, kb * eg], axis=1) for v, bt, kb, eg in zip(v4, beta4, kb4, egc4)]
    uw = [r + _bdot(ei, r) for r, ei in zip(rhs, e)]
    qg4 = [q * eg for q, eg in zip(q4, egc4)]
    ws_qs = [[_dot(jnp.concatenate([uw[u][rs, dk:], qg4[u][rs]], axis=0).astype(_BF16),
                   seqs[si][4][h].astype(_BF16)) for rs, h in zip(blocks, hs)]
             for u, (si, hs) in enumerate(units)]
    vn = [[uw[u][rs, :dk] - ws_qs[u][hh][:c] for hh, rs in enumerate(blocks)] for u in range(len(units))]
    o4 = [jnp.concatenate([x[c:] for x in ws_qs[u]], axis=0)
          + _bdot(attn4[u], jnp.concatenate(vn[u], axis=0)) for u in range(len(units))]
    outs = [[None] * GDN_HEADS for _ in seqs]
    new_state = [[None] * GDN_HEADS for _ in seqs]
    for u, (si, hs) in enumerate(units):
        z, state = seqs[si][1], seqs[si][4]
        for hh, (rs, h) in enumerate(zip(blocks, hs)):
            glast = gc[si][c - 1:c, h:h + 1]
            kd = k4[u][rs] * jnp.exp(glast - gc4[u][rs])
            new_state[si][h] = state[h] * jnp.exp(glast) + _dot_tn(kd.astype(_BF16), vn[u][hh].astype(_BF16))
            oh = o4[u][rs]
            oh = oh * lax.rsqrt(jnp.mean(oh * oh, axis=-1, keepdims=True) + RMS_EPS) * norm_g
            outs[si][h] = oh * jax.nn.silu(z[:, h * dk:(h + 1) * dk])
    return outs, new_state


def _gdn_gates(ba, alog, dtb):
    beta = jax.nn.sigmoid(ba[:, :LANES])
    g = -jnp.exp(alog) * jax.nn.softplus(ba[:, LANES:] + dtb)
    return beta, g


def _gdn_conv(xp_ref, s, x, cw_ref, *, pad, seg):
    xp_ref[s, pad:pad + seg, :] = x

    def block(col):
        cs = slice(col, col + GDN_DK)
        acc = xp_ref[s, pl.ds(pad - 3, seg), cs] * cw_ref[0:1, cs]
        for j in range(1, CONV_W):
            acc = acc + xp_ref[s, pl.ds(pad - (CONV_W - 1 - j), seg), cs] * cw_ref[j:j + 1, cs]
        return jax.nn.silu(acc)

    return block


def _gdn_prompt_body(qkv_ref, z_ref, ba_ref, cw_ref, alog_ref, dtb_ref, ng_ref,
                     o_ref, sout_ref, cout_ref, xp_ref, s_ref, *, nseg, seg, pad):
    @pl.when(pl.program_id(0) == 0)
    def _():
        xp_ref[...] = jnp.zeros(xp_ref.shape, _F32)
        s_ref[...] = jnp.zeros(s_ref.shape, _F32)

    def per_group(i, carry):
        bs = [i * GDN_SEQS_PER_STEP + k for k in range(GDN_SEQS_PER_STEP)]
        r0 = [pl.multiple_of(b * seg, seg) for b in bs]
        seqs = []
        for b, r in zip(bs, r0):
            qkv = _gdn_conv(xp_ref, b, qkv_ref[pl.ds(r, seg), :], cw_ref, pad=pad, seg=seg)
            beta, g = _gdn_gates(ba_ref[pl.ds(r, seg), :], alog_ref[...], dtb_ref[...])
            seqs.append((qkv, z_ref[pl.ds(r, seg), :], beta, g, [s_ref[b, h] for h in range(GDN_HEADS)]))
        outs, new_state = _gdn_chunks(seqs, ng_ref[...], c=seg, hg=GDN_STACK_ROWS // seg)
        for k, (b, r) in enumerate(zip(bs, r0)):
            tail = xp_ref[b, pl.ds(seg, pad), :]
            xp_ref[b, 0:pad, :] = tail
            cout_ref[b] = tail
            for h in range(GDN_HEADS):
                s_ref[b, h] = new_state[k][h]
                o_ref[pl.ds(r, seg), h * GDN_DV:(h + 1) * GDN_DV] = outs[k][h].astype(o_ref.dtype)
        return carry

    lax.fori_loop(0, nseg // GDN_SEQS_PER_STEP, per_group, 0)

    @pl.when(pl.program_id(0) == pl.num_programs(0) - 1)
    def _():
        sout_ref[...] = s_ref[...]


def _gdn_prompt(proj, ba, conv_w, alog, dtb, norm_g, *, n_tiles, nseg, seg):
    rows = nseg * seg
    cd = conv_w.shape[1]
    vw = GDN_HEADS * GDN_DV
    pad = 8
    const2 = lambda i: (0, 0)
    return pl.pallas_call(
        functools.partial(_gdn_prompt_body, nseg=nseg, seg=seg, pad=pad),
        grid=(n_tiles,),
        in_specs=[pl.BlockSpec((rows, cd), lambda i: (i, 0)),
                  pl.BlockSpec((rows, vw), lambda i: (i, cd // vw)),
                  pl.BlockSpec((rows, 2 * LANES), lambda i: (i, 0)),
                  pl.BlockSpec((CONV_W, cd), const2),
                  pl.BlockSpec((1, LANES), const2),
                  pl.BlockSpec((1, LANES), const2),
                  pl.BlockSpec((1, GDN_DV), const2)],
        out_specs=[pl.BlockSpec((rows, vw), lambda i: (i, 0)),
                   pl.BlockSpec((nseg, GDN_HEADS, GDN_DK, GDN_DV), lambda i: (0, 0, 0, 0)),
                   pl.BlockSpec((nseg, pad, cd), lambda i: (0, 0, 0))],
        out_shape=[jax.ShapeDtypeStruct((n_tiles * rows, vw), _BF16),
                   jax.ShapeDtypeStruct((nseg, GDN_HEADS, GDN_DK, GDN_DV), _F32),
                   jax.ShapeDtypeStruct((nseg, pad, cd), _F32)],
        scratch_shapes=[pltpu.VMEM((nseg, pad + seg, cd), _F32),
                        pltpu.VMEM((nseg, GDN_HEADS, GDN_DK, GDN_DV), _F32)],
        compiler_params=_cparams(1),
        name="gdn_prompt",
    )(proj, proj, ba, conv_w, alog, dtb, norm_g)


def _gdn_sample_body(qkv_ref, z_ref, ba_ref, s0_ref, c0_ref, cw_ref, alog_ref, dtb_ref, ng_ref, sall_ref,
                     o_ref, sout_ref, cout_ref, xp_ref, *, nb, seg, steps, pad):
    del sall_ref
    s0_ref, sout_ref = s0_ref.at[0], sout_ref.at[0]
    def per_group(i, carry):
        bs = [i * GDN_SEQS_PER_STEP + k for k in range(GDN_SEQS_PER_STEP)]
        seqs = []
        for k, b in enumerate(bs):
            xp_ref[k, 0:pad, :] = c0_ref[b]
            qkv = _gdn_conv(xp_ref, k, qkv_ref[b], cw_ref, pad=pad, seg=seg)
            cout_ref[b] = xp_ref[k, pl.ds(steps, pad), :]
            beta, g = _gdn_gates(ba_ref[b], alog_ref[...], dtb_ref[...])
            live = lax.broadcasted_iota(jnp.int32, beta.shape, 0) < steps
            seqs.append((qkv, z_ref[b], jnp.where(live, beta, 0.0), jnp.where(live, g, 0.0),
                         [s0_ref[b, h] for h in range(GDN_HEADS)]))
        outs, new_state = _gdn_chunks(seqs, ng_ref[...], c=seg, hg=GDN_HEADS)
        for k, b in enumerate(bs):
            for h in range(GDN_HEADS):
                sout_ref[b, h] = new_state[k][h]
                o_ref[b, :, h * GDN_DV:(h + 1) * GDN_DV] = outs[k][h]
        return carry

    lax.fori_loop(0, nb // GDN_SEQS_PER_STEP, per_group, 0)


def _gdn_sample(proj, ba, s0_all, layer, s_new_all, c0, conv_w, alog, dtb, norm_g, *, steps):
    nbatch, seg, _ = proj.shape
    cd = conv_w.shape[1]
    vw = GDN_HEADS * GDN_DV
    pad = c0.shape[1]
    nb = 8
    const2 = lambda i: (0, 0)
    return pl.pallas_call(
        functools.partial(_gdn_sample_body, nb=nb, seg=seg, steps=steps, pad=pad),
        grid=(nbatch // nb,),
        in_specs=[pl.BlockSpec((nb, seg, cd), lambda i: (i, 0, 0)),
                  pl.BlockSpec((nb, seg, vw), lambda i: (i, 0, cd // vw)),
                  pl.BlockSpec((nb, seg, 2 * LANES), lambda i: (i, 0, 0)),
                  pl.BlockSpec((1, nb, GDN_HEADS, GDN_DK, GDN_DV), lambda i: (layer, i, 0, 0, 0)),
                  pl.BlockSpec((nb, pad, cd), lambda i: (i, 0, 0)),
                  pl.BlockSpec((CONV_W, cd), const2),
                  pl.BlockSpec((1, LANES), const2),
                  pl.BlockSpec((1, LANES), const2),
                  pl.BlockSpec((1, GDN_DV), const2),
                  pl.BlockSpec(memory_space=pl.ANY)],
        out_specs=[pl.BlockSpec((nb, seg, vw), lambda i: (i, 0, 0)),
                   pl.BlockSpec((1, nb, GDN_HEADS, GDN_DK, GDN_DV), lambda i: (layer, i, 0, 0, 0)),
                   pl.BlockSpec((nb, pad, cd), lambda i: (i, 0, 0))],
        out_shape=[jax.ShapeDtypeStruct((nbatch, seg, vw), _F32),
                   jax.ShapeDtypeStruct(s_new_all.shape, _F32),
                   jax.ShapeDtypeStruct((nbatch, pad, cd), _F32)],
        scratch_shapes=[pltpu.VMEM((GDN_SEQS_PER_STEP, pad + seg, cd), _F32)],
        input_output_aliases={9: 1},
        compiler_params=_cparams(1),
        name="gdn_sample",
    )(proj, proj, ba, s0_all, c0, conv_w, alog, dtb, norm_g, s_new_all)


def _softmax_rows(s):
    e = jnp.exp(s - jnp.max(s, axis=-1, keepdims=True))
    return e / jnp.sum(e, axis=-1, keepdims=True)


def _attn_prompt_body(q_ref, k_ref, v_ref, o_ref, *, nseg, seg, ntile):
    hd = q_ref.shape[1] // XA_HEADS
    scale = hd ** -0.5
    for b in range(nseg):
        runs = [slice((t * nseg + b) * seg, (t * nseg + b + 1) * seg) for t in range(ntile)]
        heads = [slice(h * hd, (h + 1) * hd) for h in range(XA_HEADS)]
        s = [_dot_nt(jnp.concatenate([q_ref[rs, cs] for rs in runs], axis=0), k_ref[b, :, cs]) * scale
             for cs in heads]
        p = [_softmax_rows(sh).astype(_BF16) for sh in s]
        o = [_dot(ph, v_ref[b, :, cs]).astype(o_ref.dtype) for ph, cs in zip(p, heads)]
        for oh, cs in zip(o, heads):
            for t, rs in enumerate(runs):
                o_ref[rs, cs] = oh[t * seg:(t + 1) * seg]


def _attn_prompt(q, mem_k, mem_v, *, n_tiles, nseg, seg):
    ntile = math.gcd(n_tiles, 4)
    rows = nseg * seg * ntile
    d = q.shape[1]
    mem = mem_k.shape[1]
    return pl.pallas_call(
        functools.partial(_attn_prompt_body, nseg=nseg, seg=seg, ntile=ntile),
        grid=(n_tiles // ntile,),
        in_specs=[pl.BlockSpec((rows, d), lambda i: (i, 0)),
                  pl.BlockSpec((nseg, mem, d), lambda i: (0, 0, 0)),
                  pl.BlockSpec((nseg, mem, d), lambda i: (0, 0, 0))],
        out_specs=pl.BlockSpec((rows, d), lambda i: (i, 0)),
        out_shape=jax.ShapeDtypeStruct((n_tiles * nseg * seg, d), _BF16),
        compiler_params=_cparams(1),
        name="attn_prompt",
    )(q, mem_k, mem_v)


def _attn_sample_body(q_ref, k_ref, v_ref, o_ref, *, nb):
    t8 = q_ref.shape[1]
    mem, nh, hd = k_ref.shape[2:]
    scale = hd ** -0.5
    qh = lax.broadcasted_iota(jnp.int32, (nh * t8, mem * nh), 0) // t8
    kh = lax.broadcasted_iota(jnp.int32, (nh * t8, mem * nh), 1) % nh
    own = qh == kh

    def per_seq(b, carry):
        q = q_ref[b]
        q2 = jnp.concatenate([q[:, h * hd:(h + 1) * hd] for h in range(nh)], axis=0).astype(_BF16)
        k2 = k_ref[0, b].reshape(mem * nh, hd).astype(_BF16)
        v2 = v_ref[0, b].reshape(mem * nh, hd).astype(_BF16)
        s = jnp.where(own, _dot_nt(q2, k2) * scale, -jnp.inf)
        o2 = _dot(_softmax_rows(s).astype(_BF16), v2)
        for h in range(nh):
            o_ref[b, :, h * hd:(h + 1) * hd] = o2[h * t8:(h + 1) * t8]
        return carry

    lax.fori_loop(0, nb, per_seq, 0)


def _attn_sample(q, cache_k, cache_v, layer):
    nbatch, t8, d = q.shape
    mem, nh, hd = cache_k.shape[2:]
    nb = 8
    return pl.pallas_call(
        functools.partial(_attn_sample_body, nb=nb),
        grid=(nbatch // nb,),
        in_specs=[pl.BlockSpec((nb, t8, d), lambda i: (i, 0, 0)),
                  pl.BlockSpec((1, nb, mem, nh, hd), lambda i: (layer, i, 0, 0, 0)),
                  pl.BlockSpec((1, nb, mem, nh, hd), lambda i: (layer, i, 0, 0, 0))],
        out_specs=pl.BlockSpec((nb, t8, d), lambda i: (i, 0, 0)),
        out_shape=jax.ShapeDtypeStruct(q.shape, _F32),
        compiler_params=_cparams(1),
        name="attn_sample",
    )(q, cache_k, cache_v)


def kernel(x_prompt, x_sample, mem_prompt, state_lru_h, state_lru_conv, state_gdn_s, state_gdn_conv,
           cache_mem_k, cache_mem_v, ln_g, ln_b, xa_wq, xa_wk, xa_wv, xa_wo,
           lru_w_in, lru_conv_w, lru_conv_b, lru_gate_w, lru_gate_b, lru_lambda, lru_w_out,
           gdn_w_in, gdn_conv_w, gdn_a_log, gdn_dt_bias, gdn_norm_g, gdn_w_out,
           ffn_w_gu, ffn_w_down, moe_router, moe_w_gu, moe_w_down):
    depth = ln_g.shape[0]
    bp, seq, d = x_prompt.shape
    bs, ts, _ = x_sample.shape
    mem = mem_prompt.shape[1]
    alpha = (2 * depth) ** 0.25
    seg = GDN_CHUNK
    nchunk = seq // seg
    mp, ms = bp * seq, bs * ts
    assert bp * seg == ROW_TILE and ms == ROW_TILE and seq % seg == 0 and ts >= CONV_W - 1
    assert depth % 2 == 0
    t8 = 8
    pad_p = 8
    pad_s = (CONV_W - 1) * bs
    r = lru_w_in.shape[2] // 2
    cd = gdn_conv_w.shape[2]
    vw = GDN_HEADS * GDN_DV
    d_ff = ffn_w_gu.shape[2] // 2
    d_fe = moe_w_gu.shape[3] // 2

    x, xb = _pack_tokens(x_prompt.reshape(bp, nchunk, seg, d), x_sample.reshape(ms, d))

    memx = mem_prompt.reshape(bp * mem, d)
    mem_k_p = _mm(memx, xa_wk, ng=depth, tn=1024, out_dtype=_F32).reshape(depth, bp, mem, d)
    mem_v_p = _mm(memx, xa_wv, ng=depth, tn=1024, out_dtype=_F32).reshape(depth, bp, mem, d)
    mem_k_b = mem_k_p.astype(_BF16)
    mem_v_b = mem_v_p.astype(_BF16)

    ln_g3 = ln_g.reshape(depth * 3, 1, d)
    moe_w_gu3 = moe_w_gu.reshape((-1,) + moe_w_gu.shape[2:]).astype(_BF16)
    moe_w_down3 = moe_w_down.reshape((-1,) + moe_w_down.shape[2:]).astype(_BF16)
    assert mp + ms < (1 << IDX_SHIFT)
    ln_b3 = ln_b.reshape(depth * 3, 1, d)

    def to_time_major(a):
        return a.reshape(bs, ts, -1).transpose(1, 0, 2).reshape(ms, -1)

    def to_batch_major(a):
        return a.reshape(ts, bs, -1).transpose(1, 0, 2).reshape(ms, -1)

    def pad_t(a):
        a = a.reshape(bs, ts, -1)
        return jnp.pad(a, ((0, 0), (0, t8 - ts), (0, 0)))

    lru_h_p, lru_c_p, lru_h_s, lru_c_s = [], [], [], []
    gdn_s_p, gdn_c_p, gdn_c_s = [], [], []
    gdn_s_s = jnp.zeros(state_gdn_s.shape, _F32)

    for i in range(depth):
        j = i // 2
        if i % 2 == 0:
            gu = _mm(xb, lru_w_in, g0=j, tn=1024, out_dtype=_F32)[0]
            lam = lru_lambda[j].reshape(1, r)
            cb = lru_conv_b[j].reshape(1, r)
            gb = lru_gate_b[j].reshape(LRU_BLOCKS, 1, -1)
            y, h_p, c_p = _lru_seq(gu, jnp.zeros((bp, r), _F32), jnp.zeros((bp, pad_p, r), _F32),
                                   lru_conv_w[j], cb, lru_gate_w[j], gb, lam,
                                   n_tiles=nchunk, nseg=bp, seg=seg, stride=1, pad=pad_p)
            gu_s = to_time_major(gu[mp:])
            c0_s = state_lru_conv[j].transpose(1, 0, 2).reshape(1, pad_s, r)
            y_s, h_s, c_s = _lru_seq(gu_s, state_lru_h[j], c0_s, lru_conv_w[j], cb, lru_gate_w[j], gb, lam,
                                     n_tiles=1, nseg=1, seg=ms, stride=bs, pad=pad_s)
            y_s = to_batch_major(y_s)
            lru_h_p.append(h_p)
            lru_c_p.append(c_p[:, pad_p - (CONV_W - 1):, :])
            lru_h_s.append(h_s)
            lru_c_s.append(c_s.reshape(CONV_W - 1, bs, r).transpose(1, 0, 2))
            w_out, wi = lru_w_out, j
        else:
            proj = _mm(xb, gdn_w_in, g0=j, n=cd + vw, tn=1024, out_dtype=_F32)[0]
            w_tail = gdn_w_in[j, :, cd + vw:]
            w_ba = jnp.zeros((1, d, 2 * LANES), _F32)
            w_ba = w_ba.at[0, :, :GDN_HEADS].set(w_tail[:, :GDN_HEADS])
            w_ba = w_ba.at[0, :, LANES:LANES + GDN_HEADS].set(w_tail[:, GDN_HEADS:])
            ba = _mm(xb, w_ba, tn=2 * LANES, out_dtype=_F32)[0]
            alog = jnp.zeros((1, LANES), _F32).at[0, :GDN_HEADS].set(gdn_a_log[j])
            dtb = jnp.zeros((1, LANES), _F32).at[0, :GDN_HEADS].set(gdn_dt_bias[j])
            ng = gdn_norm_g[j].reshape(1, GDN_DV)
            y, s_p, c_p = _gdn_prompt(proj, ba, gdn_conv_w[j], alog, dtb, ng, n_tiles=nchunk, nseg=bp, seg=seg)
            c0_s = jnp.pad(state_gdn_conv[j], ((0, 0), (pad_p - (CONV_W - 1), 0), (0, 0)))
            y_s, gdn_s_s, c_s = _gdn_sample(pad_t(proj[mp:]), pad_t(ba[mp:]), state_gdn_s, j, gdn_s_s, c0_s,
                                            gdn_conv_w[j], alog, dtb, ng, steps=ts)
            y_s = y_s[:, :ts, :].reshape(ms, vw)
            gdn_s_p.append(s_p)
            gdn_c_p.append(c_p[:, pad_p - (CONV_W - 1):, :])
            gdn_c_s.append(c_s[:, pad_p - (CONV_W - 1):, :])
            w_out, wi = gdn_w_out, j
        x, xb, q = _mm_ln(y, y_s, w_out, wi, x, ln_g3, ln_b3, 3 * i, alpha=alpha, proj=(xa_wq, i))

        o = _attn_prompt(q, mem_k_b[i], mem_v_b[i], n_tiles=nchunk, nseg=bp, seg=seg)
        o_s = _attn_sample(pad_t(q[mp:].astype(_F32)), cache_mem_k, cache_mem_v, i)[:, :ts, :].reshape(ms, d)

        if i % 2 == 0:
            x, xb = _mm_ln(o, o_s, xa_wo, i, x, ln_g3, ln_b3, 3 * i + 1, alpha=alpha)
            hdn = _gate_up(xb, ffn_w_gu, g0=j, ng=1, f=d_ff, tf=512)[0]
            x, xb = _mm_ln(hdn, None, ffn_w_down, j, x, ln_g3, ln_b3, 3 * i + 2, alpha=alpha)
        else:
            router_pad = jnp.zeros((d, LANES), _F32).at[:, :N_EXPERTS].set(moe_router[j])
            x, xb, rt = _mm_ln(o, o_s, xa_wo, i, x, ln_g3, ln_b3, 3 * i + 1, alpha=alpha, route=router_pad)
            y2 = _moe_routed(x, _route_plan(rt, ROW_TILE), moe_w_gu3, moe_w_down3, j * N_EXPERTS, tm=ROW_TILE)
            if i < depth - 1:
                x, xb = _combine_ln(y2, rt, x, ln_g3, ln_b3, 3 * i + 2, alpha=alpha)
            else:
                y_prompt, y_sample = _combine_ln(y2, rt, x, ln_g3, ln_b3, 3 * i + 2, alpha=alpha,
                                                 prompt_shape=(bp, nchunk, seg))

    y_prompt = y_prompt.reshape(bp, seq, d)
    y_sample = y_sample.reshape(bs, ts, d)
    hd = d // XA_HEADS
    return (y_prompt, y_sample,
            jnp.stack(lru_h_p), jnp.stack(lru_c_p), jnp.stack(gdn_s_p), jnp.stack(gdn_c_p),
            mem_k_p.reshape(depth, bp, mem, XA_HEADS, hd), mem_v_p.reshape(depth, bp, mem, XA_HEADS, hd),
            jnp.stack(lru_h_s), jnp.stack(lru_c_s), gdn_s_s, jnp.stack(gdn_c_s))
```

```python
import functools
import math

import jax
import jax.numpy as jnp
from jax import lax
from jax.experimental import pallas as pl
from jax.experimental.pallas import tpu as pltpu

_F32 = jnp.float32
_BF16 = jnp.bfloat16
_HI = lax.Precision.HIGHEST

CONV_W = 4
LRU_BLOCKS = 4
LRU_C = 8.0
GDN_HEADS = 8
GDN_DK = 128
GDN_DV = 128
GDN_CHUNK = 64
GDN_SEQS_PER_STEP = 4
GDN_STACK_ROWS = 128
XA_HEADS = 4
N_EXPERTS = 8
IDX_SHIFT = 15
LN_EPS = 1e-5
RMS_EPS = 1e-6
LANES = 128
SUBLANES = 8
ROW_TILE = 512
VMEM_LIMIT_MB = 56


def _big_row_tile(m):
    return 3 * ROW_TILE if m % (3 * ROW_TILE) == 0 else min(ROW_TILE, m)


def _cparams(ndim, vmem_mb=VMEM_LIMIT_MB):
    return pltpu.CompilerParams(dimension_semantics=("arbitrary",) * ndim,
                                vmem_limit_bytes=vmem_mb * 1024 * 1024)


def _dot(a, b, precision=None):
    return jnp.dot(a, b, preferred_element_type=_F32, precision=precision)


def _bdot(a, b):
    return _dot(a.astype(_BF16), b.astype(_BF16))


def _dot_nt(a, b):
    return lax.dot_general(a, b, (((1,), (1,)), ((), ())), preferred_element_type=_F32)


def _dot_tn(a, b):
    return lax.dot_general(a, b, (((0,), (0,)), ((), ())), preferred_element_type=_F32)


def _layer_norm(v, g, b):
    mu = jnp.mean(v, axis=-1, keepdims=True)
    d = v - mu
    var = jnp.mean(d * d, axis=-1, keepdims=True)
    return d * lax.rsqrt(var + LN_EPS) * g + b


def _mm_body(x_ref, w_ref, o_ref, wb_ref):
    @pl.when(pl.program_id(2) == 0)
    def _():
        wb_ref[...] = w_ref[0].astype(_BF16)

    o_ref[0] = _dot(x_ref[...].astype(_BF16), wb_ref[...]).astype(o_ref.dtype)


def _mm(x, w, *, g0=0, ng=1, n0=0, n=None, tn, out_dtype):
    m, k = x.shape
    n = w.shape[2] if n is None else n
    tm = _big_row_tile(m)
    nb0 = n0 // tn
    return pl.pallas_call(
        _mm_body,
        grid=(ng, n // tn, m // tm),
        in_specs=[pl.BlockSpec((tm, k), lambda g, j, i: (i, 0)),
                  pl.BlockSpec((1, k, tn), lambda g, j, i: (g + g0, 0, j + nb0))],
        out_specs=pl.BlockSpec((1, tm, tn), lambda g, j, i: (g, i, j)),
        out_shape=jax.ShapeDtypeStruct((ng, m, n), out_dtype),
        scratch_shapes=[pltpu.VMEM((k, tn), _BF16)],
        compiler_params=_cparams(3),
        name="mm",
    )(x, w)


def _mm_ln_body(*refs, alpha, n1, post):
    if post is None:
        y_ref, y2_ref, w_ref, x_ref, g_ref, b_ref, o_ref, ob_ref, wb_ref = refs
    elif post == "proj":
        y_ref, y2_ref, w_ref, x_ref, g_ref, b_ref, w2_ref, o_ref, ob_ref, p_ref, wb_ref, wb2_ref = refs
    else:
        y_ref, y2_ref, w_ref, x_ref, g_ref, b_ref, r_ref, o_ref, ob_ref, p_ref, wb_ref = refs
    i = pl.program_id(0)

    @pl.when(i == 0)
    def _():
        wb_ref[...] = w_ref[0].astype(_BF16)
        if post == "proj":
            wb2_ref[...] = w2_ref[0].astype(_BF16)

    def emit(src_ref):
        mix = _dot(src_ref[...].astype(_BF16), wb_ref[...])
        res = _layer_norm(alpha * x_ref[...] + mix, g_ref[0], b_ref[0])
        o_ref[...] = res
        ob_ref[...] = res.astype(_BF16)
        if post == "proj":
            p_ref[...] = _dot(res.astype(_BF16), wb2_ref[...]).astype(p_ref.dtype)
        elif post == "route":
            p_ref[...] = _route_top2(res, r_ref[...])

    if n1 is None:
        emit(y_ref)
    else:
        pl.when(i < n1)(lambda: emit(y_ref))
        pl.when(i >= n1)(lambda: emit(y2_ref))


def _mm_ln(y, y2, w, gi, x, ln_g, ln_b, li, *, alpha, proj=None, route=None):
    m, d = x.shape
    k = y.shape[1]
    tm = min(ROW_TILE, m)
    n1 = y.shape[0] // tm
    if y2 is None:
        y2 = y
    assert n1 * tm == y.shape[0] and (n1 == m // tm or n1 * tm + y2.shape[0] == m)
    assert proj is None or route is None
    row = lambda i: (i, 0)
    in_specs = [pl.BlockSpec((tm, k), lambda i: (jnp.minimum(i, n1 - 1), 0)),
                pl.BlockSpec((tm, k), lambda i: (jnp.maximum(i - n1, 0), 0)),
                pl.BlockSpec((1, k, d), lambda i: (gi, 0, 0), pipeline_mode=pl.Buffered(1)),
                pl.BlockSpec((tm, d), row),
                pl.BlockSpec((1, 1, d), lambda i: (li, 0, 0)),
                pl.BlockSpec((1, 1, d), lambda i: (li, 0, 0))]
    args = [y, y2, w, x, ln_g, ln_b]
    out_specs = [pl.BlockSpec((tm, d), row), pl.BlockSpec((tm, d), row)]
    out_shape = [jax.ShapeDtypeStruct((m, d), _F32), jax.ShapeDtypeStruct((m, d), _BF16)]
    scratch = [pltpu.VMEM((k, d), _BF16)]
    post = None
    if proj is not None:
        post = "proj"
        w2, g2 = proj
        n2 = w2.shape[2]
        in_specs.append(pl.BlockSpec((1, d, n2), lambda i: (g2, 0, 0), pipeline_mode=pl.Buffered(1)))
        args.append(w2)
        out_specs.append(pl.BlockSpec((tm, n2), row))
        out_shape.append(jax.ShapeDtypeStruct((m, n2), _BF16))
        scratch.append(pltpu.VMEM((d, n2), _BF16))
    elif route is not None:
        post = "route"
        in_specs.append(pl.BlockSpec((d, LANES), lambda i: (0, 0)))
        args.append(route)
        out_specs.append(pl.BlockSpec((tm, LANES), row))
        out_shape.append(jax.ShapeDtypeStruct((m, LANES), _F32))
    return pl.pallas_call(
        functools.partial(_mm_ln_body, alpha=alpha, n1=None if n1 == m // tm else n1, post=post),
        grid=(m // tm,),
        in_specs=in_specs,
        out_specs=out_specs,
        out_shape=out_shape,
        scratch_shapes=scratch,
        compiler_params=_cparams(1),
        name="mm_ln",
    )(*args)


def _gu_body(x_ref, wg_ref, wu_ref, o_ref, wgb_ref, wub_ref):
    @pl.when(pl.program_id(2) == 0)
    def _():
        wgb_ref[...] = wg_ref[0].astype(_BF16)
        wub_ref[...] = wu_ref[0].astype(_BF16)

    xb = x_ref[...].astype(_BF16)
    g = _dot(xb, wgb_ref[...])
    u = _dot(xb, wub_ref[...])
    o_ref[0] = (jax.nn.silu(g) * u).astype(o_ref.dtype)


def _gate_up(x, w_gu, *, g0, ng, f, tf):
    m, k = x.shape
    tm = _big_row_tile(m)
    nf = f // tf
    return pl.pallas_call(
        _gu_body,
        grid=(ng, nf, m // tm),
        in_specs=[pl.BlockSpec((tm, k), lambda g, j, i: (i, 0)),
                  pl.BlockSpec((1, k, tf), lambda g, j, i: (g + g0, 0, j)),
                  pl.BlockSpec((1, k, tf), lambda g, j, i: (g + g0, 0, j + nf))],
        out_specs=pl.BlockSpec((1, tm, tf), lambda g, j, i: (g, i, j)),
        out_shape=jax.ShapeDtypeStruct((ng, m, f), _BF16),
        scratch_shapes=[pltpu.VMEM((k, tf), _BF16), pltpu.VMEM((k, tf), _BF16)],
        compiler_params=_cparams(3),
        name="gate_up",
    )(x, w_gu, w_gu)


def _route_top2(x, router_pad):
    x_hi = x.astype(_BF16)
    x_lo = (x - x_hi.astype(_F32)).astype(_BF16)
    r_hi = router_pad.astype(_BF16)
    r_lo = (router_pad - r_hi.astype(_F32)).astype(_BF16)
    logits = _dot(x_hi, r_hi) + (_dot(x_hi, r_lo) + _dot(x_lo, r_hi))
    lane = lax.broadcasted_iota(jnp.int32, logits.shape, 1).astype(_F32)
    l1 = jnp.where(lane < N_EXPERTS, logits, -jnp.inf)
    m1 = jnp.max(l1, axis=-1, keepdims=True)
    i1 = jnp.min(jnp.where(l1 == m1, lane, float(LANES)), axis=-1, keepdims=True)
    l2 = jnp.where(lane == i1, -jnp.inf, l1)
    m2 = jnp.max(l2, axis=-1, keepdims=True)
    i2 = jnp.min(jnp.where(l2 == m2, lane, float(LANES)), axis=-1, keepdims=True)
    e2 = jnp.exp(m2 - m1)
    den = 1.0 + e2
    return (jnp.where(lane == 0.0, i1, 0.0) + jnp.where(lane == 1.0, i2, 0.0)
            + jnp.where(lane == 2.0, 1.0 / den, 0.0) + jnp.where(lane == 3.0, e2 / den, 0.0))


def _route_plan(rt, tm):
    m = rt.shape[0]
    ef = rt[:, :2].astype(jnp.int32).reshape(-1)
    onehot = (ef[:, None] == jnp.arange(N_EXPERTS, dtype=jnp.int32)[None, :]).astype(jnp.int32)
    csum = jnp.cumsum(onehot, axis=0)
    cnt = csum[-1]
    rank = jnp.sum((csum - onehot) * onehot, axis=1)
    cnt_p = ((cnt + tm - 1) // tm) * tm
    end = jnp.cumsum(cnt_p)
    start = end - cnt_p
    dest = start[ef] + rank
    n_tiles = (2 * m) // tm + N_EXPERTS
    rows = n_tiles * tm
    pair = jnp.zeros((rows,), jnp.int32).at[dest].set(jnp.arange(2 * m, dtype=jnp.int32))
    tile_start = jnp.arange(n_tiles, dtype=jnp.int32) * tm
    texp = jnp.sum((tile_start[:, None] >= end[None, :]).astype(jnp.int32), axis=1)
    used = tile_start < end[-1]
    last = jnp.maximum(end[-1] // tm - 1, 0)
    texp = jnp.where(used, texp, texp[last])
    nval = jnp.where(used, jnp.clip(start[texp] + cnt[texp] - tile_start, 0, tm), 0)
    tile = jnp.repeat(jnp.arange(n_tiles, dtype=jnp.int32), tm)
    in_tile = jnp.arange(rows, dtype=jnp.int32) - tile * tm
    live = in_tile < jnp.repeat(nval, tm)
    tok, slot = pair // 2, pair % 2
    spare = 2 * m + (tile % 2) * tm + in_tile
    idx = jnp.where(live, tok, 0) | (jnp.where(live, slot * m + tok, spare) << IDX_SHIFT)
    n_used = (end[-1] // tm).reshape(1)
    return texp, n_used, idx


def _moe_routed_body(texp_ref, nused_ref, idx_ref, x_hbm, wg_ref, wu_ref, wd_ref, out_hbm,
                     xs_ref, ys_ref, gsem, ssem):
    i = pl.program_id(0)
    ng = xs_ref.shape[1]
    tm = ng * SUBLANES
    d = xs_ref.shape[3]
    n_used = nused_ref[0]
    slot = i % 2

    def gather(tile, s):
        def body(g, c):
            for u in range(SUBLANES):
                tok = idx_ref[tile * tm + g * SUBLANES + u] & ((1 << IDX_SHIFT) - 1)
                pltpu.make_async_copy(x_hbm.at[tok >> 3, tok & 7], xs_ref.at[s, g, u], gsem.at[s]).start()
            return c
        lax.fori_loop(0, ng, body, 0)

    def scatter(tile, s):
        def body(g, c):
            for u in range(SUBLANES):
                dst = idx_ref[tile * tm + g * SUBLANES + u] >> IDX_SHIFT
                pltpu.make_async_copy(ys_ref.at[s, g, u], out_hbm.at[dst >> 3, dst & 7], ssem.at[s]).start()
            return c
        lax.fori_loop(0, ng, body, 0)

    def wait_gather(s):
        pltpu.make_async_copy(x_hbm.at[pl.ds(0, ng)], xs_ref.at[s], gsem.at[s]).wait()

    def wait_scatter(s):
        pltpu.make_async_copy(ys_ref.at[s], out_hbm.at[pl.ds(0, ng)], ssem.at[s]).wait()

    @pl.when((i == 0) & (n_used > 0))
    def _():
        gather(0, 0)

    @pl.when(i + 1 < n_used)
    def _():
        gather(i + 1, 1 - slot)

    @pl.when(i < n_used)
    def _():
        wait_gather(slot)

        @pl.when(i >= 2)
        def _():
            wait_scatter(slot)

        xb = xs_ref[slot].reshape(tm, d).astype(_BF16)
        h = (jax.nn.silu(_dot(xb, wg_ref[0])) * _dot(xb, wu_ref[0])).astype(_BF16)
        ys_ref[slot] = _dot(h, wd_ref[0]).reshape(ng, SUBLANES, d)
        scatter(i, slot)

    @pl.when(i == pl.num_programs(0) - 1)
    def _():
        @pl.when(n_used >= 1)
        def _():
            wait_scatter((n_used - 1) % 2)

        @pl.when(n_used >= 2)
        def _():
            wait_scatter(n_used % 2)

        ys_ref[0] = jnp.zeros(ys_ref.shape[1:], _F32)
        spare0 = out_hbm.shape[0] - 2 * ng
        fills = [pltpu.make_async_copy(ys_ref.at[0], out_hbm.at[pl.ds(spare0 + k * ng, ng)], ssem.at[0])
                 for k in range(2)]
        for cp in fills:
            cp.start()
        for cp in fills:
            cp.wait()


def _moe_routed(x, plan, w_gu_bf16, w_down_bf16, e0, *, tm):
    texp, n_used, idx = plan
    m, d = x.shape
    f = w_down_bf16.shape[1]
    n_tiles = texp.shape[0]
    return pl.pallas_call(
        _moe_routed_body,
        grid_spec=pltpu.PrefetchScalarGridSpec(
            num_scalar_prefetch=3,
            grid=(n_tiles,),
            in_specs=[pl.BlockSpec(memory_space=pl.ANY),
                      pl.BlockSpec((1, d, f), lambda i, te, nu, ix: (te[i] + e0, 0, 0)),
                      pl.BlockSpec((1, d, f), lambda i, te, nu, ix: (te[i] + e0, 0, 1)),
                      pl.BlockSpec((1, f, d), lambda i, te, nu, ix: (te[i] + e0, 0, 0))],
            out_specs=pl.BlockSpec(memory_space=pl.ANY),
            scratch_shapes=[pltpu.VMEM((2, tm // SUBLANES, SUBLANES, d), _F32),
                            pltpu.VMEM((2, tm // SUBLANES, SUBLANES, d), _F32),
                            pltpu.SemaphoreType.DMA((2,)), pltpu.SemaphoreType.DMA((2,))]),
        out_shape=jax.ShapeDtypeStruct(((2 * m + 2 * tm) // SUBLANES, SUBLANES, d), _F32),
        compiler_params=_cparams(1),
        name="moe_routed",
    )(texp, n_used, idx, x.reshape(m // SUBLANES, SUBLANES, d), w_gu_bf16, w_gu_bf16,
      w_down_bf16).reshape(2 * m + 2 * tm, d)


def _combine_ln_body(a_ref, b_ref, rt_ref, x_ref, g_ref, bb_ref, *o_refs, alpha, n_prompt):
    ff = rt_ref[:, 2:3] * a_ref[...] + rt_ref[:, 3:4] * b_ref[...]
    res = _layer_norm(alpha * x_ref[...] + ff, g_ref[0], bb_ref[0])
    if n_prompt is None:
        o_refs[0][...] = res
        o_refs[1][...] = res.astype(_BF16)
    else:
        i = pl.program_id(0)
        op_ref, os_ref = o_refs

        @pl.when(i < n_prompt)
        def _():
            op_ref[...] = res.reshape(op_ref.shape)

        @pl.when(i >= n_prompt)
        def _():
            os_ref[...] = res


def _combine_ln(y2, rt, x, ln_g, ln_b, li, *, alpha, prompt_shape=None):
    m, d = x.shape
    tm = min(ROW_TILE, m)
    nt = m // tm
    if prompt_shape is None:
        n_prompt = None
        out_specs = [pl.BlockSpec((tm, d), lambda i: (i, 0)), pl.BlockSpec((tm, d), lambda i: (i, 0))]
        out_shape = [jax.ShapeDtypeStruct((m, d), _F32), jax.ShapeDtypeStruct((m, d), _BF16)]
    else:
        bp, n_prompt, seg = prompt_shape
        assert bp * seg == tm and n_prompt == nt - 1
        out_specs = [pl.BlockSpec((bp, 1, seg, d), lambda i: (0, jnp.minimum(i, n_prompt - 1), 0, 0)),
                     pl.BlockSpec((tm, d), lambda i: (0, 0))]
        out_shape = [jax.ShapeDtypeStruct((bp, n_prompt, seg, d), _F32), jax.ShapeDtypeStruct((tm, d), _F32)]
    return pl.pallas_call(
        functools.partial(_combine_ln_body, alpha=alpha, n_prompt=n_prompt),
        grid=(nt,),
        in_specs=[pl.BlockSpec((tm, d), lambda i: (i, 0)),
                  pl.BlockSpec((tm, d), lambda i: (i + nt, 0)),
                  pl.BlockSpec((tm, LANES), lambda i: (i, 0)),
                  pl.BlockSpec((tm, d), lambda i: (i, 0)),
                  pl.BlockSpec((1, 1, d), lambda i: (li, 0, 0)),
                  pl.BlockSpec((1, 1, d), lambda i: (li, 0, 0))],
        out_specs=out_specs,
        out_shape=out_shape,
        compiler_params=_cparams(1),
        name="combine_ln",
    )(y2, y2, rt, x, ln_g, ln_b)


def _pack_body(xp_ref, xs_ref, o_ref, ob_ref, *, n_prompt):
    i = pl.program_id(0)

    @pl.when(i < n_prompt)
    def _():
        o_ref[...] = xp_ref[...].reshape(o_ref.shape)
        ob_ref[...] = xp_ref[...].reshape(o_ref.shape).astype(_BF16)

    @pl.when(i >= n_prompt)
    def _():
        o_ref[...] = xs_ref[...]
        ob_ref[...] = xs_ref[...].astype(_BF16)


def _pack_tokens(x_prompt4, x_sample2):
    bp, n_prompt, seg, d = x_prompt4.shape
    tm = bp * seg
    assert x_sample2.shape == (tm, d)
    return pl.pallas_call(
        functools.partial(_pack_body, n_prompt=n_prompt),
        grid=(n_prompt + 1,),
        in_specs=[pl.BlockSpec((bp, 1, seg, d), lambda i: (0, jnp.minimum(i, n_prompt - 1), 0, 0)),
                  pl.BlockSpec((tm, d), lambda i: (0, 0))],
        out_specs=[pl.BlockSpec((tm, d), lambda i: (i, 0)), pl.BlockSpec((tm, d), lambda i: (i, 0))],
        out_shape=[jax.ShapeDtypeStruct(((n_prompt + 1) * tm, d), _F32),
                   jax.ShapeDtypeStruct(((n_prompt + 1) * tm, d), _BF16)],
        compiler_params=_cparams(1),
        name="pack_tokens",
    )(x_prompt4, x_sample2)


def _seg_scan(a, b, tstep, n_steps, stride):
    sh = stride
    for s in range(n_steps):
        keep = tstep >= (1 << s)
        a_sh = pltpu.roll(a, sh, 0)
        b_sh = pltpu.roll(b, sh, 0)
        b = jnp.where(keep, a * b_sh + b, b)
        a = jnp.where(keep, a * a_sh, a)
        sh *= 2
    return a, b


def _lru_body(gate_ref, rec_ref, h0_ref, c0_ref, cw_ref, cb_ref, gw_ref, gb_ref, lam_ref,
              y_ref, hout_ref, cout_ref, xp_ref, hc_ref, cv_ref, gwb_ref, *, nseg, seg, stride, pad):
    rows = nseg * seg
    bw = gw_ref.shape[1]

    @pl.when(pl.program_id(0) == 0)
    def _():
        xp_ref[:, 0:pad, :] = c0_ref[...]
        hc_ref[...] = h0_ref[...]
        gwb_ref[...] = gw_ref[...].astype(_BF16)

    for s in range(nseg):
        xp_ref[s, pad:pad + seg, :] = rec_ref[s * seg:(s + 1) * seg, :]
        acc = xp_ref[s, pad - 3 * stride:pad - 3 * stride + seg, :] * cw_ref[0:1, :]
        for j in range(1, CONV_W):
            off = pad - (CONV_W - 1 - j) * stride
            acc = acc + xp_ref[s, off:off + seg, :] * cw_ref[j:j + 1, :]
        cv_ref[s * seg:(s + 1) * seg, :] = acc + cb_ref[...]
        tail = xp_ref[s, seg:seg + pad, :]
        xp_ref[s, 0:pad, :] = tail
        cout_ref[s] = tail

    row = lax.broadcasted_iota(jnp.int32, (rows, bw), 0)
    if stride == 1:
        tstep = row & (seg - 1)
    else:
        tstep = row >> int(math.log2(stride))
    n_steps = int(math.log2(seg // stride))

    for n in range(LRU_BLOCKS):
        cs = slice(n * bw, (n + 1) * bw)
        cv = cv_ref[:, cs]
        gates = _dot(cv.astype(_BF16), gwb_ref[n]) + gb_ref[n]
        r = jax.nn.sigmoid(gates[:, :bw])
        ig = jax.nn.sigmoid(gates[:, bw:])
        log_a = -LRU_C * r * jax.nn.softplus(-lam_ref[:, cs])
        a = jnp.exp(log_a)
        s = -jnp.tanh(log_a) * (a * a + 1.0)
        mult = jnp.where(s > 0.0, s * lax.rsqrt(s), 0.0)
        a_cum, b_cum = _seg_scan(a, mult * ig * cv, tstep, n_steps, stride)
        hc = hc_ref[:, cs]
        if stride == 1:
            hprev = jnp.concatenate([jnp.broadcast_to(hc[s:s + 1, :], (seg, bw)) for s in range(nseg)], axis=0)
        else:
            hprev = jnp.concatenate([hc] * (seg // stride), axis=0)
        h = a_cum * hprev + b_cum
        y_ref[:, cs] = (h * jax.nn.gelu(gate_ref[:, cs])).astype(y_ref.dtype)
        if stride == 1:
            hc_ref[:, cs] = jnp.concatenate([h[(s + 1) * seg - 1:(s + 1) * seg, :] for s in range(nseg)], axis=0)
        else:
            hc_ref[:, cs] = h[rows - stride:, :]

    hout_ref[...] = hc_ref[...]


def _lru_seq(gu, h0, c0, conv_w, conv_b, gate_w, gate_b, lam, *, n_tiles, nseg, seg, stride, pad):
    rows = nseg * seg
    r = gu.shape[1] // 2
    bw = r // LRU_BLOCKS
    const2 = lambda i: (0, 0)
    const3 = lambda i: (0, 0, 0)
    return pl.pallas_call(
        functools.partial(_lru_body, nseg=nseg, seg=seg, stride=stride, pad=pad),
        grid=(n_tiles,),
        in_specs=[pl.BlockSpec((rows, r), lambda i: (i, 0)),
                  pl.BlockSpec((rows, r), lambda i: (i, 1)),
                  pl.BlockSpec((nseg * stride, r), const2),
                  pl.BlockSpec((nseg, pad, r), const3),
                  pl.BlockSpec((CONV_W, r), const2),
                  pl.BlockSpec((1, r), const2),
                  pl.BlockSpec((LRU_BLOCKS, bw, 2 * bw), const3),
                  pl.BlockSpec((LRU_BLOCKS, 1, 2 * bw), const3),
                  pl.BlockSpec((1, r), const2)],
        out_specs=[pl.BlockSpec((rows, r), lambda i: (i, 0)),
                   pl.BlockSpec((nseg * stride, r), const2),
                   pl.BlockSpec((nseg, pad, r), const3)],
        out_shape=[jax.ShapeDtypeStruct((n_tiles * rows, r), _BF16),
                   jax.ShapeDtypeStruct((nseg * stride, r), _F32),
                   jax.ShapeDtypeStruct((nseg, pad, r), _F32)],
        scratch_shapes=[pltpu.VMEM((nseg, pad + seg, r), _F32),
                        pltpu.VMEM((nseg * stride, r), _F32),
                        pltpu.VMEM((rows, r), _F32),
                        pltpu.VMEM((LRU_BLOCKS, bw, 2 * bw), _BF16)],
        compiler_params=_cparams(1),
        name="lru_seq",
    )(gu, gu, h0, c0, conv_w, conv_b, gate_w, gate_b, lam)


def _l2norm(t):
    return t * lax.rsqrt(jnp.sum(t * t, axis=-1, keepdims=True) + RMS_EPS)


def _gdn_chunks(seqs, norm_g, *, c, hg):
    dk = GDN_DK
    qk_w = GDN_HEADS * dk
    r4 = hg * c
    lg_c = int(math.log2(c))
    row = lax.broadcasted_iota(jnp.int32, (r4, r4), 0)
    col = lax.broadcasted_iota(jnp.int32, (r4, r4), 1)
    blk = {1 << s: (row >> s) == (col >> s) for s in range(3, lg_c + 1)}
    incl = blk[c] & (row >= col)
    strict = blk[c] & (row > col)
    units = [(si, list(range(g0, g0 + hg))) for si in range(len(seqs)) for g0 in range(0, GDN_HEADS, hg)]
    blocks = [slice(hh * c, (hh + 1) * c) for hh in range(hg)]

    def stack(fn, hs, axis=0):
        return jnp.concatenate([fn(h) for h in hs], axis=axis)

    tr = lax.broadcasted_iota(jnp.int32, (c, c), 0)
    tc = lax.broadcasted_iota(jnp.int32, (c, c), 1)
    tri = jnp.where(tr >= tc, 1.0, 0.0)
    gc = [_dot(tri, sq[3], precision=_HI) for sq in seqs]
    gct = [jnp.concatenate([x, jnp.zeros((LANES - c, LANES), _F32)], axis=0).T for x in gc]

    q4 = [stack(lambda h: _l2norm(seqs[si][0](h * dk)) * (dk ** -0.5), hs) for si, hs in units]
    k4 = [stack(lambda h: _l2norm(seqs[si][0](qk_w + h * dk)), hs) for si, hs in units]
    v4 = [stack(lambda h: seqs[si][0](2 * qk_w + h * dk), hs) for si, hs in units]
    beta4 = [stack(lambda h: jnp.broadcast_to(seqs[si][2][:, h:h + 1], (c, dk)), hs) for si, hs in units]
    gc4 = [stack(lambda h: jnp.broadcast_to(gc[si][:, h:h + 1], (c, dk)), hs) for si, hs in units]
    gcol = [stack(lambda h: jnp.broadcast_to(gc[si][:, h:h + 1], (c, r4)), hs) for si, hs in units]
    grow = [stack(lambda h: gct[si][h:h + 1, 0:c], hs, axis=1) for si, hs in units]
    decay = [jnp.exp(jnp.where(incl, gcl - grw, -jnp.inf)) for gcl, grw in zip(gcol, grow)]
    kb4 = [k * bt for k, bt in zip(k4, beta4)]
    aq = [_dot_nt(jnp.concatenate([kb, q], axis=0).astype(_BF16), k.astype(_BF16)) for kb, q, k in zip(kb4, q4, k4)]
    a4 = [jnp.where(strict, x[:r4] * dc, 0.0) for x, dc in zip(aq, decay)]
    attn4 = [x[r4:] * dc for x, dc in zip(aq, decay)]
    c0 = min(c, 16)
    np_ = [-jnp.where(blk[c0], x, 0.0) for x in a4]
    e = list(np_)
    for _ in range(int(math.log2(c0)) - 1):
        np_ = [_bdot(x, x) for x in np_]
        e = [ei + ni + _bdot(ni, ei) for ei, ni in zip(e, np_)]
    size = c0
    while size < c:
        lo = [jnp.where(blk[2 * size] & ~blk[size], x, 0.0) for x in a4]
        m = [li + _bdot(li, ei) for li, ei in zip(lo, e)]
        e = [ei - (mi + _bdot(ei, mi)) for ei, mi in zip(e, m)]
        size *= 2
    egc4 = [jnp.exp(x) for x in gc4]
    rhs = [jnp.concatenate([v * bt, kb * eg], axis=1) for v, bt, kb, eg in zip(v4, beta4, kb4, egc4)]
    uw = [r + _bdot(ei, r) for r, ei in zip(rhs, e)]
    qg4 = [q * eg for q, eg in zip(q4, egc4)]
    ws_qs = [[_dot(jnp.concatenate([uw[u][rs, dk:], qg4[u][rs]], axis=0).astype(_BF16),
                   seqs[si][4][h].astype(_BF16)) for rs, h in zip(blocks, hs)]
             for u, (si, hs) in enumerate(units)]
    vn = [[uw[u][rs, :dk] - ws_qs[u][hh][:c] for hh, rs in enumerate(blocks)] for u in range(len(units))]
    o4 = [jnp.concatenate([x[c:] for x in ws_qs[u]], axis=0)
          + _bdot(attn4[u], jnp.concatenate(vn[u], axis=0)) for u in range(len(units))]
    outs = [[None] * GDN_HEADS for _ in seqs]
    new_state = [[None] * GDN_HEADS for _ in seqs]
    for u, (si, hs) in enumerate(units):
        z, state = seqs[si][1], seqs[si][4]
        for hh, (rs, h) in enumerate(zip(blocks, hs)):
            glast = gc[si][c - 1:c, h:h + 1]
            kd = k4[u][rs] * jnp.exp(glast - gc4[u][rs])
            new_state[si][h] = state[h] * jnp.exp(glast) + _dot_tn(kd.astype(_BF16), vn[u][hh].astype(_BF16))
            oh = o4[u][rs]
            oh = oh * lax.rsqrt(jnp.mean(oh * oh, axis=-1, keepdims=True) + RMS_EPS) * norm_g
            outs[si][h] = oh * jax.nn.silu(z[:, h * dk:(h + 1) * dk])
    return outs, new_state


def _gdn_gates(ba, alog, dtb):
    beta = jax.nn.sigmoid(ba[:, :LANES])
    g = -jnp.exp(alog) * jax.nn.softplus(ba[:, LANES:] + dtb)
    return beta, g


def _gdn_conv(xp_ref, s, x, cw_ref, *, pad, seg):
    xp_ref[s, pad:pad + seg, :] = x

    def block(col):
        cs = slice(col, col + GDN_DK)
        acc = xp_ref[s, pl.ds(pad - 3, seg), cs] * cw_ref[0:1, cs]
        for j in range(1, CONV_W):
            acc = acc + xp_ref[s, pl.ds(pad - (CONV_W - 1 - j), seg), cs] * cw_ref[j:j + 1, cs]
        return jax.nn.silu(acc)

    return block


def _gdn_prompt_body(qkv_ref, z_ref, ba_ref, cw_ref, alog_ref, dtb_ref, ng_ref,
                     o_ref, sout_ref, cout_ref, xp_ref, s_ref, *, nseg, seg, pad):
    @pl.when(pl.program_id(0) == 0)
    def _():
        xp_ref[...] = jnp.zeros(xp_ref.shape, _F32)
        s_ref[...] = jnp.zeros(s_ref.shape, _F32)

    def per_group(i, carry):
        bs = [i * GDN_SEQS_PER_STEP + k for k in range(GDN_SEQS_PER_STEP)]
        r0 = [pl.multiple_of(b * seg, seg) for b in bs]
        seqs = []
        for b, r in zip(bs, r0):
            qkv = _gdn_conv(xp_ref, b, qkv_ref[pl.ds(r, seg), :], cw_ref, pad=pad, seg=seg)
            beta, g = _gdn_gates(ba_ref[pl.ds(r, seg), :], alog_ref[...], dtb_ref[...])
            seqs.append((qkv, z_ref[pl.ds(r, seg), :], beta, g, [s_ref[b, h] for h in range(GDN_HEADS)]))
        outs, new_state = _gdn_chunks(seqs, ng_ref[...], c=seg, hg=GDN_STACK_ROWS // seg)
        for k, (b, r) in enumerate(zip(bs, r0)):
            tail = xp_ref[b, pl.ds(seg, pad), :]
            xp_ref[b, 0:pad, :] = tail
            cout_ref[b] = tail
            for h in range(GDN_HEADS):
                s_ref[b, h] = new_state[k][h]
                o_ref[pl.ds(r, seg), h * GDN_DV:(h + 1) * GDN_DV] = outs[k][h].astype(o_ref.dtype)
        return carry

    lax.fori_loop(0, nseg // GDN_SEQS_PER_STEP, per_group, 0)

    @pl.when(pl.program_id(0) == pl.num_programs(0) - 1)
    def _():
        sout_ref[...] = s_ref[...]


def _gdn_prompt(proj, ba, conv_w, alog, dtb, norm_g, *, n_tiles, nseg, seg):
    rows = nseg * seg
    cd = conv_w.shape[1]
    vw = GDN_HEADS * GDN_DV
    pad = 8
    const2 = lambda i: (0, 0)
    return pl.pallas_call(
        functools.partial(_gdn_prompt_body, nseg=nseg, seg=seg, pad=pad),
        grid=(n_tiles,),
        in_specs=[pl.BlockSpec((rows, cd), lambda i: (i, 0)),
                  pl.BlockSpec((rows, vw), lambda i: (i, cd // vw)),
                  pl.BlockSpec((rows, 2 * LANES), lambda i: (i, 0)),
                  pl.BlockSpec((CONV_W, cd), const2),
                  pl.BlockSpec((1, LANES), const2),
                  pl.BlockSpec((1, LANES), const2),
                  pl.BlockSpec((1, GDN_DV), const2)],
        out_specs=[pl.BlockSpec((rows, vw), lambda i: (i, 0)),
                   pl.BlockSpec((nseg, GDN_HEADS, GDN_DK, GDN_DV), lambda i: (0, 0, 0, 0)),
                   pl.BlockSpec((nseg, pad, cd), lambda i: (0, 0, 0))],
        out_shape=[jax.ShapeDtypeStruct((n_tiles * rows, vw), _BF16),
                   jax.ShapeDtypeStruct((nseg, GDN_HEADS, GDN_DK, GDN_DV), _F32),
                   jax.ShapeDtypeStruct((nseg, pad, cd), _F32)],
        scratch_shapes=[pltpu.VMEM((nseg, pad + seg, cd), _F32),
                        pltpu.VMEM((nseg, GDN_HEADS, GDN_DK, GDN_DV), _F32)],
        compiler_params=_cparams(1),
        name="gdn_prompt",
    )(proj, proj, ba, conv_w, alog, dtb, norm_g)


def _gdn_sample_body(qkv_ref, z_ref, ba_ref, s0_ref, c0_ref, cw_ref, alog_ref, dtb_ref, ng_ref, sall_ref,
                     o_ref, sout_ref, cout_ref, xp_ref, *, nb, seg, steps, pad):
    del sall_ref
    s0_ref, sout_ref = s0_ref.at[0], sout_ref.at[0]
    def per_group(i, carry):
        bs = [i * GDN_SEQS_PER_STEP + k for k in range(GDN_SEQS_PER_STEP)]
        seqs = []
        for k, b in enumerate(bs):
            xp_ref[k, 0:pad, :] = c0_ref[b]
            qkv = _gdn_conv(xp_ref, k, qkv_ref[b], cw_ref, pad=pad, seg=seg)
            cout_ref[b] = xp_ref[k, pl.ds(steps, pad), :]
            beta, g = _gdn_gates(ba_ref[b], alog_ref[...], dtb_ref[...])
            live = lax.broadcasted_iota(jnp.int32, beta.shape, 0) < steps
            seqs.append((qkv, z_ref[b], jnp.where(live, beta, 0.0), jnp.where(live, g, 0.0),
                         [s0_ref[b, h] for h in range(GDN_HEADS)]))
        outs, new_state = _gdn_chunks(seqs, ng_ref[...], c=seg, hg=GDN_HEADS)
        for k, b in enumerate(bs):
            for h in range(GDN_HEADS):
                sout_ref[b, h] = new_state[k][h]
                o_ref[b, :, h * GDN_DV:(h + 1) * GDN_DV] = outs[k][h]
        return carry

    lax.fori_loop(0, nb // GDN_SEQS_PER_STEP, per_group, 0)


def _gdn_sample(proj, ba, s0_all, layer, s_new_all, c0, conv_w, alog, dtb, norm_g, *, steps):
    nbatch, seg, _ = proj.shape
    cd = conv_w.shape[1]
    vw = GDN_HEADS * GDN_DV
    pad = c0.shape[1]
    nb = 8
    const2 = lambda i: (0, 0)
    return pl.pallas_call(
        functools.partial(_gdn_sample_body, nb=nb, seg=seg, steps=steps, pad=pad),
        grid=(nbatch // nb,),
        in_specs=[pl.BlockSpec((nb, seg, cd), lambda i: (i, 0, 0)),
                  pl.BlockSpec((nb, seg, vw), lambda i: (i, 0, cd // vw)),
                  pl.BlockSpec((nb, seg, 2 * LANES), lambda i: (i, 0, 0)),
                  pl.BlockSpec((1, nb, GDN_HEADS, GDN_DK, GDN_DV), lambda i: (layer, i, 0, 0, 0)),
                  pl.BlockSpec((nb, pad, cd), lambda i: (i, 0, 0)),
                  pl.BlockSpec((CONV_W, cd), const2),
                  pl.BlockSpec((1, LANES), const2),
                  pl.BlockSpec((1, LANES), const2),
                  pl.BlockSpec((1, GDN_DV), const2),
                  pl.BlockSpec(memory_space=pl.ANY)],
        out_specs=[pl.BlockSpec((nb, seg, vw), lambda i: (i, 0, 0)),
                   pl.BlockSpec((1, nb, GDN_HEADS, GDN_DK, GDN_DV), lambda i: (layer, i, 0, 0, 0)),
                   pl.BlockSpec((nb, pad, cd), lambda i: (i, 0, 0))],
        out_shape=[jax.ShapeDtypeStruct((nbatch, seg, vw), _F32),
                   jax.ShapeDtypeStruct(s_new_all.shape, _F32),
                   jax.ShapeDtypeStruct((nbatch, pad, cd), _F32)],
        scratch_shapes=[pltpu.VMEM((GDN_SEQS_PER_STEP, pad + seg, cd), _F32)],
        input_output_aliases={9: 1},
        compiler_params=_cparams(1),
        name="gdn_sample",
    )(proj, proj, ba, s0_all, c0, conv_w, alog, dtb, norm_g, s_new_all)


def _softmax_rows(s):
    e = jnp.exp(s - jnp.max(s, axis=-1, keepdims=True))
    return e / jnp.sum(e, axis=-1, keepdims=True)


def _attn_prompt_body(q_ref, k_ref, v_ref, o_ref, *, nseg, seg, ntile):
    hd = q_ref.shape[1] // XA_HEADS
    scale = hd ** -0.5
    for b in range(nseg):
        runs = [slice((t * nseg + b) * seg, (t * nseg + b + 1) * seg) for t in range(ntile)]
        heads = [slice(h * hd, (h + 1) * hd) for h in range(XA_HEADS)]
        s = [_dot_nt(jnp.concatenate([q_ref[rs, cs] for rs in runs], axis=0), k_ref[b, :, cs]) * scale
             for cs in heads]
        p = [_softmax_rows(sh).astype(_BF16) for sh in s]
        o = [_dot(ph, v_ref[b, :, cs]).astype(o_ref.dtype) for ph, cs in zip(p, heads)]
        for oh, cs in zip(o, heads):
            for t, rs in enumerate(runs):
                o_ref[rs, cs] = oh[t * seg:(t + 1) * seg]


def _attn_prompt(q, mem_k, mem_v, *, n_tiles, nseg, seg):
    ntile = math.gcd(n_tiles, 4)
    rows = nseg * seg * ntile
    d = q.shape[1]
    mem = mem_k.shape[1]
    return pl.pallas_call(
        functools.partial(_attn_prompt_body, nseg=nseg, seg=seg, ntile=ntile),
        grid=(n_tiles // ntile,),
        in_specs=[pl.BlockSpec((rows, d), lambda i: (i, 0)),
                  pl.BlockSpec((nseg, mem, d), lambda i: (0, 0, 0)),
                  pl.BlockSpec((nseg, mem, d), lambda i: (0, 0, 0))],
        out_specs=pl.BlockSpec((rows, d), lambda i: (i, 0)),
        out_shape=jax.ShapeDtypeStruct((n_tiles * nseg * seg, d), _BF16),
        compiler_params=_cparams(1),
        name="attn_prompt",
    )(q, mem_k, mem_v)


def _attn_sample_body(q_ref, k_ref, v_ref, o_ref, *, nb):
    t8 = q_ref.shape[1]
    mem, nh, hd = k_ref.shape[2:]
    scale = hd ** -0.5
    qh = lax.broadcasted_iota(jnp.int32, (nh * t8, mem * nh), 0) // t8
    kh = lax.broadcasted_iota(jnp.int32, (nh * t8, mem * nh), 1) % nh
    own = qh == kh

    def per_seq(b, carry):
        q = q_ref[b]
        q2 = jnp.concatenate([q[:, h * hd:(h + 1) * hd] for h in range(nh)], axis=0).astype(_BF16)
        k2 = k_ref[0, b].reshape(mem * nh, hd).astype(_BF16)
        v2 = v_ref[0, b].reshape(mem * nh, hd).astype(_BF16)
        s = jnp.where(own, _dot_nt(q2, k2) * scale, -jnp.inf)
        o2 = _dot(_softmax_rows(s).astype(_BF16), v2)
        for h in range(nh):
            o_ref[b, :, h * hd:(h + 1) * hd] = o2[h * t8:(h + 1) * t8]
        return carry

    lax.fori_loop(0, nb, per_seq, 0)


def _attn_sample(q, cache_k, cache_v, layer):
    nbatch, t8, d = q.shape
    mem, nh, hd = cache_k.shape[2:]
    nb = 8
    return pl.pallas_call(
        functools.partial(_attn_sample_body, nb=nb),
        grid=(nbatch // nb,),
        in_specs=[pl.BlockSpec((nb, t8, d), lambda i: (i, 0, 0)),
                  pl.BlockSpec((1, nb, mem, nh, hd), lambda i: (layer, i, 0, 0, 0)),
                  pl.BlockSpec((1, nb, mem, nh, hd), lambda i: (layer, i, 0, 0, 0))],
        out_specs=pl.BlockSpec((nb, t8, d), lambda i: (i, 0, 0)),
        out_shape=jax.ShapeDtypeStruct(q.shape, _F32),
        compiler_params=_cparams(1),
        name="attn_sample",
    )(q, cache_k, cache_v)


def kernel(x_prompt, x_sample, mem_prompt, state_lru_h, state_lru_conv, state_gdn_s, state_gdn_conv,
           cache_mem_k, cache_mem_v, ln_g, ln_b, xa_wq, xa_wk, xa_wv, xa_wo,
           lru_w_in, lru_conv_w, lru_conv_b, lru_gate_w, lru_gate_b, lru_lambda, lru_w_out,
           gdn_w_in, gdn_conv_w, gdn_a_log, gdn_dt_bias, gdn_norm_g, gdn_w_out,
           ffn_w_gu, ffn_w_down, moe_router, moe_w_gu, moe_w_down):
    depth = ln_g.shape[0]
    bp, seq, d = x_prompt.shape
    bs, ts, _ = x_sample.shape
    mem = mem_prompt.shape[1]
    alpha = (2 * depth) ** 0.25
    seg = GDN_CHUNK
    nchunk = seq // seg
    mp, ms = bp * seq, bs * ts
    assert bp * seg == ROW_TILE and ms == ROW_TILE and seq % seg == 0 and ts >= CONV_W - 1
    assert depth % 2 == 0
    t8 = 8
    pad_p = 8
    pad_s = (CONV_W - 1) * bs
    r = lru_w_in.shape[2] // 2
    cd = gdn_conv_w.shape[2]
    vw = GDN_HEADS * GDN_DV
    d_ff = ffn_w_gu.shape[2] // 2
    d_fe = moe_w_gu.shape[3] // 2

    x, xb = _pack_tokens(x_prompt.reshape(bp, nchunk, seg, d), x_sample.reshape(ms, d))

    memx = mem_prompt.reshape(bp * mem, d)
    mem_k_p = _mm(memx, xa_wk, ng=depth, tn=1024, out_dtype=_F32).reshape(depth, bp, mem, d)
    mem_v_p = _mm(memx, xa_wv, ng=depth, tn=1024, out_dtype=_F32).reshape(depth, bp, mem, d)
    mem_k_b = mem_k_p.astype(_BF16)
    mem_v_b = mem_v_p.astype(_BF16)

    ln_g3 = ln_g.reshape(depth * 3, 1, d)
    moe_w_gu3 = moe_w_gu.reshape((-1,) + moe_w_gu.shape[2:]).astype(_BF16)
    moe_w_down3 = moe_w_down.reshape((-1,) + moe_w_down.shape[2:]).astype(_BF16)
    assert mp + ms < (1 << IDX_SHIFT)
    ln_b3 = ln_b.reshape(depth * 3, 1, d)

    def to_time_major(a):
        return a.reshape(bs, ts, -1).transpose(1, 0, 2).reshape(ms, -1)

    def to_batch_major(a):
        return a.reshape(ts, bs, -1).transpose(1, 0, 2).reshape(ms, -1)

    def pad_t(a):
        a = a.reshape(bs, ts, -1)
        return jnp.pad(a, ((0, 0), (0, t8 - ts), (0, 0)))

    lru_h_p, lru_c_p, lru_h_s, lru_c_s = [], [], [], []
    gdn_s_p, gdn_c_p, gdn_c_s = [], [], []
    gdn_s_s = jnp.zeros(state_gdn_s.shape, _F32)

    for i in range(depth):
        j = i // 2
        if i % 2 == 0:
            gu = _mm(xb, lru_w_in, g0=j, tn=1024, out_dtype=_F32)[0]
            lam = lru_lambda[j].reshape(1, r)
            cb = lru_conv_b[j].reshape(1, r)
            gb = lru_gate_b[j].reshape(LRU_BLOCKS, 1, -1)
            y, h_p, c_p = _lru_seq(gu, jnp.zeros((bp, r), _F32), jnp.zeros((bp, pad_p, r), _F32),
                                   lru_conv_w[j], cb, lru_gate_w[j], gb, lam,
                                   n_tiles=nchunk, nseg=bp, seg=seg, stride=1, pad=pad_p)
            gu_s = to_time_major(gu[mp:])
            c0_s = state_lru_conv[j].transpose(1, 0, 2).reshape(1, pad_s, r)
            y_s, h_s, c_s = _lru_seq(gu_s, state_lru_h[j], c0_s, lru_conv_w[j], cb, lru_gate_w[j], gb, lam,
                                     n_tiles=1, nseg=1, seg=ms, stride=bs, pad=pad_s)
            y_s = to_batch_major(y_s)
            lru_h_p.append(h_p)
            lru_c_p.append(c_p[:, pad_p - (CONV_W - 1):, :])
            lru_h_s.append(h_s)
            lru_c_s.append(c_s.reshape(CONV_W - 1, bs, r).transpose(1, 0, 2))
            w_out, wi = lru_w_out, j
        else:
            proj = _mm(xb, gdn_w_in, g0=j, n=cd + vw, tn=1024, out_dtype=_F32)[0]
            w_tail = gdn_w_in[j, :, cd + vw:]
            w_ba = jnp.zeros((1, d, 2 * LANES), _F32)
            w_ba = w_ba.at[0, :, :GDN_HEADS].set(w_tail[:, :GDN_HEADS])
            w_ba = w_ba.at[0, :, LANES:LANES + GDN_HEADS].set(w_tail[:, GDN_HEADS:])
            ba = _mm(xb, w_ba, tn=2 * LANES, out_dtype=_F32)[0]
            alog = jnp.zeros((1, LANES), _F32).at[0, :GDN_HEADS].set(gdn_a_log[j])
            dtb = jnp.zeros((1, LANES), _F32).at[0, :GDN_HEADS].set(gdn_dt_bias[j])
            ng = gdn_norm_g[j].reshape(1, GDN_DV)
            y, s_p, c_p = _gdn_prompt(proj, ba, gdn_conv_w[j], alog, dtb, ng, n_tiles=nchunk, nseg=bp, seg=seg)
            c0_s = jnp.pad(state_gdn_conv[j], ((0, 0), (pad_p - (CONV_W - 1), 0), (0, 0)))
            y_s, gdn_s_s, c_s = _gdn_sample(pad_t(proj[mp:]), pad_t(ba[mp:]), state_gdn_s, j, gdn_s_s, c0_s,
                                            gdn_conv_w[j], alog, dtb, ng, steps=ts)
            y_s = y_s[:, :ts, :].reshape(ms, vw)
            gdn_s_p.append(s_p)
            gdn_c_p.append(c_p[:, pad_p - (CONV_W - 1):, :])
            gdn_c_s.append(c_s[:, pad_p - (CONV_W - 1):, :])
            w_out, wi = gdn_w_out, j
        x, xb, q = _mm_ln(y, y_s, w_out, wi, x, ln_g3, ln_b3, 3 * i, alpha=alpha, proj=(xa_wq, i))

        o = _attn_prompt(q, mem_k_b[i], mem_v_b[i], n_tiles=nchunk, nseg=bp, seg=seg)
        o_s = _attn_sample(pad_t(q[mp:].astype(_F32)), cache_mem_k, cache_mem_v, i)[:, :ts, :].reshape(ms, d)

        if i % 2 == 0:
            x, xb = _mm_ln(o, o_s, xa_wo, i, x, ln_g3, ln_b3, 3 * i + 1, alpha=alpha)
            hdn = _gate_up(xb, ffn_w_gu, g0=j, ng=1, f=d_ff, tf=512)[0]
            x, xb = _mm_ln(hdn, None, ffn_w_down, j, x, ln_g3, ln_b3, 3 * i + 2, alpha=alpha)
        else:
            router_pad = jnp.zeros((d, LANES), _F32).at[:, :N_EXPERTS].set(moe_router[j])
            x, xb, rt = _mm_ln(o, o_s, xa_wo, i, x, ln_g3, ln_b3, 3 * i + 1, alpha=alpha, route=router_pad)
            y2 = _moe_routed(x, _route_plan(rt, ROW_TILE), moe_w_gu3, moe_w_down3, j * N_EXPERTS, tm=ROW_TILE)
            if i < depth - 1:
                x, xb = _combine_ln(y2, rt, x, ln_g3, ln_b3, 3 * i + 2, alpha=alpha)
            else:
                y_prompt, y_sample = _combine_ln(y2, rt, x, ln_g3, ln_b3, 3 * i + 2, alpha=alpha,
                                                 prompt_shape=(bp, nchunk, seg))

    y_prompt = y_prompt.reshape(bp, seq, d)
    y_sample = y_sample.reshape(bs, ts, d)
    hd = d // XA_HEADS
    return (y_prompt, y_sample,
            jnp.stack(lru_h_p), jnp.stack(lru_c_p), jnp.stack(gdn_s_p), jnp.stack(gdn_c_p),
            mem_k_p.reshape(depth, bp, mem, XA_HEADS, hd), mem_v_p.reshape(depth, bp, mem, XA_HEADS, hd),
            jnp.stack(lru_h_s), jnp.stack(lru_c_s), gdn_s_s, jnp.stack(gdn_c_s))
```

```python
import functools
import math

import jax
import jax.numpy as jnp
from jax import lax
from jax.experimental import pallas as pl
from jax.experimental.pallas import tpu as pltpu

_F32 = jnp.float32
_BF16 = jnp.bfloat16
_HI = lax.Precision.HIGHEST

CONV_W = 4
LRU_BLOCKS = 4
LRU_C = 8.0
GDN_HEADS = 8
GDN_DK = 128
GDN_DV = 128
GDN_CHUNK = 64
GDN_SEQS_PER_STEP = 4
GDN_STACK_ROWS = 128
XA_HEADS = 4
ATTN_SEQS_PER_STEP = 4
N_EXPERTS = 8
IDX_SHIFT = 15
LN_EPS = 1e-5
RMS_EPS = 1e-6
LANES = 128
SUBLANES = 8
ROW_TILE = 512
VMEM_LIMIT_MB = 56


def _big_row_tile(m):
    return 3 * ROW_TILE if m % (3 * ROW_TILE) == 0 else min(ROW_TILE, m)


def _cparams(ndim, vmem_mb=VMEM_LIMIT_MB):
    return pltpu.CompilerParams(dimension_semantics=("arbitrary",) * ndim,
                                vmem_limit_bytes=vmem_mb * 1024 * 1024)


def _dot(a, b, precision=None):
    return jnp.dot(a, b, preferred_element_type=_F32, precision=precision)


def _bdot(a, b):
    return _dot(a.astype(_BF16), b.astype(_BF16))


def _dot_nt(a, b):
    return lax.dot_general(a, b, (((1,), (1,)), ((), ())), preferred_element_type=_F32)


def _dot_tn(a, b):
    return lax.dot_general(a, b, (((0,), (0,)), ((), ())), preferred_element_type=_F32)


def _layer_norm(v, g, b):
    mu = jnp.mean(v, axis=-1, keepdims=True)
    d = v - mu
    var = jnp.mean(d * d, axis=-1, keepdims=True)
    return d * lax.rsqrt(var + LN_EPS) * g + b


def _mm_body(x_ref, w_ref, o_ref, wb_ref):
    @pl.when(pl.program_id(2) == 0)
    def _():
        wb_ref[...] = w_ref[0].astype(_BF16)

    o_ref[0] = _dot(x_ref[...].astype(_BF16), wb_ref[...]).astype(o_ref.dtype)


def _mm(x, w, *, g0=0, ng=1, n0=0, n=None, tn, out_dtype):
    m, k = x.shape
    n = w.shape[2] if n is None else n
    tm = _big_row_tile(m)
    nb0 = n0 // tn
    return pl.pallas_call(
        _mm_body,
        grid=(ng, n // tn, m // tm),
        in_specs=[pl.BlockSpec((tm, k), lambda g, j, i: (i, 0)),
                  pl.BlockSpec((1, k, tn), lambda g, j, i: (g + g0, 0, j + nb0))],
        out_specs=pl.BlockSpec((1, tm, tn), lambda g, j, i: (g, i, j)),
        out_shape=jax.ShapeDtypeStruct((ng, m, n), out_dtype),
        scratch_shapes=[pltpu.VMEM((k, tn), _BF16)],
        compiler_params=_cparams(3),
        name="mm",
    )(x, w)


def _mm_ln_body(*refs, alpha, n1, post):
    if post is None:
        y_ref, y2_ref, w_ref, x_ref, g_ref, b_ref, o_ref, ob_ref, wb_ref = refs
    elif post == "proj":
        y_ref, y2_ref, w_ref, x_ref, g_ref, b_ref, w2_ref, o_ref, ob_ref, p_ref, wb_ref, wb2_ref = refs
    else:
        y_ref, y2_ref, w_ref, x_ref, g_ref, b_ref, r_ref, o_ref, ob_ref, p_ref, wb_ref = refs
    i = pl.program_id(0)

    @pl.when(i == 0)
    def _():
        wb_ref[...] = w_ref[0].astype(_BF16)
        if post == "proj":
            wb2_ref[...] = w2_ref[0].astype(_BF16)

    def emit(src_ref):
        mix = _dot(src_ref[...].astype(_BF16), wb_ref[...])
        res = _layer_norm(alpha * x_ref[...] + mix, g_ref[0], b_ref[0])
        o_ref[...] = res
        ob_ref[...] = res.astype(_BF16)
        if post == "proj":
            p_ref[...] = _dot(res.astype(_BF16), wb2_ref[...]).astype(p_ref.dtype)
        elif post == "route":
            p_ref[...] = _route_top2(res, r_ref[...])

    if n1 is None:
        emit(y_ref)
    else:
        pl.when(i < n1)(lambda: emit(y_ref))
        pl.when(i >= n1)(lambda: emit(y2_ref))


def _mm_ln(y, y2, w, gi, x, ln_g, ln_b, li, *, alpha, proj=None, route=None):
    m, d = x.shape
    k = y.shape[1]
    tm = min(ROW_TILE, m)
    n1 = y.shape[0] // tm
    if y2 is None:
        y2 = y
    assert n1 * tm == y.shape[0] and (n1 == m // tm or n1 * tm + y2.shape[0] == m)
    assert proj is None or route is None
    row = lambda i: (i, 0)
    in_specs = [pl.BlockSpec((tm, k), lambda i: (jnp.minimum(i, n1 - 1), 0)),
                pl.BlockSpec((tm, k), lambda i: (jnp.maximum(i - n1, 0), 0)),
                pl.BlockSpec((1, k, d), lambda i: (gi, 0, 0), pipeline_mode=pl.Buffered(1)),
                pl.BlockSpec((tm, d), row),
                pl.BlockSpec((1, 1, d), lambda i: (li, 0, 0)),
                pl.BlockSpec((1, 1, d), lambda i: (li, 0, 0))]
    args = [y, y2, w, x, ln_g, ln_b]
    out_specs = [pl.BlockSpec((tm, d), row), pl.BlockSpec((tm, d), row)]
    out_shape = [jax.ShapeDtypeStruct((m, d), _F32), jax.ShapeDtypeStruct((m, d), _BF16)]
    scratch = [pltpu.VMEM((k, d), _BF16)]
    post = None
    if proj is not None:
        post = "proj"
        w2, g2 = proj
        n2 = w2.shape[2]
        in_specs.append(pl.BlockSpec((1, d, n2), lambda i: (g2, 0, 0), pipeline_mode=pl.Buffered(1)))
        args.append(w2)
        out_specs.append(pl.BlockSpec((tm, n2), row))
        out_shape.append(jax.ShapeDtypeStruct((m, n2), _BF16))
        scratch.append(pltpu.VMEM((d, n2), _BF16))
    elif route is not None:
        post = "route"
        in_specs.append(pl.BlockSpec((d, LANES), lambda i: (0, 0)))
        args.append(route)
        out_specs.append(pl.BlockSpec((tm, LANES), row))
        out_shape.append(jax.ShapeDtypeStruct((m, LANES), _F32))
    return pl.pallas_call(
        functools.partial(_mm_ln_body, alpha=alpha, n1=None if n1 == m // tm else n1, post=post),
        grid=(m // tm,),
        in_specs=in_specs,
        out_specs=out_specs,
        out_shape=out_shape,
        scratch_shapes=scratch,
        compiler_params=_cparams(1),
        name="mm_ln",
    )(*args)


def _gu_body(x_ref, wg_ref, wu_ref, o_ref, wgb_ref, wub_ref):
    @pl.when(pl.program_id(2) == 0)
    def _():
        wgb_ref[...] = wg_ref[0].astype(_BF16)
        wub_ref[...] = wu_ref[0].astype(_BF16)

    xb = x_ref[...].astype(_BF16)
    g = _dot(xb, wgb_ref[...])
    u = _dot(xb, wub_ref[...])
    o_ref[0] = (jax.nn.silu(g) * u).astype(o_ref.dtype)


def _gate_up(x, w_gu, *, g0, ng, f, tf):
    m, k = x.shape
    tm = _big_row_tile(m)
    nf = f // tf
    return pl.pallas_call(
        _gu_body,
        grid=(ng, nf, m // tm),
        in_specs=[pl.BlockSpec((tm, k), lambda g, j, i: (i, 0)),
                  pl.BlockSpec((1, k, tf), lambda g, j, i: (g + g0, 0, j)),
                  pl.BlockSpec((1, k, tf), lambda g, j, i: (g + g0, 0, j + nf))],
        out_specs=pl.BlockSpec((1, tm, tf), lambda g, j, i: (g, i, j)),
        out_shape=jax.ShapeDtypeStruct((ng, m, f), _BF16),
        scratch_shapes=[pltpu.VMEM((k, tf), _BF16), pltpu.VMEM((k, tf), _BF16)],
        compiler_params=_cparams(3),
        name="gate_up",
    )(x, w_gu, w_gu)


def _route_top2(x, router_pad):
    x_hi = x.astype(_BF16)
    x_lo = (x - x_hi.astype(_F32)).astype(_BF16)
    r_hi = router_pad.astype(_BF16)
    r_lo = (router_pad - r_hi.astype(_F32)).astype(_BF16)
    logits = _dot(x_hi, r_hi) + (_dot(x_hi, r_lo) + _dot(x_lo, r_hi))
    lane = lax.broadcasted_iota(jnp.int32, logits.shape, 1).astype(_F32)
    l1 = jnp.where(lane < N_EXPERTS, logits, -jnp.inf)
    m1 = jnp.max(l1, axis=-1, keepdims=True)
    i1 = jnp.min(jnp.where(l1 == m1, lane, float(LANES)), axis=-1, keepdims=True)
    l2 = jnp.where(lane == i1, -jnp.inf, l1)
    m2 = jnp.max(l2, axis=-1, keepdims=True)
    i2 = jnp.min(jnp.where(l2 == m2, lane, float(LANES)), axis=-1, keepdims=True)
    e2 = jnp.exp(m2 - m1)
    den = 1.0 + e2
    return (jnp.where(lane == 0.0, i1, 0.0) + jnp.where(lane == 1.0, i2, 0.0)
            + jnp.where(lane == 2.0, 1.0 / den, 0.0) + jnp.where(lane == 3.0, e2 / den, 0.0))


def _route_plan(rt, tm):
    m = rt.shape[0]
    ef = rt[:, :2].astype(jnp.int32).reshape(-1)
    onehot = (ef[:, None] == jnp.arange(N_EXPERTS, dtype=jnp.int32)[None, :]).astype(jnp.int32)
    csum = jnp.cumsum(onehot, axis=0)
    cnt = csum[-1]
    rank = jnp.sum((csum - onehot) * onehot, axis=1)
    cnt_p = ((cnt + tm - 1) // tm) * tm
    end = jnp.cumsum(cnt_p)
    start = end - cnt_p
    dest = start[ef] + rank
    n_tiles = (2 * m) // tm + N_EXPERTS
    rows = n_tiles * tm
    pair = jnp.zeros((rows,), jnp.int32).at[dest].set(jnp.arange(2 * m, dtype=jnp.int32))
    tile_start = jnp.arange(n_tiles, dtype=jnp.int32) * tm
    texp = jnp.sum((tile_start[:, None] >= end[None, :]).astype(jnp.int32), axis=1)
    used = tile_start < end[-1]
    last = jnp.maximum(end[-1] // tm - 1, 0)
    texp = jnp.where(used, texp, texp[last])
    nval = jnp.where(used, jnp.clip(start[texp] + cnt[texp] - tile_start, 0, tm), 0)
    tile = jnp.repeat(jnp.arange(n_tiles, dtype=jnp.int32), tm)
    in_tile = jnp.arange(rows, dtype=jnp.int32) - tile * tm
    live = in_tile < jnp.repeat(nval, tm)
    tok, slot = pair // 2, pair % 2
    spare = 2 * m + (tile % 2) * tm + in_tile
    idx = jnp.where(live, tok, 0) | (jnp.where(live, slot * m + tok, spare) << IDX_SHIFT)
    n_used = (end[-1] // tm).reshape(1)
    return texp, n_used, idx


def _moe_routed_body(texp_ref, nused_ref, idx_ref, x_hbm, wg_ref, wu_ref, wd_ref, out_hbm,
                     xs_ref, ys_ref, gsem, ssem):
    i = pl.program_id(0)
    ng = xs_ref.shape[1]
    tm = ng * SUBLANES
    d = xs_ref.shape[3]
    n_used = nused_ref[0]
    slot = i % 2

    def gather(tile, s):
        def body(g, c):
            for u in range(SUBLANES):
                tok = idx_ref[tile * tm + g * SUBLANES + u] & ((1 << IDX_SHIFT) - 1)
                pltpu.make_async_copy(x_hbm.at[tok >> 3, tok & 7], xs_ref.at[s, g, u], gsem.at[s]).start()
            return c
        lax.fori_loop(0, ng, body, 0)

    def scatter(tile, s):
        def body(g, c):
            for u in range(SUBLANES):
                dst = idx_ref[tile * tm + g * SUBLANES + u] >> IDX_SHIFT
                pltpu.make_async_copy(ys_ref.at[s, g, u], out_hbm.at[dst >> 3, dst & 7], ssem.at[s]).start()
            return c
        lax.fori_loop(0, ng, body, 0)

    def wait_gather(s):
        pltpu.make_async_copy(x_hbm.at[pl.ds(0, ng)], xs_ref.at[s], gsem.at[s]).wait()

    def wait_scatter(s):
        pltpu.make_async_copy(ys_ref.at[s], out_hbm.at[pl.ds(0, ng)], ssem.at[s]).wait()

    @pl.when((i == 0) & (n_used > 0))
    def _():
        gather(0, 0)

    @pl.when(i + 1 < n_used)
    def _():
        gather(i + 1, 1 - slot)

    @pl.when(i < n_used)
    def _():
        wait_gather(slot)

        @pl.when(i >= 2)
        def _():
            wait_scatter(slot)

        xb = xs_ref[slot].reshape(tm, d).astype(_BF16)
        h = (jax.nn.silu(_dot(xb, wg_ref[0])) * _dot(xb, wu_ref[0])).astype(_BF16)
        ys_ref[slot] = _dot(h, wd_ref[0]).reshape(ng, SUBLANES, d)
        scatter(i, slot)

    @pl.when(i == pl.num_programs(0) - 1)
    def _():
        @pl.when(n_used >= 1)
        def _():
            wait_scatter((n_used - 1) % 2)

        @pl.when(n_used >= 2)
        def _():
            wait_scatter(n_used % 2)

        ys_ref[0] = jnp.zeros(ys_ref.shape[1:], _F32)
        spare0 = out_hbm.shape[0] - 2 * ng
        fills = [pltpu.make_async_copy(ys_ref.at[0], out_hbm.at[pl.ds(spare0 + k * ng, ng)], ssem.at[0])
                 for k in range(2)]
        for cp in fills:
            cp.start()
        for cp in fills:
            cp.wait()


def _moe_routed(x, plan, w_gu_bf16, w_down_bf16, e0, *, tm):
    texp, n_used, idx = plan
    m, d = x.shape
    f = w_down_bf16.shape[1]
    n_tiles = texp.shape[0]
    return pl.pallas_call(
        _moe_routed_body,
        grid_spec=pltpu.PrefetchScalarGridSpec(
            num_scalar_prefetch=3,
            grid=(n_tiles,),
            in_specs=[pl.BlockSpec(memory_space=pl.ANY),
                      pl.BlockSpec((1, d, f), lambda i, te, nu, ix: (te[i] + e0, 0, 0)),
                      pl.BlockSpec((1, d, f), lambda i, te, nu, ix: (te[i] + e0, 0, 1)),
                      pl.BlockSpec((1, f, d), lambda i, te, nu, ix: (te[i] + e0, 0, 0))],
            out_specs=pl.BlockSpec(memory_space=pl.ANY),
            scratch_shapes=[pltpu.VMEM((2, tm // SUBLANES, SUBLANES, d), _F32),
                            pltpu.VMEM((2, tm // SUBLANES, SUBLANES, d), _F32),
                            pltpu.SemaphoreType.DMA((2,)), pltpu.SemaphoreType.DMA((2,))]),
        out_shape=jax.ShapeDtypeStruct(((2 * m + 2 * tm) // SUBLANES, SUBLANES, d), _F32),
        compiler_params=_cparams(1),
        name="moe_routed",
    )(texp, n_used, idx, x.reshape(m // SUBLANES, SUBLANES, d), w_gu_bf16, w_gu_bf16,
      w_down_bf16).reshape(2 * m + 2 * tm, d)


def _combine_ln_body(a_ref, b_ref, rt_ref, x_ref, g_ref, bb_ref, *o_refs, alpha, n_prompt):
    ff = rt_ref[:, 2:3] * a_ref[...] + rt_ref[:, 3:4] * b_ref[...]
    res = _layer_norm(alpha * x_ref[...] + ff, g_ref[0], bb_ref[0])
    if n_prompt is None:
        o_refs[0][...] = res
        o_refs[1][...] = res.astype(_BF16)
    else:
        i = pl.program_id(0)
        op_ref, os_ref = o_refs

        @pl.when(i < n_prompt)
        def _():
            op_ref[...] = res.reshape(op_ref.shape)

        @pl.when(i >= n_prompt)
        def _():
            os_ref[...] = res


def _combine_ln(y2, rt, x, ln_g, ln_b, li, *, alpha, prompt_shape=None):
    m, d = x.shape
    tm = min(ROW_TILE, m)
    nt = m // tm
    if prompt_shape is None:
        n_prompt = None
        out_specs = [pl.BlockSpec((tm, d), lambda i: (i, 0)), pl.BlockSpec((tm, d), lambda i: (i, 0))]
        out_shape = [jax.ShapeDtypeStruct((m, d), _F32), jax.ShapeDtypeStruct((m, d), _BF16)]
    else:
        bp, n_prompt, seg = prompt_shape
        assert bp * seg == tm and n_prompt == nt - 1
        out_specs = [pl.BlockSpec((bp, 1, seg, d), lambda i: (0, jnp.minimum(i, n_prompt - 1), 0, 0)),
                     pl.BlockSpec((tm, d), lambda i: (0, 0))]
        out_shape = [jax.ShapeDtypeStruct((bp, n_prompt, seg, d), _F32), jax.ShapeDtypeStruct((tm, d), _F32)]
    return pl.pallas_call(
        functools.partial(_combine_ln_body, alpha=alpha, n_prompt=n_prompt),
        grid=(nt,),
        in_specs=[pl.BlockSpec((tm, d), lambda i: (i, 0)),
                  pl.BlockSpec((tm, d), lambda i: (i + nt, 0)),
                  pl.BlockSpec((tm, LANES), lambda i: (i, 0)),
                  pl.BlockSpec((tm, d), lambda i: (i, 0)),
                  pl.BlockSpec((1, 1, d), lambda i: (li, 0, 0)),
                  pl.BlockSpec((1, 1, d), lambda i: (li, 0, 0))],
        out_specs=out_specs,
        out_shape=out_shape,
        compiler_params=_cparams(1),
        name="combine_ln",
    )(y2, y2, rt, x, ln_g, ln_b)


def _pack_body(xp_ref, xs_ref, o_ref, ob_ref, *, n_prompt):
    i = pl.program_id(0)

    @pl.when(i < n_prompt)
    def _():
        o_ref[...] = xp_ref[...].reshape(o_ref.shape)
        ob_ref[...] = xp_ref[...].reshape(o_ref.shape).astype(_BF16)

    @pl.when(i >= n_prompt)
    def _():
        o_ref[...] = xs_ref[...]
        ob_ref[...] = xs_ref[...].astype(_BF16)


def _pack_tokens(x_prompt4, x_sample2):
    bp, n_prompt, seg, d = x_prompt4.shape
    tm = bp * seg
    assert x_sample2.shape == (tm, d)
    return pl.pallas_call(
        functools.partial(_pack_body, n_prompt=n_prompt),
        grid=(n_prompt + 1,),
        in_specs=[pl.BlockSpec((bp, 1, seg, d), lambda i: (0, jnp.minimum(i, n_prompt - 1), 0, 0)),
                  pl.BlockSpec((tm, d), lambda i: (0, 0))],
        out_specs=[pl.BlockSpec((tm, d), lambda i: (i, 0)), pl.BlockSpec((tm, d), lambda i: (i, 0))],
        out_shape=[jax.ShapeDtypeStruct(((n_prompt + 1) * tm, d), _F32),
                   jax.ShapeDtypeStruct(((n_prompt + 1) * tm, d), _BF16)],
        compiler_params=_cparams(1),
        name="pack_tokens",
    )(x_prompt4, x_sample2)


def _seg_scan(a, b, tstep, n_steps, stride):
    sh = stride
    for s in range(n_steps):
        keep = tstep >= (1 << s)
        a_sh = pltpu.roll(a, sh, 0)
        b_sh = pltpu.roll(b, sh, 0)
        b = jnp.where(keep, a * b_sh + b, b)
        a = jnp.where(keep, a * a_sh, a)
        sh *= 2
    return a, b


def _lru_body(gate_ref, rec_ref, h0_ref, c0_ref, cw_ref, cb_ref, gw_ref, gb_ref, lam_ref,
              y_ref, hout_ref, cout_ref, xp_ref, hc_ref, cv_ref, gwb_ref, *, nseg, seg, stride, pad):
    rows = nseg * seg
    bw = gw_ref.shape[1]

    @pl.when(pl.program_id(0) == 0)
    def _():
        xp_ref[:, 0:pad, :] = c0_ref[...]
        hc_ref[...] = h0_ref[...]
        gwb_ref[...] = gw_ref[...].astype(_BF16)

    for s in range(nseg):
        xp_ref[s, pad:pad + seg, :] = rec_ref[s * seg:(s + 1) * seg, :]
        if stride % SUBLANES == 0:
            taps = [xp_ref[s, pad - k * stride:pad - k * stride + seg, :] for k in range(CONV_W)]
        else:
            hx = xp_ref[s, 0:pad + seg, :]
            taps = [hx[pad:]] + [pltpu.roll(hx, k * stride, 0)[pad:] for k in range(1, CONV_W)]
        acc = taps[CONV_W - 1] * cw_ref[0:1, :]
        for j in range(1, CONV_W):
            acc = acc + taps[CONV_W - 1 - j] * cw_ref[j:j + 1, :]
        cv_ref[s * seg:(s + 1) * seg, :] = acc + cb_ref[...]
        tail = xp_ref[s, seg:seg + pad, :]
        xp_ref[s, 0:pad, :] = tail
        cout_ref[s] = tail

    row = lax.broadcasted_iota(jnp.int32, (rows, bw), 0)
    if stride == 1:
        tstep = row & (seg - 1)
    else:
        tstep = row >> int(math.log2(stride))
    n_steps = int(math.log2(seg // stride))

    for n in range(LRU_BLOCKS):
        cs = slice(n * bw, (n + 1) * bw)
        cv = cv_ref[:, cs]
        gates = _dot(cv.astype(_BF16), gwb_ref[n]) + gb_ref[n]
        r = jax.nn.sigmoid(gates[:, :bw])
        ig = jax.nn.sigmoid(gates[:, bw:])
        log_a = -LRU_C * r * jax.nn.softplus(-lam_ref[:, cs])
        a = jnp.exp(log_a)
        s = -jnp.tanh(log_a) * (a * a + 1.0)
        mult = jnp.where(s > 0.0, s * lax.rsqrt(s), 0.0)
        a_cum, b_cum = _seg_scan(a, mult * ig * cv, tstep, n_steps, stride)
        hc = hc_ref[:, cs]
        if stride == 1:
            hprev = jnp.concatenate([jnp.broadcast_to(hc[s:s + 1, :], (seg, bw)) for s in range(nseg)], axis=0)
        else:
            hprev = jnp.concatenate([hc] * (seg // stride), axis=0)
        h = a_cum * hprev + b_cum
        y_ref[:, cs] = (h * jax.nn.gelu(gate_ref[:, cs])).astype(y_ref.dtype)
        if stride == 1:
            hc_ref[:, cs] = jnp.concatenate([h[(s + 1) * seg - 1:(s + 1) * seg, :] for s in range(nseg)], axis=0)
        else:
            hc_ref[:, cs] = h[rows - stride:, :]

    hout_ref[...] = hc_ref[...]


def _lru_seq(gu, h0, c0, conv_w, conv_b, gate_w, gate_b, lam, *, n_tiles, nseg, seg, stride, pad):
    rows = nseg * seg
    r = gu.shape[1] // 2
    bw = r // LRU_BLOCKS
    const2 = lambda i: (0, 0)
    const3 = lambda i: (0, 0, 0)
    return pl.pallas_call(
        functools.partial(_lru_body, nseg=nseg, seg=seg, stride=stride, pad=pad),
        grid=(n_tiles,),
        in_specs=[pl.BlockSpec((rows, r), lambda i: (i, 0)),
                  pl.BlockSpec((rows, r), lambda i: (i, 1)),
                  pl.BlockSpec((nseg * stride, r), const2),
                  pl.BlockSpec((nseg, pad, r), const3),
                  pl.BlockSpec((CONV_W, r), const2),
                  pl.BlockSpec((1, r), const2),
                  pl.BlockSpec((LRU_BLOCKS, bw, 2 * bw), const3),
                  pl.BlockSpec((LRU_BLOCKS, 1, 2 * bw), const3),
                  pl.BlockSpec((1, r), const2)],
        out_specs=[pl.BlockSpec((rows, r), lambda i: (i, 0)),
                   pl.BlockSpec((nseg * stride, r), const2),
                   pl.BlockSpec((nseg, pad, r), const3)],
        out_shape=[jax.ShapeDtypeStruct((n_tiles * rows, r), _BF16),
                   jax.ShapeDtypeStruct((nseg * stride, r), _F32),
                   jax.ShapeDtypeStruct((nseg, pad, r), _F32)],
        scratch_shapes=[pltpu.VMEM((nseg, pad + seg, r), _F32),
                        pltpu.VMEM((nseg * stride, r), _F32),
                        pltpu.VMEM((rows, r), _F32),
                        pltpu.VMEM((LRU_BLOCKS, bw, 2 * bw), _BF16)],
        compiler_params=_cparams(1),
        name="lru_seq",
    )(gu, gu, h0, c0, conv_w, conv_b, gate_w, gate_b, lam)


def _l2norm(t):
    return t * lax.rsqrt(jnp.sum(t * t, axis=-1, keepdims=True) + RMS_EPS)


def _gdn_chunks(seqs, norm_g, *, c, hg):
    dk = GDN_DK
    qk_w = GDN_HEADS * dk
    r4 = hg * c
    lg_c = int(math.log2(c))
    row = lax.broadcasted_iota(jnp.int32, (r4, r4), 0)
    col = lax.broadcasted_iota(jnp.int32, (r4, r4), 1)
    blk = {1 << s: (row >> s) == (col >> s) for s in range(3, lg_c + 1)}
    incl = blk[c] & (row >= col)
    strict = blk[c] & (row > col)
    units = [(si, list(range(g0, g0 + hg))) for si in range(len(seqs)) for g0 in range(0, GDN_HEADS, hg)]
    blocks = [slice(hh * c, (hh + 1) * c) for hh in range(hg)]

    def stack(fn, hs, axis=0):
        return jnp.concatenate([fn(h) for h in hs], axis=axis)

    tr = lax.broadcasted_iota(jnp.int32, (c, c), 0)
    tc = lax.broadcasted_iota(jnp.int32, (c, c), 1)
    tri = jnp.where(tr >= tc, 1.0, 0.0)
    gc = [_dot(tri, sq[3], precision=_HI) for sq in seqs]
    gct = [jnp.concatenate([x, jnp.zeros((LANES - c, LANES), _F32)], axis=0).T for x in gc]

    q4 = [stack(lambda h: _l2norm(seqs[si][0](h * dk)) * (dk ** -0.5), hs) for si, hs in units]
    k4 = [stack(lambda h: _l2norm(seqs[si][0](qk_w + h * dk)), hs) for si, hs in units]
    v4 = [stack(lambda h: seqs[si][0](2 * qk_w + h * dk), hs) for si, hs in units]
    beta4 = [stack(lambda h: jnp.broadcast_to(seqs[si][2][:, h:h + 1], (c, dk)), hs) for si, hs in units]
    gc4 = [stack(lambda h: jnp.broadcast_to(gc[si][:, h:h + 1], (c, dk)), hs) for si, hs in units]
    gcol = [stack(lambda h: jnp.broadcast_to(gc[si][:, h:h + 1], (c, r4)), hs) for si, hs in units]
    grow = [stack(lambda h: gct[si][h:h + 1, 0:c], hs, axis=1) for si, hs in units]
    decay = [jnp.exp(jnp.where(incl, gcl - grw, -jnp.inf)) for gcl, grw in zip(gcol, grow)]
    kb4 = [k * bt for k, bt in zip(k4, beta4)]
    aq = [_dot_nt(jnp.concatenate([kb, q], axis=0).astype(_BF16), k.astype(_BF16)) for kb, q, k in zip(kb4, q4, k4)]
    a4 = [jnp.where(strict, x[:r4] * dc, 0.0) for x, dc in zip(aq, decay)]
    attn4 = [x[r4:] * dc for x, dc in zip(aq, decay)]
    c0 = min(c, 16)
    np_ = [-jnp.where(blk[c0], x, 0.0) for x in a4]
    e = list(np_)
    for _ in range(int(math.log2(c0)) - 1):
        np_ = [_bdot(x, x) for x in np_]
        e = [ei + ni + _bdot(ni, ei) for ei, ni in zip(e, np_)]
    size = c0
    while size < c:
        lo = [jnp.where(blk[2 * size] & ~blk[size], x, 0.0) for x in a4]
        m = [li + _bdot(li, ei) for li, ei in zip(lo, e)]
        e = [ei - (mi + _bdot(ei, mi)) for ei, mi in zip(e, m)]
        size *= 2
    egc4 = [jnp.exp(x) for x in gc4]
    rhs = [jnp.concatenate([v * bt, kb * eg], axis=1) for v, bt, kb, eg in zip(v4, beta4, kb4, egc4)]
    uw = [r + _bdot(ei, r) for r, ei in zip(rhs, e)]
    qg4 = [q * eg for q, eg in zip(q4, egc4)]
    ws_qs = [[_dot(jnp.concatenate([uw[u][rs, dk:], qg4[u][rs]], axis=0).astype(_BF16),
                   seqs[si][4][h].astype(_BF16)) for rs, h in zip(blocks, hs)]
             for u, (si, hs) in enumerate(units)]
    vn = [[uw[u][rs, :dk] - ws_qs[u][hh][:c] for hh, rs in enumerate(blocks)] for u in range(len(units))]
    o4 = [jnp.concatenate([x[c:] for x in ws_qs[u]], axis=0)
          + _bdot(attn4[u], jnp.concatenate(vn[u], axis=0)) for u in range(len(units))]
    outs = [[None] * GDN_HEADS for _ in seqs]
    new_state = [[None] * GDN_HEADS for _ in seqs]
    for u, (si, hs) in enumerate(units):
        z, state = seqs[si][1], seqs[si][4]
        for hh, (rs, h) in enumerate(zip(blocks, hs)):
            glast = gc[si][c - 1:c, h:h + 1]
            kd = k4[u][rs] * jnp.exp(glast - gc4[u][rs])
            new_state[si][h] = state[h] * jnp.exp(glast) + _dot_tn(kd.astype(_BF16), vn[u][hh].astype(_BF16))
            oh = o4[u][rs]
            oh = oh * lax.rsqrt(jnp.mean(oh * oh, axis=-1, keepdims=True) + RMS_EPS) * norm_g
            outs[si][h] = oh * jax.nn.silu(z[:, h * dk:(h + 1) * dk])
    return outs, new_state


def _gdn_gates(ba, alog, dtb):
    beta = jax.nn.sigmoid(ba[:, :LANES])
    g = -jnp.exp(alog) * jax.nn.softplus(ba[:, LANES:] + dtb)
    return beta, g


def _gdn_conv(xp_ref, s, x, cw_ref, *, pad, seg):
    xp_ref[s, pad:pad + seg, :] = x

    def block(col):
        cs = slice(col, col + GDN_DK)
        hx = xp_ref[s, pl.ds(0, pad + seg), cs]
        acc = pltpu.roll(hx, CONV_W - 1, 0)[pad:] * cw_ref[0:1, cs]
        for j in range(1, CONV_W - 1):
            acc = acc + pltpu.roll(hx, CONV_W - 1 - j, 0)[pad:] * cw_ref[j:j + 1, cs]
        return jax.nn.silu(acc + hx[pad:] * cw_ref[CONV_W - 1:CONV_W, cs])

    return block


def _gdn_prompt_body(qkv_ref, z_ref, ba_ref, cw_ref, alog_ref, dtb_ref, ng_ref,
                     o_ref, sout_ref, cout_ref, xp_ref, s_ref, *, nseg, seg, pad):
    @pl.when(pl.program_id(0) == 0)
    def _():
        xp_ref[...] = jnp.zeros(xp_ref.shape, _F32)
        s_ref[...] = jnp.zeros(s_ref.shape, _F32)

    def per_group(i, carry):
        bs = [i * GDN_SEQS_PER_STEP + k for k in range(GDN_SEQS_PER_STEP)]
        r0 = [pl.multiple_of(b * seg, seg) for b in bs]
        seqs = []
        for b, r in zip(bs, r0):
            qkv = _gdn_conv(xp_ref, b, qkv_ref[pl.ds(r, seg), :], cw_ref, pad=pad, seg=seg)
            beta, g = _gdn_gates(ba_ref[pl.ds(r, seg), :], alog_ref[...], dtb_ref[...])
            seqs.append((qkv, z_ref[pl.ds(r, seg), :], beta, g, [s_ref[b, h] for h in range(GDN_HEADS)]))
        outs, new_state = _gdn_chunks(seqs, ng_ref[...], c=seg, hg=GDN_STACK_ROWS // seg)
        for k, (b, r) in enumerate(zip(bs, r0)):
            tail = xp_ref[b, pl.ds(seg, pad), :]
            xp_ref[b, 0:pad, :] = tail
            cout_ref[b] = tail
            for h in range(GDN_HEADS):
                s_ref[b, h] = new_state[k][h]
                o_ref[pl.ds(r, seg), h * GDN_DV:(h + 1) * GDN_DV] = outs[k][h].astype(o_ref.dtype)
        return carry

    lax.fori_loop(0, nseg // GDN_SEQS_PER_STEP, per_group, 0)

    @pl.when(pl.program_id(0) == pl.num_programs(0) - 1)
    def _():
        sout_ref[...] = s_ref[...]


def _gdn_prompt(proj, ba, conv_w, alog, dtb, norm_g, *, n_tiles, nseg, seg):
    rows = nseg * seg
    cd = conv_w.shape[1]
    vw = GDN_HEADS * GDN_DV
    pad = 8
    const2 = lambda i: (0, 0)
    return pl.pallas_call(
        functools.partial(_gdn_prompt_body, nseg=nseg, seg=seg, pad=pad),
        grid=(n_tiles,),
        in_specs=[pl.BlockSpec((rows, cd), lambda i: (i, 0)),
                  pl.BlockSpec((rows, vw), lambda i: (i, cd // vw)),
                  pl.BlockSpec((rows, 2 * LANES), lambda i: (i, 0)),
                  pl.BlockSpec((CONV_W, cd), const2),
                  pl.BlockSpec((1, LANES), const2),
                  pl.BlockSpec((1, LANES), const2),
                  pl.BlockSpec((1, GDN_DV), const2)],
        out_specs=[pl.BlockSpec((rows, vw), lambda i: (i, 0)),
                   pl.BlockSpec((nseg, GDN_HEADS, GDN_DK, GDN_DV), lambda i: (0, 0, 0, 0)),
                   pl.BlockSpec((nseg, pad, cd), lambda i: (0, 0, 0))],
        out_shape=[jax.ShapeDtypeStruct((n_tiles * rows, vw), _BF16),
                   jax.ShapeDtypeStruct((nseg, GDN_HEADS, GDN_DK, GDN_DV), _F32),
                   jax.ShapeDtypeStruct((nseg, pad, cd), _F32)],
        scratch_shapes=[pltpu.VMEM((nseg, pad + seg, cd), _F32),
                        pltpu.VMEM((nseg, GDN_HEADS, GDN_DK, GDN_DV), _F32)],
        compiler_params=_cparams(1),
        name="gdn_prompt",
    )(proj, proj, ba, conv_w, alog, dtb, norm_g)


def _gdn_sample_body(qkv_ref, z_ref, ba_ref, s0_ref, c0_ref, cw_ref, alog_ref, dtb_ref, ng_ref, sall_ref,
                     o_ref, sout_ref, cout_ref, xp_ref, *, nb, seg, steps, pad):
    del sall_ref
    s0_ref, sout_ref = s0_ref.at[0], sout_ref.at[0]
    def per_group(i, carry):
        bs = [i * GDN_SEQS_PER_STEP + k for k in range(GDN_SEQS_PER_STEP)]
        seqs = []
        for k, b in enumerate(bs):
            xp_ref[k, 0:pad, :] = c0_ref[b]
            qkv = _gdn_conv(xp_ref, k, qkv_ref[b], cw_ref, pad=pad, seg=seg)
            cout_ref[b] = xp_ref[k, pl.ds(steps, pad), :]
            beta, g = _gdn_gates(ba_ref[b], alog_ref[...], dtb_ref[...])
            live = lax.broadcasted_iota(jnp.int32, beta.shape, 0) < steps
            seqs.append((qkv, z_ref[b], jnp.where(live, beta, 0.0), jnp.where(live, g, 0.0),
                         [s0_ref[b, h] for h in range(GDN_HEADS)]))
        outs, new_state = _gdn_chunks(seqs, ng_ref[...], c=seg, hg=GDN_HEADS)
        for k, b in enumerate(bs):
            for h in range(GDN_HEADS):
                sout_ref[b, h] = new_state[k][h]
                o_ref[b, :, h * GDN_DV:(h + 1) * GDN_DV] = outs[k][h]
        return carry

    lax.fori_loop(0, nb // GDN_SEQS_PER_STEP, per_group, 0)


def _gdn_sample(proj, ba, s0_all, layer, s_new_all, c0, conv_w, alog, dtb, norm_g, *, steps):
    nbatch, seg, _ = proj.shape
    cd = conv_w.shape[1]
    vw = GDN_HEADS * GDN_DV
    pad = c0.shape[1]
    nb = 8
    const2 = lambda i: (0, 0)
    return pl.pallas_call(
        functools.partial(_gdn_sample_body, nb=nb, seg=seg, steps=steps, pad=pad),
        grid=(nbatch // nb,),
        in_specs=[pl.BlockSpec((nb, seg, cd), lambda i: (i, 0, 0)),
                  pl.BlockSpec((nb, seg, vw), lambda i: (i, 0, cd // vw)),
                  pl.BlockSpec((nb, seg, 2 * LANES), lambda i: (i, 0, 0)),
                  pl.BlockSpec((1, nb, GDN_HEADS, GDN_DK, GDN_DV), lambda i: (layer, i, 0, 0, 0)),
                  pl.BlockSpec((nb, pad, cd), lambda i: (i, 0, 0)),
                  pl.BlockSpec((CONV_W, cd), const2),
                  pl.BlockSpec((1, LANES), const2),
                  pl.BlockSpec((1, LANES), const2),
                  pl.BlockSpec((1, GDN_DV), const2),
                  pl.BlockSpec(memory_space=pl.ANY)],
        out_specs=[pl.BlockSpec((nb, seg, vw), lambda i: (i, 0, 0)),
                   pl.BlockSpec((1, nb, GDN_HEADS, GDN_DK, GDN_DV), lambda i: (layer, i, 0, 0, 0)),
                   pl.BlockSpec((nb, pad, cd), lambda i: (i, 0, 0))],
        out_shape=[jax.ShapeDtypeStruct((nbatch, seg, vw), _F32),
                   jax.ShapeDtypeStruct(s_new_all.shape, _F32),
                   jax.ShapeDtypeStruct((nbatch, pad, cd), _F32)],
        scratch_shapes=[pltpu.VMEM((GDN_SEQS_PER_STEP, pad + seg, cd), _F32)],
        input_output_aliases={9: 1},
        compiler_params=_cparams(1),
        name="gdn_sample",
    )(proj, proj, ba, s0_all, c0, conv_w, alog, dtb, norm_g, s_new_all)


def _softmax_rows(s):
    e = jnp.exp(s - jnp.max(s, axis=-1, keepdims=True))
    return e / jnp.sum(e, axis=-1, keepdims=True)


def _attn_prompt_body(q_ref, k_ref, v_ref, o_ref, *, nseg, seg, ntile):
    hd = q_ref.shape[1] // XA_HEADS
    scale = hd ** -0.5
    for b in range(nseg):
        runs = [slice((t * nseg + b) * seg, (t * nseg + b + 1) * seg) for t in range(ntile)]
        heads = [slice(h * hd, (h + 1) * hd) for h in range(XA_HEADS)]
        s = [_dot_nt(jnp.concatenate([q_ref[rs, cs] for rs in runs], axis=0), k_ref[b, :, cs]) * scale
             for cs in heads]
        p = [_softmax_rows(sh).astype(_BF16) for sh in s]
        o = [_dot(ph, v_ref[b, :, cs]).astype(o_ref.dtype) for ph, cs in zip(p, heads)]
        for oh, cs in zip(o, heads):
            for t, rs in enumerate(runs):
                o_ref[rs, cs] = oh[t * seg:(t + 1) * seg]


def _attn_prompt(q, mem_k, mem_v, *, n_tiles, nseg, seg):
    ntile = math.gcd(n_tiles, 4)
    rows = nseg * seg * ntile
    d = q.shape[1]
    mem = mem_k.shape[1]
    return pl.pallas_call(
        functools.partial(_attn_prompt_body, nseg=nseg, seg=seg, ntile=ntile),
        grid=(n_tiles // ntile,),
        in_specs=[pl.BlockSpec((rows, d), lambda i: (i, 0)),
                  pl.BlockSpec((nseg, mem, d), lambda i: (0, 0, 0)),
                  pl.BlockSpec((nseg, mem, d), lambda i: (0, 0, 0))],
        out_specs=pl.BlockSpec((rows, d), lambda i: (i, 0)),
        out_shape=jax.ShapeDtypeStruct((n_tiles * nseg * seg, d), _BF16),
        compiler_params=_cparams(1),
        name="attn_prompt",
    )(q, mem_k, mem_v)


def _attn_sample_body(q_ref, k_ref, v_ref, o_ref, *, nb):
    t8 = q_ref.shape[1]
    mem, nh, hd = k_ref.shape[2:]
    scale = hd ** -0.5
    qh = lax.broadcasted_iota(jnp.int32, (nh * t8, mem * nh), 0) // t8
    kh = lax.broadcasted_iota(jnp.int32, (nh * t8, mem * nh), 1) % nh
    own = qh == kh

    def per_group(i, carry):
        bs = [i * ATTN_SEQS_PER_STEP + k for k in range(ATTN_SEQS_PER_STEP)]
        q2 = [jnp.concatenate([q_ref[b][:, h * hd:(h + 1) * hd] for h in range(nh)], axis=0).astype(_BF16)
              for b in bs]
        s = [jnp.where(own, _dot_nt(qq, k_ref[0, b].reshape(mem * nh, hd).astype(_BF16)) * scale, -jnp.inf)
             for qq, b in zip(q2, bs)]
        p = [_softmax_rows(x).astype(_BF16) for x in s]
        o2 = [_dot(pp, v_ref[0, b].reshape(mem * nh, hd).astype(_BF16)) for pp, b in zip(p, bs)]
        for oo, b in zip(o2, bs):
            for h in range(nh):
                o_ref[b, :, h * hd:(h + 1) * hd] = oo[h * t8:(h + 1) * t8]
        return carry

    lax.fori_loop(0, nb // ATTN_SEQS_PER_STEP, per_group, 0)


def _attn_sample(q, cache_k, cache_v, layer):
    nbatch, t8, d = q.shape
    mem, nh, hd = cache_k.shape[2:]
    nb = 8
    return pl.pallas_call(
        functools.partial(_attn_sample_body, nb=nb),
        grid=(nbatch // nb,),
        in_specs=[pl.BlockSpec((nb, t8, d), lambda i: (i, 0, 0)),
                  pl.BlockSpec((1, nb, mem, nh, hd), lambda i: (layer, i, 0, 0, 0)),
                  pl.BlockSpec((1, nb, mem, nh, hd), lambda i: (layer, i, 0, 0, 0))],
        out_specs=pl.BlockSpec((nb, t8, d), lambda i: (i, 0, 0)),
        out_shape=jax.ShapeDtypeStruct(q.shape, _F32),
        compiler_params=_cparams(1),
        name="attn_sample",
    )(q, cache_k, cache_v)


def kernel(x_prompt, x_sample, mem_prompt, state_lru_h, state_lru_conv, state_gdn_s, state_gdn_conv,
           cache_mem_k, cache_mem_v, ln_g, ln_b, xa_wq, xa_wk, xa_wv, xa_wo,
           lru_w_in, lru_conv_w, lru_conv_b, lru_gate_w, lru_gate_b, lru_lambda, lru_w_out,
           gdn_w_in, gdn_conv_w, gdn_a_log, gdn_dt_bias, gdn_norm_g, gdn_w_out,
           ffn_w_gu, ffn_w_down, moe_router, moe_w_gu, moe_w_down):
    depth = ln_g.shape[0]
    bp, seq, d = x_prompt.shape
    bs, ts, _ = x_sample.shape
    mem = mem_prompt.shape[1]
    alpha = (2 * depth) ** 0.25
    seg = GDN_CHUNK
    nchunk = seq // seg
    mp, ms = bp * seq, bs * ts
    assert bp * seg == ROW_TILE and ms == ROW_TILE and seq % seg == 0 and ts >= CONV_W - 1
    assert depth % 2 == 0
    t8 = 8
    pad_p = 8
    pad_s = (CONV_W - 1) * bs
    r = lru_w_in.shape[2] // 2
    cd = gdn_conv_w.shape[2]
    vw = GDN_HEADS * GDN_DV
    d_ff = ffn_w_gu.shape[2] // 2
    d_fe = moe_w_gu.shape[3] // 2

    x, xb = _pack_tokens(x_prompt.reshape(bp, nchunk, seg, d), x_sample.reshape(ms, d))

    memx = mem_prompt.reshape(bp * mem, d)
    mem_k_p = _mm(memx, xa_wk, ng=depth, tn=1024, out_dtype=_F32).reshape(depth, bp, mem, d)
    mem_v_p = _mm(memx, xa_wv, ng=depth, tn=1024, out_dtype=_F32).reshape(depth, bp, mem, d)
    mem_k_b = mem_k_p.astype(_BF16)
    mem_v_b = mem_v_p.astype(_BF16)

    ln_g3 = ln_g.reshape(depth * 3, 1, d)
    moe_w_gu3 = moe_w_gu.reshape((-1,) + moe_w_gu.shape[2:]).astype(_BF16)
    moe_w_down3 = moe_w_down.reshape((-1,) + moe_w_down.shape[2:]).astype(_BF16)
    assert mp + ms < (1 << IDX_SHIFT)
    ln_b3 = ln_b.reshape(depth * 3, 1, d)

    def to_time_major(a):
        return a.reshape(bs, ts, -1).transpose(1, 0, 2).reshape(ms, -1)

    def to_batch_major(a):
        return a.reshape(ts, bs, -1).transpose(1, 0, 2).reshape(ms, -1)

    def pad_t(a):
        a = a.reshape(bs, ts, -1)
        return jnp.pad(a, ((0, 0), (0, t8 - ts), (0, 0)))

    lru_h_p, lru_c_p, lru_h_s, lru_c_s = [], [], [], []
    gdn_s_p, gdn_c_p, gdn_c_s = [], [], []
    gdn_s_s = jnp.zeros(state_gdn_s.shape, _F32)

    for i in range(depth):
        j = i // 2
        if i % 2 == 0:
            gu = _mm(xb, lru_w_in, g0=j, tn=1024, out_dtype=_F32)[0]
            lam = lru_lambda[j].reshape(1, r)
            cb = lru_conv_b[j].reshape(1, r)
            gb = lru_gate_b[j].reshape(LRU_BLOCKS, 1, -1)
            y, h_p, c_p = _lru_seq(gu, jnp.zeros((bp, r), _F32), jnp.zeros((bp, pad_p, r), _F32),
                                   lru_conv_w[j], cb, lru_gate_w[j], gb, lam,
                                   n_tiles=nchunk, nseg=bp, seg=seg, stride=1, pad=pad_p)
            gu_s = to_time_major(gu[mp:])
            c0_s = state_lru_conv[j].transpose(1, 0, 2).reshape(1, pad_s, r)
            y_s, h_s, c_s = _lru_seq(gu_s, state_lru_h[j], c0_s, lru_conv_w[j], cb, lru_gate_w[j], gb, lam,
                                     n_tiles=1, nseg=1, seg=ms, stride=bs, pad=pad_s)
            y_s = to_batch_major(y_s)
            lru_h_p.append(h_p)
            lru_c_p.append(c_p[:, pad_p - (CONV_W - 1):, :])
            lru_h_s.append(h_s)
            lru_c_s.append(c_s.reshape(CONV_W - 1, bs, r).transpose(1, 0, 2))
            w_out, wi = lru_w_out, j
        else:
            proj = _mm(xb, gdn_w_in, g0=j, n=cd + vw, tn=1024, out_dtype=_F32)[0]
            w_tail = gdn_w_in[j, :, cd + vw:]
            w_ba = jnp.zeros((1, d, 2 * LANES), _F32)
            w_ba = w_ba.at[0, :, :GDN_HEADS].set(w_tail[:, :GDN_HEADS])
            w_ba = w_ba.at[0, :, LANES:LANES + GDN_HEADS].set(w_tail[:, GDN_HEADS:])
            ba = _mm(xb, w_ba, tn=2 * LANES, out_dtype=_F32)[0]
            alog = jnp.zeros((1, LANES), _F32).at[0, :GDN_HEADS].set(gdn_a_log[j])
            dtb = jnp.zeros((1, LANES), _F32).at[0, :GDN_HEADS].set(gdn_dt_bias[j])
            ng = gdn_norm_g[j].reshape(1, GDN_DV)
            y, s_p, c_p = _gdn_prompt(proj, ba, gdn_conv_w[j], alog, dtb, ng, n_tiles=nchunk, nseg=bp, seg=seg)
            c0_s = jnp.pad(state_gdn_conv[j], ((0, 0), (pad_p - (CONV_W - 1), 0), (0, 0)))
            y_s, gdn_s_s, c_s = _gdn_sample(pad_t(proj[mp:]), pad_t(ba[mp:]), state_gdn_s, j, gdn_s_s, c0_s,
                                            gdn_conv_w[j], alog, dtb, ng, steps=ts)
            y_s = y_s[:, :ts, :].reshape(ms, vw)
            gdn_s_p.append(s_p)
            gdn_c_p.append(c_p[:, pad_p - (CONV_W - 1):, :])
            gdn_c_s.append(c_s[:, pad_p - (CONV_W - 1):, :])
            w_out, wi = gdn_w_out, j
        x, xb, q = _mm_ln(y, y_s, w_out, wi, x, ln_g3, ln_b3, 3 * i, alpha=alpha, proj=(xa_wq, i))

        o = _attn_prompt(q, mem_k_b[i], mem_v_b[i], n_tiles=nchunk, nseg=bp, seg=seg)
        o_s = _attn_sample(pad_t(q[mp:].astype(_F32)), cache_mem_k, cache_mem_v, i)[:, :ts, :].reshape(ms, d)

        if i % 2 == 0:
            x, xb = _mm_ln(o, o_s, xa_wo, i, x, ln_g3, ln_b3, 3 * i + 1, alpha=alpha)
            hdn = _gate_up(xb, ffn_w_gu, g0=j, ng=1, f=d_ff, tf=512)[0]
            x, xb = _mm_ln(hdn, None, ffn_w_down, j, x, ln_g3, ln_b3, 3 * i + 2, alpha=alpha)
        else:
            router_pad = jnp.zeros((d, LANES), _F32).at[:, :N_EXPERTS].set(moe_router[j])
            x, xb, rt = _mm_ln(o, o_s, xa_wo, i, x, ln_g3, ln_b3, 3 * i + 1, alpha=alpha, route=router_pad)
            y2 = _moe_routed(x, _route_plan(rt, ROW_TILE), moe_w_gu3, moe_w_down3, j * N_EXPERTS, tm=ROW_TILE)
            if i < depth - 1:
                x, xb = _combine_ln(y2, rt, x, ln_g3, ln_b3, 3 * i + 2, alpha=alpha)
            else:
                y_prompt, y_sample = _combine_ln(y2, rt, x, ln_g3, ln_b3, 3 * i + 2, alpha=alpha,
                                                 prompt_shape=(bp, nchunk, seg))

    y_prompt = y_prompt.reshape(bp, seq, d)
    y_sample = y_sample.reshape(bs, ts, d)
    hd = d // XA_HEADS
    return (y_prompt, y_sample,
            jnp.stack(lru_h_p), jnp.stack(lru_c_p), jnp.stack(gdn_s_p), jnp.stack(gdn_c_p),
            mem_k_p.reshape(depth, bp, mem, XA_HEADS, hd), mem_v_p.reshape(depth, bp, mem, XA_HEADS, hd),
            jnp.stack(lru_h_s), jnp.stack(lru_c_s), gdn_s_s, jnp.stack(gdn_c_s))
```

```python
import functools
import math

import jax
import jax.numpy as jnp
from jax import lax
from jax.experimental import pallas as pl
from jax.experimental.pallas import tpu as pltpu

_F32 = jnp.float32
_BF16 = jnp.bfloat16
_HI = lax.Precision.HIGHEST

CONV_W = 4
LRU_BLOCKS = 4
LRU_C = 8.0
GDN_HEADS = 8
GDN_DK = 128
GDN_DV = 128
GDN_CHUNK = 64
GDN_SEQS_PER_STEP = 4
GDN_STACK_ROWS = 128
XA_HEADS = 4
ATTN_SEQS_PER_STEP = 4
N_EXPERTS = 8
IDX_SHIFT = 15
LN_EPS = 1e-5
RMS_EPS = 1e-6
LANES = 128
SUBLANES = 8
ROW_TILE = 512
MM_LN_SPLIT = 2
VMEM_LIMIT_MB = 56


def _big_row_tile(m):
    return 3 * ROW_TILE if m % (3 * ROW_TILE) == 0 else min(ROW_TILE, m)


def _cparams(ndim, vmem_mb=VMEM_LIMIT_MB):
    return pltpu.CompilerParams(dimension_semantics=("arbitrary",) * ndim,
                                vmem_limit_bytes=vmem_mb * 1024 * 1024)


def _dot(a, b, precision=None):
    return jnp.dot(a, b, preferred_element_type=_F32, precision=precision)


def _bdot(a, b):
    return _dot(a.astype(_BF16), b.astype(_BF16))


def _dot_nt(a, b):
    return lax.dot_general(a, b, (((1,), (1,)), ((), ())), preferred_element_type=_F32)


def _dot_tn(a, b):
    return lax.dot_general(a, b, (((0,), (0,)), ((), ())), preferred_element_type=_F32)


def _layer_norm(v, g, b):
    mu = jnp.mean(v, axis=-1, keepdims=True)
    d = v - mu
    var = jnp.mean(d * d, axis=-1, keepdims=True)
    return d * lax.rsqrt(var + LN_EPS) * g + b


def _mm_body(x_ref, w_ref, o_ref, wb_ref):
    @pl.when(pl.program_id(2) == 0)
    def _():
        wb_ref[...] = w_ref[0].astype(_BF16)

    o_ref[0] = _dot(x_ref[...].astype(_BF16), wb_ref[...]).astype(o_ref.dtype)


def _mm(x, w, *, g0=0, ng=1, n0=0, n=None, tn, out_dtype):
    m, k = x.shape
    n = w.shape[2] if n is None else n
    tm = _big_row_tile(m)
    nb0 = n0 // tn
    return pl.pallas_call(
        _mm_body,
        grid=(ng, n // tn, m // tm),
        in_specs=[pl.BlockSpec((tm, k), lambda g, j, i: (i, 0)),
                  pl.BlockSpec((1, k, tn), lambda g, j, i: (g + g0, 0, j + nb0))],
        out_specs=pl.BlockSpec((1, tm, tn), lambda g, j, i: (g, i, j)),
        out_shape=jax.ShapeDtypeStruct((ng, m, n), out_dtype),
        scratch_shapes=[pltpu.VMEM((k, tn), _BF16)],
        compiler_params=_cparams(3),
        name="mm",
    )(x, w)


def _mm_ln_body(*refs, alpha, n1, post):
    if post is None:
        y_ref, y2_ref, w_ref, x_ref, g_ref, b_ref, o_ref, ob_ref, wb_ref = refs
    elif post == "proj":
        y_ref, y2_ref, w_ref, x_ref, g_ref, b_ref, w2_ref, o_ref, ob_ref, p_ref, wb_ref, wb2_ref = refs
    else:
        y_ref, y2_ref, w_ref, x_ref, g_ref, b_ref, r_ref, o_ref, ob_ref, p_ref, wb_ref = refs
    i = pl.program_id(0)

    @pl.when(i == 0)
    def _():
        wb_ref[...] = w_ref[0].astype(_BF16)
        if post == "proj":
            wb2_ref[...] = w2_ref[0].astype(_BF16)

    def emit(src_ref):
        tm = o_ref.shape[0]
        blocks = [slice(k * tm // MM_LN_SPLIT, (k + 1) * tm // MM_LN_SPLIT) for k in range(MM_LN_SPLIT)]
        mix = [_dot(src_ref[rs, :].astype(_BF16), wb_ref[...]) for rs in blocks]
        res = [_layer_norm(alpha * x_ref[rs, :] + m, g_ref[0], b_ref[0]) for rs, m in zip(blocks, mix)]
        for rs, r in zip(blocks, res):
            o_ref[rs, :] = r
            ob_ref[rs, :] = r.astype(_BF16)
        if post == "proj":
            for rs, r in zip(blocks, res):
                p_ref[rs, :] = _dot(r.astype(_BF16), wb2_ref[...]).astype(p_ref.dtype)
        elif post == "route":
            for rs, r in zip(blocks, res):
                p_ref[rs, :] = _route_top2(r, r_ref[...])

    if n1 is None:
        emit(y_ref)
    else:
        pl.when(i < n1)(lambda: emit(y_ref))
        pl.when(i >= n1)(lambda: emit(y2_ref))


def _mm_ln(y, y2, w, gi, x, ln_g, ln_b, li, *, alpha, proj=None, route=None):
    m, d = x.shape
    k = y.shape[1]
    tm = min(ROW_TILE, m)
    n1 = y.shape[0] // tm
    if y2 is None:
        y2 = y
    assert n1 * tm == y.shape[0] and (n1 == m // tm or n1 * tm + y2.shape[0] == m)
    assert proj is None or route is None
    row = lambda i: (i, 0)
    in_specs = [pl.BlockSpec((tm, k), lambda i: (jnp.minimum(i, n1 - 1), 0)),
                pl.BlockSpec((tm, k), lambda i: (jnp.maximum(i - n1, 0), 0)),
                pl.BlockSpec((1, k, d), lambda i: (gi, 0, 0), pipeline_mode=pl.Buffered(1)),
                pl.BlockSpec((tm, d), row),
                pl.BlockSpec((1, 1, d), lambda i: (li, 0, 0)),
                pl.BlockSpec((1, 1, d), lambda i: (li, 0, 0))]
    args = [y, y2, w, x, ln_g, ln_b]
    out_specs = [pl.BlockSpec((tm, d), row), pl.BlockSpec((tm, d), row)]
    out_shape = [jax.ShapeDtypeStruct((m, d), _F32), jax.ShapeDtypeStruct((m, d), _BF16)]
    scratch = [pltpu.VMEM((k, d), _BF16)]
    post = None
    if proj is not None:
        post = "proj"
        w2, g2 = proj
        n2 = w2.shape[2]
        in_specs.append(pl.BlockSpec((1, d, n2), lambda i: (g2, 0, 0), pipeline_mode=pl.Buffered(1)))
        args.append(w2)
        out_specs.append(pl.BlockSpec((tm, n2), row))
        out_shape.append(jax.ShapeDtypeStruct((m, n2), _BF16))
        scratch.append(pltpu.VMEM((d, n2), _BF16))
    elif route is not None:
        post = "route"
        in_specs.append(pl.BlockSpec((d, LANES), lambda i: (0, 0)))
        args.append(route)
        out_specs.append(pl.BlockSpec((tm, LANES), row))
        out_shape.append(jax.ShapeDtypeStruct((m, LANES), _F32))
    return pl.pallas_call(
        functools.partial(_mm_ln_body, alpha=alpha, n1=None if n1 == m // tm else n1, post=post),
        grid=(m // tm,),
        in_specs=in_specs,
        out_specs=out_specs,
        out_shape=out_shape,
        scratch_shapes=scratch,
        compiler_params=_cparams(1),
        name="mm_ln",
    )(*args)


def _gu_body(x_ref, wg_ref, wu_ref, o_ref, wgb_ref, wub_ref):
    @pl.when(pl.program_id(2) == 0)
    def _():
        wgb_ref[...] = wg_ref[0].astype(_BF16)
        wub_ref[...] = wu_ref[0].astype(_BF16)

    xb = x_ref[...].astype(_BF16)
    g = _dot(xb, wgb_ref[...])
    u = _dot(xb, wub_ref[...])
    o_ref[0] = (jax.nn.silu(g) * u).astype(o_ref.dtype)


def _gate_up(x, w_gu, *, g0, ng, f, tf):
    m, k = x.shape
    tm = _big_row_tile(m)
    nf = f // tf
    return pl.pallas_call(
        _gu_body,
        grid=(ng, nf, m // tm),
        in_specs=[pl.BlockSpec((tm, k), lambda g, j, i: (i, 0)),
                  pl.BlockSpec((1, k, tf), lambda g, j, i: (g + g0, 0, j)),
                  pl.BlockSpec((1, k, tf), lambda g, j, i: (g + g0, 0, j + nf))],
        out_specs=pl.BlockSpec((1, tm, tf), lambda g, j, i: (g, i, j)),
        out_shape=jax.ShapeDtypeStruct((ng, m, f), _BF16),
        scratch_shapes=[pltpu.VMEM((k, tf), _BF16), pltpu.VMEM((k, tf), _BF16)],
        compiler_params=_cparams(3),
        name="gate_up",
    )(x, w_gu, w_gu)


def _route_top2(x, router_pad):
    x_hi = x.astype(_BF16)
    x_lo = (x - x_hi.astype(_F32)).astype(_BF16)
    r_hi = router_pad.astype(_BF16)
    r_lo = (router_pad - r_hi.astype(_F32)).astype(_BF16)
    logits = _dot(x_hi, r_hi) + (_dot(x_hi, r_lo) + _dot(x_lo, r_hi))
    lane = lax.broadcasted_iota(jnp.int32, logits.shape, 1).astype(_F32)
    l1 = jnp.where(lane < N_EXPERTS, logits, -jnp.inf)
    m1 = jnp.max(l1, axis=-1, keepdims=True)
    i1 = jnp.min(jnp.where(l1 == m1, lane, float(LANES)), axis=-1, keepdims=True)
    l2 = jnp.where(lane == i1, -jnp.inf, l1)
    m2 = jnp.max(l2, axis=-1, keepdims=True)
    i2 = jnp.min(jnp.where(l2 == m2, lane, float(LANES)), axis=-1, keepdims=True)
    e2 = jnp.exp(m2 - m1)
    den = 1.0 + e2
    return (jnp.where(lane == 0.0, i1, 0.0) + jnp.where(lane == 1.0, i2, 0.0)
            + jnp.where(lane == 2.0, 1.0 / den, 0.0) + jnp.where(lane == 3.0, e2 / den, 0.0))


def _route_plan(rt, tm):
    m = rt.shape[0]
    ef = rt[:, :2].astype(jnp.int32).reshape(-1)
    onehot = (ef[:, None] == jnp.arange(N_EXPERTS, dtype=jnp.int32)[None, :]).astype(jnp.int32)
    csum = jnp.cumsum(onehot, axis=0)
    cnt = csum[-1]
    rank = jnp.sum((csum - onehot) * onehot, axis=1)
    cnt_p = ((cnt + tm - 1) // tm) * tm
    end = jnp.cumsum(cnt_p)
    start = end - cnt_p
    dest = start[ef] + rank
    n_tiles = (2 * m) // tm + N_EXPERTS
    rows = n_tiles * tm
    pair = jnp.zeros((rows,), jnp.int32).at[dest].set(jnp.arange(2 * m, dtype=jnp.int32))
    tile_start = jnp.arange(n_tiles, dtype=jnp.int32) * tm
    texp = jnp.sum((tile_start[:, None] >= end[None, :]).astype(jnp.int32), axis=1)
    used = tile_start < end[-1]
    last = jnp.maximum(end[-1] // tm - 1, 0)
    texp = jnp.where(used, texp, texp[last])
    nval = jnp.where(used, jnp.clip(start[texp] + cnt[texp] - tile_start, 0, tm), 0)
    tile = jnp.repeat(jnp.arange(n_tiles, dtype=jnp.int32), tm)
    in_tile = jnp.arange(rows, dtype=jnp.int32) - tile * tm
    live = in_tile < jnp.repeat(nval, tm)
    tok, slot = pair // 2, pair % 2
    spare = 2 * m + (tile % 2) * tm + in_tile
    idx = jnp.where(live, tok, 0) | (jnp.where(live, slot * m + tok, spare) << IDX_SHIFT)
    n_used = (end[-1] // tm).reshape(1)
    return texp, n_used, idx


def _moe_routed_body(texp_ref, nused_ref, idx_ref, x_hbm, wg_ref, wu_ref, wd_ref, out_hbm,
                     xs_ref, ys_ref, wgb_ref, wub_ref, wdb_ref, gsem, ssem):
    i = pl.program_id(0)
    ng = xs_ref.shape[1]
    tm = ng * SUBLANES
    d = xs_ref.shape[3]
    n_used = nused_ref[0]
    slot = i % 2

    def gather(tile, s):
        def body(g, c):
            for u in range(SUBLANES):
                tok = idx_ref[tile * tm + g * SUBLANES + u] & ((1 << IDX_SHIFT) - 1)
                pltpu.make_async_copy(x_hbm.at[tok >> 3, tok & 7], xs_ref.at[s, g, u], gsem.at[s]).start()
            return c
        lax.fori_loop(0, ng, body, 0)

    def scatter(tile, s):
        def body(g, c):
            for u in range(SUBLANES):
                dst = idx_ref[tile * tm + g * SUBLANES + u] >> IDX_SHIFT
                pltpu.make_async_copy(ys_ref.at[s, g, u], out_hbm.at[dst >> 3, dst & 7], ssem.at[s]).start()
            return c
        lax.fori_loop(0, ng, body, 0)

    def wait_gather(s):
        pltpu.make_async_copy(x_hbm.at[pl.ds(0, ng)], xs_ref.at[s], gsem.at[s]).wait()

    def wait_scatter(s):
        pltpu.make_async_copy(ys_ref.at[s], out_hbm.at[pl.ds(0, ng)], ssem.at[s]).wait()

    @pl.when((i == 0) & (n_used > 0))
    def _():
        gather(0, 0)

    @pl.when(i + 1 < n_used)
    def _():
        gather(i + 1, 1 - slot)

    @pl.when(i < n_used)
    def _():
        wait_gather(slot)

        @pl.when(i >= 2)
        def _():
            wait_scatter(slot)

        @pl.when((i == 0) | (texp_ref[i] != texp_ref[jnp.maximum(i - 1, 0)]))
        def _():
            wgb_ref[...] = wg_ref[0].astype(_BF16)
            wub_ref[...] = wu_ref[0].astype(_BF16)
            wdb_ref[...] = wd_ref[0].astype(_BF16)

        xb = xs_ref[slot].reshape(tm, d).astype(_BF16)
        h = (jax.nn.silu(_dot(xb, wgb_ref[...])) * _dot(xb, wub_ref[...])).astype(_BF16)
        ys_ref[slot] = _dot(h, wdb_ref[...]).reshape(ng, SUBLANES, d)
        scatter(i, slot)

    @pl.when(i == pl.num_programs(0) - 1)
    def _():
        @pl.when(n_used >= 1)
        def _():
            wait_scatter((n_used - 1) % 2)

        @pl.when(n_used >= 2)
        def _():
            wait_scatter(n_used % 2)

        ys_ref[0] = jnp.zeros(ys_ref.shape[1:], _F32)
        spare0 = out_hbm.shape[0] - 2 * ng
        fills = [pltpu.make_async_copy(ys_ref.at[0], out_hbm.at[pl.ds(spare0 + k * ng, ng)], ssem.at[0])
                 for k in range(2)]
        for cp in fills:
            cp.start()
        for cp in fills:
            cp.wait()


def _moe_routed(x, plan, w_gu, w_down, e0, *, tm):
    texp, n_used, idx = plan
    m, d = x.shape
    f = w_down.shape[1]
    n_tiles = texp.shape[0]
    once = pl.Buffered(1)
    return pl.pallas_call(
        _moe_routed_body,
        grid_spec=pltpu.PrefetchScalarGridSpec(
            num_scalar_prefetch=3,
            grid=(n_tiles,),
            in_specs=[pl.BlockSpec(memory_space=pl.ANY),
                      pl.BlockSpec((1, d, f), lambda i, te, nu, ix: (te[i] + e0, 0, 0), pipeline_mode=once),
                      pl.BlockSpec((1, d, f), lambda i, te, nu, ix: (te[i] + e0, 0, 1), pipeline_mode=once),
                      pl.BlockSpec((1, f, d), lambda i, te, nu, ix: (te[i] + e0, 0, 0), pipeline_mode=once)],
            out_specs=pl.BlockSpec(memory_space=pl.ANY),
            scratch_shapes=[pltpu.VMEM((2, tm // SUBLANES, SUBLANES, d), _F32),
                            pltpu.VMEM((2, tm // SUBLANES, SUBLANES, d), _F32),
                            pltpu.VMEM((d, f), _BF16), pltpu.VMEM((d, f), _BF16), pltpu.VMEM((f, d), _BF16),
                            pltpu.SemaphoreType.DMA((2,)), pltpu.SemaphoreType.DMA((2,))]),
        out_shape=jax.ShapeDtypeStruct(((2 * m + 2 * tm) // SUBLANES, SUBLANES, d), _F32),
        compiler_params=_cparams(1),
        name="moe_routed",
    )(texp, n_used, idx, x.reshape(m // SUBLANES, SUBLANES, d), w_gu, w_gu,
      w_down).reshape(2 * m + 2 * tm, d)


def _combine_ln_body(a_ref, b_ref, rt_ref, x_ref, g_ref, bb_ref, *o_refs, alpha, n_prompt):
    ff = rt_ref[:, 2:3] * a_ref[...] + rt_ref[:, 3:4] * b_ref[...]
    res = _layer_norm(alpha * x_ref[...] + ff, g_ref[0], bb_ref[0])
    if n_prompt is None:
        o_refs[0][...] = res
        o_refs[1][...] = res.astype(_BF16)
    else:
        i = pl.program_id(0)
        op_ref, os_ref = o_refs

        @pl.when(i < n_prompt)
        def _():
            op_ref[...] = res.reshape(op_ref.shape)

        @pl.when(i >= n_prompt)
        def _():
            os_ref[...] = res


def _combine_ln(y2, rt, x, ln_g, ln_b, li, *, alpha, prompt_shape=None):
    m, d = x.shape
    tm = min(ROW_TILE, m)
    nt = m // tm
    if prompt_shape is None:
        n_prompt = None
        out_specs = [pl.BlockSpec((tm, d), lambda i: (i, 0)), pl.BlockSpec((tm, d), lambda i: (i, 0))]
        out_shape = [jax.ShapeDtypeStruct((m, d), _F32), jax.ShapeDtypeStruct((m, d), _BF16)]
    else:
        bp, n_prompt, seg = prompt_shape
        assert bp * seg == tm and n_prompt == nt - 1
        out_specs = [pl.BlockSpec((bp, 1, seg, d), lambda i: (0, jnp.minimum(i, n_prompt - 1), 0, 0)),
                     pl.BlockSpec((tm, d), lambda i: (0, 0))]
        out_shape = [jax.ShapeDtypeStruct((bp, n_prompt, seg, d), _F32), jax.ShapeDtypeStruct((tm, d), _F32)]
    return pl.pallas_call(
        functools.partial(_combine_ln_body, alpha=alpha, n_prompt=n_prompt),
        grid=(nt,),
        in_specs=[pl.BlockSpec((tm, d), lambda i: (i, 0)),
                  pl.BlockSpec((tm, d), lambda i: (i + nt, 0)),
                  pl.BlockSpec((tm, LANES), lambda i: (i, 0)),
                  pl.BlockSpec((tm, d), lambda i: (i, 0)),
                  pl.BlockSpec((1, 1, d), lambda i: (li, 0, 0)),
                  pl.BlockSpec((1, 1, d), lambda i: (li, 0, 0))],
        out_specs=out_specs,
        out_shape=out_shape,
        compiler_params=_cparams(1),
        name="combine_ln",
    )(y2, y2, rt, x, ln_g, ln_b)


def _pack_body(xp_ref, xs_ref, o_ref, ob_ref, *, n_prompt):
    i = pl.program_id(0)

    @pl.when(i < n_prompt)
    def _():
        o_ref[...] = xp_ref[...].reshape(o_ref.shape)
        ob_ref[...] = xp_ref[...].reshape(o_ref.shape).astype(_BF16)

    @pl.when(i >= n_prompt)
    def _():
        o_ref[...] = xs_ref[...]
        ob_ref[...] = xs_ref[...].astype(_BF16)


def _pack_tokens(x_prompt4, x_sample2):
    bp, n_prompt, seg, d = x_prompt4.shape
    tm = bp * seg
    assert x_sample2.shape == (tm, d)
    return pl.pallas_call(
        functools.partial(_pack_body, n_prompt=n_prompt),
        grid=(n_prompt + 1,),
        in_specs=[pl.BlockSpec((bp, 1, seg, d), lambda i: (0, jnp.minimum(i, n_prompt - 1), 0, 0)),
                  pl.BlockSpec((tm, d), lambda i: (0, 0))],
        out_specs=[pl.BlockSpec((tm, d), lambda i: (i, 0)), pl.BlockSpec((tm, d), lambda i: (i, 0))],
        out_shape=[jax.ShapeDtypeStruct(((n_prompt + 1) * tm, d), _F32),
                   jax.ShapeDtypeStruct(((n_prompt + 1) * tm, d), _BF16)],
        compiler_params=_cparams(1),
        name="pack_tokens",
    )(x_prompt4, x_sample2)


def _seg_scan(a, b, tstep, n_steps, stride):
    sh = stride
    for s in range(n_steps):
        keep = tstep >= (1 << s)
        a_sh = pltpu.roll(a, sh, 0)
        b_sh = pltpu.roll(b, sh, 0)
        b = jnp.where(keep, a * b_sh + b, b)
        a = jnp.where(keep, a * a_sh, a)
        sh *= 2
    return a, b


def _lru_body(gate_ref, rec_ref, h0_ref, c0_ref, cw_ref, cb_ref, gw_ref, gb_ref, lam_ref,
              y_ref, hout_ref, cout_ref, xp_ref, hc_ref, cv_ref, gwb_ref, *, nseg, seg, stride, pad):
    rows = nseg * seg
    bw = gw_ref.shape[1]

    @pl.when(pl.program_id(0) == 0)
    def _():
        xp_ref[:, 0:pad, :] = c0_ref[...]
        hc_ref[...] = h0_ref[...]
        gwb_ref[...] = gw_ref[...].astype(_BF16)

    for s in range(nseg):
        xp_ref[s, pad:pad + seg, :] = rec_ref[s * seg:(s + 1) * seg, :]
        if stride % SUBLANES == 0:
            taps = [xp_ref[s, pad - k * stride:pad - k * stride + seg, :] for k in range(CONV_W)]
        else:
            hx = xp_ref[s, 0:pad + seg, :]
            taps = [hx[pad:]] + [pltpu.roll(hx, k * stride, 0)[pad:] for k in range(1, CONV_W)]
        acc = taps[CONV_W - 1] * cw_ref[0:1, :]
        for j in range(1, CONV_W):
            acc = acc + taps[CONV_W - 1 - j] * cw_ref[j:j + 1, :]
        cv_ref[s * seg:(s + 1) * seg, :] = acc + cb_ref[...]
        tail = xp_ref[s, seg:seg + pad, :]
        xp_ref[s, 0:pad, :] = tail
        cout_ref[s] = tail

    row = lax.broadcasted_iota(jnp.int32, (rows, bw), 0)
    if stride == 1:
        tstep = row & (seg - 1)
    else:
        tstep = row >> int(math.log2(stride))
    n_steps = int(math.log2(seg // stride))

    for n in range(LRU_BLOCKS):
        cs = slice(n * bw, (n + 1) * bw)
        cv = cv_ref[:, cs]
        gates = _dot(cv.astype(_BF16), gwb_ref[n]) + gb_ref[n]
        r = jax.nn.sigmoid(gates[:, :bw])
        ig = jax.nn.sigmoid(gates[:, bw:])
        log_a = -LRU_C * r * jax.nn.softplus(-lam_ref[:, cs])
        a = jnp.exp(log_a)
        s = -jnp.tanh(log_a) * (a * a + 1.0)
        mult = jnp.where(s > 0.0, s * lax.rsqrt(s), 0.0)
        a_cum, b_cum = _seg_scan(a, mult * ig * cv, tstep, n_steps, stride)
        hc = hc_ref[:, cs]
        if stride == 1:
            hprev = jnp.concatenate([jnp.broadcast_to(hc[s:s + 1, :], (seg, bw)) for s in range(nseg)], axis=0)
        else:
            hprev = jnp.concatenate([hc] * (seg // stride), axis=0)
        h = a_cum * hprev + b_cum
        y_ref[:, cs] = (h * jax.nn.gelu(gate_ref[:, cs])).astype(y_ref.dtype)
        if stride == 1:
            hc_ref[:, cs] = jnp.concatenate([h[(s + 1) * seg - 1:(s + 1) * seg, :] for s in range(nseg)], axis=0)
        else:
            hc_ref[:, cs] = h[rows - stride:, :]

    hout_ref[...] = hc_ref[...]


def _lru_seq(gu, h0, c0, conv_w, conv_b, gate_w, gate_b, lam, *, n_tiles, nseg, seg, stride, pad):
    rows = nseg * seg
    r = gu.shape[1] // 2
    bw = r // LRU_BLOCKS
    const2 = lambda i: (0, 0)
    const3 = lambda i: (0, 0, 0)
    return pl.pallas_call(
        functools.partial(_lru_body, nseg=nseg, seg=seg, stride=stride, pad=pad),
        grid=(n_tiles,),
        in_specs=[pl.BlockSpec((rows, r), lambda i: (i, 0)),
                  pl.BlockSpec((rows, r), lambda i: (i, 1)),
                  pl.BlockSpec((nseg * stride, r), const2),
                  pl.BlockSpec((nseg, pad, r), const3),
                  pl.BlockSpec((CONV_W, r), const2),
                  pl.BlockSpec((1, r), const2),
                  pl.BlockSpec((LRU_BLOCKS, bw, 2 * bw), const3),
                  pl.BlockSpec((LRU_BLOCKS, 1, 2 * bw), const3),
                  pl.BlockSpec((1, r), const2)],
        out_specs=[pl.BlockSpec((rows, r), lambda i: (i, 0)),
                   pl.BlockSpec((nseg * stride, r), const2),
                   pl.BlockSpec((nseg, pad, r), const3)],
        out_shape=[jax.ShapeDtypeStruct((n_tiles * rows, r), _BF16),
                   jax.ShapeDtypeStruct((nseg * stride, r), _F32),
                   jax.ShapeDtypeStruct((nseg, pad, r), _F32)],
        scratch_shapes=[pltpu.VMEM((nseg, pad + seg, r), _F32),
                        pltpu.VMEM((nseg * stride, r), _F32),
                        pltpu.VMEM((rows, r), _F32),
                        pltpu.VMEM((LRU_BLOCKS, bw, 2 * bw), _BF16)],
        compiler_params=_cparams(1),
        name="lru_seq",
    )(gu, gu, h0, c0, conv_w, conv_b, gate_w, gate_b, lam)


def _l2norm(t):
    return t * lax.rsqrt(jnp.sum(t * t, axis=-1, keepdims=True) + RMS_EPS)


def _gdn_chunks(seqs, norm_g, *, c, hg):
    dk = GDN_DK
    qk_w = GDN_HEADS * dk
    r4 = hg * c
    lg_c = int(math.log2(c))
    row = lax.broadcasted_iota(jnp.int32, (r4, r4), 0)
    col = lax.broadcasted_iota(jnp.int32, (r4, r4), 1)
    blk = {1 << s: (row >> s) == (col >> s) for s in range(3, lg_c + 1)}
    incl = blk[c] & (row >= col)
    strict = blk[c] & (row > col)
    units = [(si, list(range(g0, g0 + hg))) for si in range(len(seqs)) for g0 in range(0, GDN_HEADS, hg)]
    blocks = [slice(hh * c, (hh + 1) * c) for hh in range(hg)]

    def stack(fn, hs, axis=0):
        return jnp.concatenate([fn(h) for h in hs], axis=axis)

    tr = lax.broadcasted_iota(jnp.int32, (c, c), 0)
    tc = lax.broadcasted_iota(jnp.int32, (c, c), 1)
    tri = jnp.where(tr >= tc, 1.0, 0.0)
    gc = [_dot(tri, sq[3], precision=_HI) for sq in seqs]
    gct = [jnp.concatenate([x, jnp.zeros((LANES - c, LANES), _F32)], axis=0).T for x in gc]

    q4 = [stack(lambda h: _l2norm(seqs[si][0](h * dk)) * (dk ** -0.5), hs) for si, hs in units]
    k4 = [stack(lambda h: _l2norm(seqs[si][0](qk_w + h * dk)), hs) for si, hs in units]
    v4 = [stack(lambda h: seqs[si][0](2 * qk_w + h * dk), hs) for si, hs in units]
    beta4 = [stack(lambda h: jnp.broadcast_to(seqs[si][2][:, h:h + 1], (c, dk)), hs) for si, hs in units]
    gc4 = [stack(lambda h: jnp.broadcast_to(gc[si][:, h:h + 1], (c, dk)), hs) for si, hs in units]
    gcol = [stack(lambda h: jnp.broadcast_to(gc[si][:, h:h + 1], (c, r4)), hs) for si, hs in units]
    grow = [stack(lambda h: gct[si][h:h + 1, 0:c], hs, axis=1) for si, hs in units]
    decay = [jnp.exp(jnp.where(incl, gcl - grw, -jnp.inf)) for gcl, grw in zip(gcol, grow)]
    kb4 = [k * bt for k, bt in zip(k4, beta4)]
    aq = [_dot_nt(jnp.concatenate([kb, q], axis=0).astype(_BF16), k.astype(_BF16)) for kb, q, k in zip(kb4, q4, k4)]
    a4 = [jnp.where(strict, x[:r4] * dc, 0.0) for x, dc in zip(aq, decay)]
    attn4 = [x[r4:] * dc for x, dc in zip(aq, decay)]
    c0 = min(c, 16)
    np_ = [-jnp.where(blk[c0], x, 0.0) for x in a4]
    e = list(np_)
    for _ in range(int(math.log2(c0)) - 1):
        np_ = [_bdot(x, x) for x in np_]
        e = [ei + ni + _bdot(ni, ei) for ei, ni in zip(e, np_)]
    size = c0
    while size < c:
        lo = [jnp.where(blk[2 * size] & ~blk[size], x, 0.0) for x in a4]
        m = [li + _bdot(li, ei) for li, ei in zip(lo, e)]
        e = [ei - (mi + _bdot(ei, mi)) for ei, mi in zip(e, m)]
        size *= 2
    egc4 = [jnp.exp(x) for x in gc4]
    rhs = [jnp.concatenate([v * bt, kb * eg], axis=1) for v, bt, kb, eg in zip(v4, beta4, kb4, egc4)]
    uw = [r + _bdot(ei, r) for r, ei in zip(rhs, e)]
    qg4 = [q * eg for q, eg in zip(q4, egc4)]
    ws_qs = [[_dot(jnp.concatenate([uw[u][rs, dk:], qg4[u][rs]], axis=0).astype(_BF16),
                   seqs[si][4][h].astype(_BF16)) for rs, h in zip(blocks, hs)]
             for u, (si, hs) in enumerate(units)]
    vn = [[uw[u][rs, :dk] - ws_qs[u][hh][:c] for hh, rs in enumerate(blocks)] for u in range(len(units))]
    o4 = [jnp.concatenate([x[c:] for x in ws_qs[u]], axis=0)
          + _bdot(attn4[u], jnp.concatenate(vn[u], axis=0)) for u in range(len(units))]
    outs = [[None] * GDN_HEADS for _ in seqs]
    new_state = [[None] * GDN_HEADS for _ in seqs]
    for u, (si, hs) in enumerate(units):
        z, state = seqs[si][1], seqs[si][4]
        for hh, (rs, h) in enumerate(zip(blocks, hs)):
            glast = gc[si][c - 1:c, h:h + 1]
            kd = k4[u][rs] * jnp.exp(glast - gc4[u][rs])
            new_state[si][h] = state[h] * jnp.exp(glast) + _dot_tn(kd.astype(_BF16), vn[u][hh].astype(_BF16))
            oh = o4[u][rs]
            oh = oh * lax.rsqrt(jnp.mean(oh * oh, axis=-1, keepdims=True) + RMS_EPS) * norm_g
            outs[si][h] = oh * jax.nn.silu(z[:, h * dk:(h + 1) * dk])
    return outs, new_state


def _gdn_gates(ba, alog, dtb):
    beta = jax.nn.sigmoid(ba[:, :LANES])
    g = -jnp.exp(alog) * jax.nn.softplus(ba[:, LANES:] + dtb)
    return beta, g


def _gdn_conv(xp_ref, s, x, cw_ref, *, pad, seg):
    xp_ref[s, pad:pad + seg, :] = x

    def block(col):
        cs = slice(col, col + GDN_DK)
        hx = xp_ref[s, pl.ds(0, pad + seg), cs]
        acc = pltpu.roll(hx, CONV_W - 1, 0)[pad:] * cw_ref[0:1, cs]
        for j in range(1, CONV_W - 1):
            acc = acc + pltpu.roll(hx, CONV_W - 1 - j, 0)[pad:] * cw_ref[j:j + 1, cs]
        return jax.nn.silu(acc + hx[pad:] * cw_ref[CONV_W - 1:CONV_W, cs])

    return block


def _gdn_prompt_body(qkv_ref, z_ref, ba_ref, cw_ref, alog_ref, dtb_ref, ng_ref,
                     o_ref, sout_ref, cout_ref, xp_ref, s_ref, *, nseg, seg, pad):
    @pl.when(pl.program_id(0) == 0)
    def _():
        xp_ref[...] = jnp.zeros(xp_ref.shape, _F32)
        s_ref[...] = jnp.zeros(s_ref.shape, _F32)

    def per_group(i, carry):
        bs = [i * GDN_SEQS_PER_STEP + k for k in range(GDN_SEQS_PER_STEP)]
        r0 = [pl.multiple_of(b * seg, seg) for b in bs]
        seqs = []
        for b, r in zip(bs, r0):
            qkv = _gdn_conv(xp_ref, b, qkv_ref[pl.ds(r, seg), :], cw_ref, pad=pad, seg=seg)
            beta, g = _gdn_gates(ba_ref[pl.ds(r, seg), :], alog_ref[...], dtb_ref[...])
            seqs.append((qkv, z_ref[pl.ds(r, seg), :], beta, g, [s_ref[b, h] for h in range(GDN_HEADS)]))
        outs, new_state = _gdn_chunks(seqs, ng_ref[...], c=seg, hg=GDN_STACK_ROWS // seg)
        for k, (b, r) in enumerate(zip(bs, r0)):
            tail = xp_ref[b, pl.ds(seg, pad), :]
            xp_ref[b, 0:pad, :] = tail
            cout_ref[b] = tail
            for h in range(GDN_HEADS):
                s_ref[b, h] = new_state[k][h]
                o_ref[pl.ds(r, seg), h * GDN_DV:(h + 1) * GDN_DV] = outs[k][h].astype(o_ref.dtype)
        return carry

    lax.fori_loop(0, nseg // GDN_SEQS_PER_STEP, per_group, 0)

    @pl.when(pl.program_id(0) == pl.num_programs(0) - 1)
    def _():
        sout_ref[...] = s_ref[...]


def _gdn_prompt(proj, ba, conv_w, alog, dtb, norm_g, *, n_tiles, nseg, seg):
    rows = nseg * seg
    cd = conv_w.shape[1]
    vw = GDN_HEADS * GDN_DV
    pad = 8
    const2 = lambda i: (0, 0)
    return pl.pallas_call(
        functools.partial(_gdn_prompt_body, nseg=nseg, seg=seg, pad=pad),
        grid=(n_tiles,),
        in_specs=[pl.BlockSpec((rows, cd), lambda i: (i, 0)),
                  pl.BlockSpec((rows, vw), lambda i: (i, cd // vw)),
                  pl.BlockSpec((rows, 2 * LANES), lambda i: (i, 0)),
                  pl.BlockSpec((CONV_W, cd), const2),
                  pl.BlockSpec((1, LANES), const2),
                  pl.BlockSpec((1, LANES), const2),
                  pl.BlockSpec((1, GDN_DV), const2)],
        out_specs=[pl.BlockSpec((rows, vw), lambda i: (i, 0)),
                   pl.BlockSpec((nseg, GDN_HEADS, GDN_DK, GDN_DV), lambda i: (0, 0, 0, 0)),
                   pl.BlockSpec((nseg, pad, cd), lambda i: (0, 0, 0))],
        out_shape=[jax.ShapeDtypeStruct((n_tiles * rows, vw), _BF16),
                   jax.ShapeDtypeStruct((nseg, GDN_HEADS, GDN_DK, GDN_DV), _F32),
                   jax.ShapeDtypeStruct((nseg, pad, cd), _F32)],
        scratch_shapes=[pltpu.VMEM((nseg, pad + seg, cd), _F32),
                        pltpu.VMEM((nseg, GDN_HEADS, GDN_DK, GDN_DV), _F32)],
        compiler_params=_cparams(1),
        name="gdn_prompt",
    )(proj, proj, ba, conv_w, alog, dtb, norm_g)


def _gdn_sample_body(qkv_ref, z_ref, ba_ref, s0_ref, c0_ref, cw_ref, alog_ref, dtb_ref, ng_ref, sall_ref,
                     o_ref, sout_ref, cout_ref, xp_ref, *, nb, seg, steps, pad):
    del sall_ref
    s0_ref, sout_ref = s0_ref.at[0], sout_ref.at[0]
    def per_group(i, carry):
        bs = [i * GDN_SEQS_PER_STEP + k for k in range(GDN_SEQS_PER_STEP)]
        seqs = []
        for k, b in enumerate(bs):
            xp_ref[k, 0:pad, :] = c0_ref[b]
            qkv = _gdn_conv(xp_ref, k, qkv_ref[b], cw_ref, pad=pad, seg=seg)
            cout_ref[b] = xp_ref[k, pl.ds(steps, pad), :]
            beta, g = _gdn_gates(ba_ref[b], alog_ref[...], dtb_ref[...])
            live = lax.broadcasted_iota(jnp.int32, beta.shape, 0) < steps
            seqs.append((qkv, z_ref[b], jnp.where(live, beta, 0.0), jnp.where(live, g, 0.0),
                         [s0_ref[b, h] for h in range(GDN_HEADS)]))
        outs, new_state = _gdn_chunks(seqs, ng_ref[...], c=seg, hg=GDN_HEADS)
        for k, b in enumerate(bs):
            for h in range(GDN_HEADS):
                sout_ref[b, h] = new_state[k][h]
                o_ref[b, :, h * GDN_DV:(h + 1) * GDN_DV] = outs[k][h]
        return carry

    lax.fori_loop(0, nb // GDN_SEQS_PER_STEP, per_group, 0)


def _gdn_sample(proj, ba, s0_all, layer, s_new_all, c0, conv_w, alog, dtb, norm_g, *, steps):
    nbatch, seg, _ = proj.shape
    cd = conv_w.shape[1]
    vw = GDN_HEADS * GDN_DV
    pad = c0.shape[1]
    nb = 8
    const2 = lambda i: (0, 0)
    return pl.pallas_call(
        functools.partial(_gdn_sample_body, nb=nb, seg=seg, steps=steps, pad=pad),
        grid=(nbatch // nb,),
        in_specs=[pl.BlockSpec((nb, seg, cd), lambda i: (i, 0, 0)),
                  pl.BlockSpec((nb, seg, vw), lambda i: (i, 0, cd // vw)),
                  pl.BlockSpec((nb, seg, 2 * LANES), lambda i: (i, 0, 0)),
                  pl.BlockSpec((1, nb, GDN_HEADS, GDN_DK, GDN_DV), lambda i: (layer, i, 0, 0, 0)),
                  pl.BlockSpec((nb, pad, cd), lambda i: (i, 0, 0)),
                  pl.BlockSpec((CONV_W, cd), const2),
                  pl.BlockSpec((1, LANES), const2),
                  pl.BlockSpec((1, LANES), const2),
                  pl.BlockSpec((1, GDN_DV), const2),
                  pl.BlockSpec(memory_space=pl.ANY)],
        out_specs=[pl.BlockSpec((nb, seg, vw), lambda i: (i, 0, 0)),
                   pl.BlockSpec((1, nb, GDN_HEADS, GDN_DK, GDN_DV), lambda i: (layer, i, 0, 0, 0)),
                   pl.BlockSpec((nb, pad, cd), lambda i: (i, 0, 0))],
        out_shape=[jax.ShapeDtypeStruct((nbatch, seg, vw), _F32),
                   jax.ShapeDtypeStruct(s_new_all.shape, _F32),
                   jax.ShapeDtypeStruct((nbatch, pad, cd), _F32)],
        scratch_shapes=[pltpu.VMEM((GDN_SEQS_PER_STEP, pad + seg, cd), _F32)],
        input_output_aliases={9: 1},
        compiler_params=_cparams(1),
        name="gdn_sample",
    )(proj, proj, ba, s0_all, c0, conv_w, alog, dtb, norm_g, s_new_all)


def _softmax_rows(s):
    e = jnp.exp(s - jnp.max(s, axis=-1, keepdims=True))
    return e / jnp.sum(e, axis=-1, keepdims=True)


def _attn_prompt_body(q_ref, k_ref, v_ref, o_ref, *, nseg, seg, ntile):
    hd = q_ref.shape[1] // XA_HEADS
    scale = hd ** -0.5
    for b in range(nseg):
        runs = [slice((t * nseg + b) * seg, (t * nseg + b + 1) * seg) for t in range(ntile)]
        heads = [slice(h * hd, (h + 1) * hd) for h in range(XA_HEADS)]
        s = [_dot_nt(jnp.concatenate([q_ref[rs, cs] for rs in runs], axis=0), k_ref[b, :, cs]) * scale
             for cs in heads]
        p = [_softmax_rows(sh).astype(_BF16) for sh in s]
        o = [_dot(ph, v_ref[b, :, cs]).astype(o_ref.dtype) for ph, cs in zip(p, heads)]
        for oh, cs in zip(o, heads):
            for t, rs in enumerate(runs):
                o_ref[rs, cs] = oh[t * seg:(t + 1) * seg]


def _attn_prompt(q, mem_k, mem_v, *, n_tiles, nseg, seg):
    ntile = math.gcd(n_tiles, 4)
    rows = nseg * seg * ntile
    d = q.shape[1]
    mem = mem_k.shape[1]
    return pl.pallas_call(
        functools.partial(_attn_prompt_body, nseg=nseg, seg=seg, ntile=ntile),
        grid=(n_tiles // ntile,),
        in_specs=[pl.BlockSpec((rows, d), lambda i: (i, 0)),
                  pl.BlockSpec((nseg, mem, d), lambda i: (0, 0, 0)),
                  pl.BlockSpec((nseg, mem, d), lambda i: (0, 0, 0))],
        out_specs=pl.BlockSpec((rows, d), lambda i: (i, 0)),
        out_shape=jax.ShapeDtypeStruct((n_tiles * nseg * seg, d), _BF16),
        compiler_params=_cparams(1),
        name="attn_prompt",
    )(q, mem_k, mem_v)


def _attn_sample_body(q_ref, k_ref, v_ref, o_ref, *, nb):
    t8 = q_ref.shape[1]
    mem, nh, hd = k_ref.shape[2:]
    scale = hd ** -0.5
    qh = lax.broadcasted_iota(jnp.int32, (nh * t8, mem * nh), 0) // t8
    kh = lax.broadcasted_iota(jnp.int32, (nh * t8, mem * nh), 1) % nh
    own = qh == kh

    def per_group(i, carry):
        bs = [i * ATTN_SEQS_PER_STEP + k for k in range(ATTN_SEQS_PER_STEP)]
        q2 = [jnp.concatenate([q_ref[b][:, h * hd:(h + 1) * hd] for h in range(nh)], axis=0).astype(_BF16)
              for b in bs]
        s = [jnp.where(own, _dot_nt(qq, k_ref[0, b].reshape(mem * nh, hd).astype(_BF16)) * scale, -jnp.inf)
             for qq, b in zip(q2, bs)]
        p = [_softmax_rows(x).astype(_BF16) for x in s]
        o2 = [_dot(pp, v_ref[0, b].reshape(mem * nh, hd).astype(_BF16)) for pp, b in zip(p, bs)]
        for oo, b in zip(o2, bs):
            for h in range(nh):
                o_ref[b, :, h * hd:(h + 1) * hd] = oo[h * t8:(h + 1) * t8]
        return carry

    lax.fori_loop(0, nb // ATTN_SEQS_PER_STEP, per_group, 0)


def _attn_sample(q, cache_k, cache_v, layer):
    nbatch, t8, d = q.shape
    mem, nh, hd = cache_k.shape[2:]
    nb = 8
    return pl.pallas_call(
        functools.partial(_attn_sample_body, nb=nb),
        grid=(nbatch // nb,),
        in_specs=[pl.BlockSpec((nb, t8, d), lambda i: (i, 0, 0)),
                  pl.BlockSpec((1, nb, mem, nh, hd), lambda i: (layer, i, 0, 0, 0)),
                  pl.BlockSpec((1, nb, mem, nh, hd), lambda i: (layer, i, 0, 0, 0))],
        out_specs=pl.BlockSpec((nb, t8, d), lambda i: (i, 0, 0)),
        out_shape=jax.ShapeDtypeStruct(q.shape, _F32),
        compiler_params=_cparams(1),
        name="attn_sample",
    )(q, cache_k, cache_v)


def kernel(x_prompt, x_sample, mem_prompt, state_lru_h, state_lru_conv, state_gdn_s, state_gdn_conv,
           cache_mem_k, cache_mem_v, ln_g, ln_b, xa_wq, xa_wk, xa_wv, xa_wo,
           lru_w_in, lru_conv_w, lru_conv_b, lru_gate_w, lru_gate_b, lru_lambda, lru_w_out,
           gdn_w_in, gdn_conv_w, gdn_a_log, gdn_dt_bias, gdn_norm_g, gdn_w_out,
           ffn_w_gu, ffn_w_down, moe_router, moe_w_gu, moe_w_down):
    depth = ln_g.shape[0]
    bp, seq, d = x_prompt.shape
    bs, ts, _ = x_sample.shape
    mem = mem_prompt.shape[1]
    alpha = (2 * depth) ** 0.25
    seg = GDN_CHUNK
    nchunk = seq // seg
    mp, ms = bp * seq, bs * ts
    assert bp * seg == ROW_TILE and ms == ROW_TILE and seq % seg == 0 and ts >= CONV_W - 1
    assert depth % 2 == 0
    t8 = 8
    pad_p = 8
    pad_s = (CONV_W - 1) * bs
    r = lru_w_in.shape[2] // 2
    cd = gdn_conv_w.shape[2]
    vw = GDN_HEADS * GDN_DV
    d_ff = ffn_w_gu.shape[2] // 2
    d_fe = moe_w_gu.shape[3] // 2

    x, xb = _pack_tokens(x_prompt.reshape(bp, nchunk, seg, d), x_sample.reshape(ms, d))

    memx = mem_prompt.reshape(bp * mem, d)
    mem_k_p = _mm(memx, xa_wk, ng=depth, tn=1024, out_dtype=_F32).reshape(depth, bp, mem, d)
    mem_v_p = _mm(memx, xa_wv, ng=depth, tn=1024, out_dtype=_F32).reshape(depth, bp, mem, d)
    mem_k_b = mem_k_p.astype(_BF16)
    mem_v_b = mem_v_p.astype(_BF16)

    ln_g3 = ln_g.reshape(depth * 3, 1, d)
    moe_w_gu3 = moe_w_gu.reshape((-1,) + moe_w_gu.shape[2:])
    moe_w_down3 = moe_w_down.reshape((-1,) + moe_w_down.shape[2:])
    assert mp + ms < (1 << IDX_SHIFT)
    ln_b3 = ln_b.reshape(depth * 3, 1, d)

    def to_time_major(a):
        return a.reshape(bs, ts, -1).transpose(1, 0, 2).reshape(ms, -1)

    def to_batch_major(a):
        return a.reshape(ts, bs, -1).transpose(1, 0, 2).reshape(ms, -1)

    def pad_t(a):
        a = a.reshape(bs, ts, -1)
        return jnp.pad(a, ((0, 0), (0, t8 - ts), (0, 0)))

    lru_h_p, lru_c_p, lru_h_s, lru_c_s = [], [], [], []
    gdn_s_p, gdn_c_p, gdn_c_s = [], [], []
    gdn_s_s = jnp.zeros(state_gdn_s.shape, _F32)

    for i in range(depth):
        j = i // 2
        if i % 2 == 0:
            gu = _mm(xb, lru_w_in, g0=j, tn=1024, out_dtype=_F32)[0]
            lam = lru_lambda[j].reshape(1, r)
            cb = lru_conv_b[j].reshape(1, r)
            gb = lru_gate_b[j].reshape(LRU_BLOCKS, 1, -1)
            y, h_p, c_p = _lru_seq(gu, jnp.zeros((bp, r), _F32), jnp.zeros((bp, pad_p, r), _F32),
                                   lru_conv_w[j], cb, lru_gate_w[j], gb, lam,
                                   n_tiles=nchunk, nseg=bp, seg=seg, stride=1, pad=pad_p)
            gu_s = to_time_major(gu[mp:])
            c0_s = state_lru_conv[j].transpose(1, 0, 2).reshape(1, pad_s, r)
            y_s, h_s, c_s = _lru_seq(gu_s, state_lru_h[j], c0_s, lru_conv_w[j], cb, lru_gate_w[j], gb, lam,
                                     n_tiles=1, nseg=1, seg=ms, stride=bs, pad=pad_s)
            y_s = to_batch_major(y_s)
            lru_h_p.append(h_p)
            lru_c_p.append(c_p[:, pad_p - (CONV_W - 1):, :])
            lru_h_s.append(h_s)
            lru_c_s.append(c_s.reshape(CONV_W - 1, bs, r).transpose(1, 0, 2))
            w_out, wi = lru_w_out, j
        else:
            proj = _mm(xb, gdn_w_in, g0=j, n=cd + vw, tn=1024, out_dtype=_F32)[0]
            w_tail = gdn_w_in[j, :, cd + vw:]
            w_ba = jnp.zeros((1, d, 2 * LANES), _F32)
            w_ba = w_ba.at[0, :, :GDN_HEADS].set(w_tail[:, :GDN_HEADS])
            w_ba = w_ba.at[0, :, LANES:LANES + GDN_HEADS].set(w_tail[:, GDN_HEADS:])
            ba = _mm(xb, w_ba, tn=2 * LANES, out_dtype=_F32)[0]
            alog = jnp.zeros((1, LANES), _F32).at[0, :GDN_HEADS].set(gdn_a_log[j])
            dtb = jnp.zeros((1, LANES), _F32).at[0, :GDN_HEADS].set(gdn_dt_bias[j])
            ng = gdn_norm_g[j].reshape(1, GDN_DV)
            y, s_p, c_p = _gdn_prompt(proj, ba, gdn_conv_w[j], alog, dtb, ng, n_tiles=nchunk, nseg=bp, seg=seg)
            c0_s = jnp.pad(state_gdn_conv[j], ((0, 0), (pad_p - (CONV_W - 1), 0), (0, 0)))
            y_s, gdn_s_s, c_s = _gdn_sample(pad_t(proj[mp:]), pad_t(ba[mp:]), state_gdn_s, j, gdn_s_s, c0_s,
                                            gdn_conv_w[j], alog, dtb, ng, steps=ts)
            y_s = y_s[:, :ts, :].reshape(ms, vw)
            gdn_s_p.append(s_p)
            gdn_c_p.append(c_p[:, pad_p - (CONV_W - 1):, :])
            gdn_c_s.append(c_s[:, pad_p - (CONV_W - 1):, :])
            w_out, wi = gdn_w_out, j
        x, xb, q = _mm_ln(y, y_s, w_out, wi, x, ln_g3, ln_b3, 3 * i, alpha=alpha, proj=(xa_wq, i))

        o = _attn_prompt(q, mem_k_b[i], mem_v_b[i], n_tiles=nchunk, nseg=bp, seg=seg)
        o_s = _attn_sample(pad_t(q[mp:].astype(_F32)), cache_mem_k, cache_mem_v, i)[:, :ts, :].reshape(ms, d)

        if i % 2 == 0:
            x, xb = _mm_ln(o, o_s, xa_wo, i, x, ln_g3, ln_b3, 3 * i + 1, alpha=alpha)
            hdn = _gate_up(xb, ffn_w_gu, g0=j, ng=1, f=d_ff, tf=512)[0]
            x, xb = _mm_ln(hdn, None, ffn_w_down, j, x, ln_g3, ln_b3, 3 * i + 2, alpha=alpha)
        else:
            router_pad = jnp.zeros((d, LANES), _F32).at[:, :N_EXPERTS].set(moe_router[j])
            x, xb, rt = _mm_ln(o, o_s, xa_wo, i, x, ln_g3, ln_b3, 3 * i + 1, alpha=alpha, route=router_pad)
            y2 = _moe_routed(x, _route_plan(rt, ROW_TILE), moe_w_gu3, moe_w_down3, j * N_EXPERTS, tm=ROW_TILE)
            if i < depth - 1:
                x, xb = _combine_ln(y2, rt, x, ln_g3, ln_b3, 3 * i + 2, alpha=alpha)
            else:
                y_prompt, y_sample = _combine_ln(y2, rt, x, ln_g3, ln_b3, 3 * i + 2, alpha=alpha,
                                                 prompt_shape=(bp, nchunk, seg))

    y_prompt = y_prompt.reshape(bp, seq, d)
    y_sample = y_sample.reshape(bs, ts, d)
    hd = d // XA_HEADS
    return (y_prompt, y_sample,
            jnp.stack(lru_h_p), jnp.stack(lru_c_p), jnp.stack(gdn_s_p), jnp.stack(gdn_c_p),
            mem_k_p.reshape(depth, bp, mem, XA_HEADS, hd), mem_v_p.reshape(depth, bp, mem, XA_HEADS, hd),
            jnp.stack(lru_h_s), jnp.stack(lru_c_s), gdn_s_s, jnp.stack(gdn_c_s))
```

```python
import functools
import math

import jax
import jax.numpy as jnp
from jax import lax
from jax.experimental import pallas as pl
from jax.experimental.pallas import tpu as pltpu

_F32 = jnp.float32
_BF16 = jnp.bfloat16
_HI = lax.Precision.HIGHEST

CONV_W = 4
LRU_BLOCKS = 4
LRU_C = 8.0
GDN_HEADS = 8
GDN_DK = 128
GDN_DV = 128
GDN_CHUNK = 64
GDN_SEQS_PER_STEP = 4
GDN_STACK_ROWS = 128
XA_HEADS = 4
ATTN_SEQS_PER_STEP = 4
N_EXPERTS = 8
IDX_SHIFT = 15
LN_EPS = 1e-5
RMS_EPS = 1e-6
LANES = 128
SUBLANES = 8
ROW_TILE = 512
MM_LN_SPLIT = 2
VMEM_LIMIT_MB = 56


def _big_row_tile(m):
    return 3 * ROW_TILE if m % (3 * ROW_TILE) == 0 else min(ROW_TILE, m)


def _cparams(ndim, vmem_mb=VMEM_LIMIT_MB):
    return pltpu.CompilerParams(dimension_semantics=("arbitrary",) * ndim,
                                vmem_limit_bytes=vmem_mb * 1024 * 1024)


def _dot(a, b, precision=None):
    return jnp.dot(a, b, preferred_element_type=_F32, precision=precision)


def _bdot(a, b):
    return _dot(a.astype(_BF16), b.astype(_BF16))


def _dot_nt(a, b):
    return lax.dot_general(a, b, (((1,), (1,)), ((), ())), preferred_element_type=_F32)


def _dot_tn(a, b):
    return lax.dot_general(a, b, (((0,), (0,)), ((), ())), preferred_element_type=_F32)


def _layer_norm(v, g, b):
    mu = jnp.mean(v, axis=-1, keepdims=True)
    d = v - mu
    var = jnp.mean(d * d, axis=-1, keepdims=True)
    return d * lax.rsqrt(var + LN_EPS) * g + b


def _mm_body(x_ref, w_ref, o_ref, wb_ref):
    @pl.when(pl.program_id(2) == 0)
    def _():
        wb_ref[...] = w_ref[0].astype(_BF16)

    o_ref[0] = _dot(x_ref[...].astype(_BF16), wb_ref[...]).astype(o_ref.dtype)


def _mm(x, w, *, g0=0, ng=1, n0=0, n=None, tn, out_dtype):
    m, k = x.shape
    n = w.shape[2] if n is None else n
    tm = _big_row_tile(m)
    nb0 = n0 // tn
    return pl.pallas_call(
        _mm_body,
        grid=(ng, n // tn, m // tm),
        in_specs=[pl.BlockSpec((tm, k), lambda g, j, i: (i, 0)),
                  pl.BlockSpec((1, k, tn), lambda g, j, i: (g + g0, 0, j + nb0))],
        out_specs=pl.BlockSpec((1, tm, tn), lambda g, j, i: (g, i, j)),
        out_shape=jax.ShapeDtypeStruct((ng, m, n), out_dtype),
        scratch_shapes=[pltpu.VMEM((k, tn), _BF16)],
        compiler_params=_cparams(3),
        name="mm",
    )(x, w)


def _mm_ln_body(*refs, alpha, n1, post):
    if post is None:
        y_ref, y2_ref, w_ref, x_ref, g_ref, b_ref, o_ref, ob_ref, wb_ref = refs
    elif post == "proj":
        y_ref, y2_ref, w_ref, x_ref, g_ref, b_ref, w2_ref, o_ref, ob_ref, p_ref, wb_ref, wb2_ref = refs
    else:
        y_ref, y2_ref, w_ref, x_ref, g_ref, b_ref, r_ref, o_ref, ob_ref, p_ref, wb_ref = refs
    i = pl.program_id(0)

    @pl.when(i == 0)
    def _():
        wb_ref[...] = w_ref[0].astype(_BF16)
        if post == "proj":
            wb2_ref[...] = w2_ref[0].astype(_BF16)

    def emit(src_ref):
        tm = o_ref.shape[0]
        blocks = [slice(k * tm // MM_LN_SPLIT, (k + 1) * tm // MM_LN_SPLIT) for k in range(MM_LN_SPLIT)]
        mix = [_dot(src_ref[rs, :].astype(_BF16), wb_ref[...]) for rs in blocks]
        res = [_layer_norm(alpha * x_ref[rs, :] + m, g_ref[0], b_ref[0]) for rs, m in zip(blocks, mix)]
        for rs, r in zip(blocks, res):
            o_ref[rs, :] = r
            ob_ref[rs, :] = r.astype(_BF16)
        if post == "proj":
            for rs, r in zip(blocks, res):
                p_ref[rs, :] = _dot(r.astype(_BF16), wb2_ref[...]).astype(p_ref.dtype)
        elif post == "route":
            for rs, r in zip(blocks, res):
                p_ref[rs, :] = _route_top2(r, r_ref[...])

    if n1 is None:
        emit(y_ref)
    else:
        pl.when(i < n1)(lambda: emit(y_ref))
        pl.when(i >= n1)(lambda: emit(y2_ref))


def _mm_ln(y, y2, w, gi, x, ln_g, ln_b, li, *, alpha, proj=None, route=None):
    m, d = x.shape
    k = y.shape[1]
    tm = min(ROW_TILE, m)
    n1 = y.shape[0] // tm
    if y2 is None:
        y2 = y
    assert n1 * tm == y.shape[0] and (n1 == m // tm or n1 * tm + y2.shape[0] == m)
    assert proj is None or route is None
    row = lambda i: (i, 0)
    in_specs = [pl.BlockSpec((tm, k), lambda i: (jnp.minimum(i, n1 - 1), 0)),
                pl.BlockSpec((tm, k), lambda i: (jnp.maximum(i - n1, 0), 0)),
                pl.BlockSpec((1, k, d), lambda i: (gi, 0, 0), pipeline_mode=pl.Buffered(1)),
                pl.BlockSpec((tm, d), row),
                pl.BlockSpec((1, 1, d), lambda i: (li, 0, 0)),
                pl.BlockSpec((1, 1, d), lambda i: (li, 0, 0))]
    args = [y, y2, w, x, ln_g, ln_b]
    out_specs = [pl.BlockSpec((tm, d), row), pl.BlockSpec((tm, d), row)]
    out_shape = [jax.ShapeDtypeStruct((m, d), _F32), jax.ShapeDtypeStruct((m, d), _BF16)]
    scratch = [pltpu.VMEM((k, d), _BF16)]
    post = None
    if proj is not None:
        post = "proj"
        w2, g2 = proj
        n2 = w2.shape[2]
        in_specs.append(pl.BlockSpec((1, d, n2), lambda i: (g2, 0, 0), pipeline_mode=pl.Buffered(1)))
        args.append(w2)
        out_specs.append(pl.BlockSpec((tm, n2), row))
        out_shape.append(jax.ShapeDtypeStruct((m, n2), _BF16))
        scratch.append(pltpu.VMEM((d, n2), _BF16))
    elif route is not None:
        post = "route"
        in_specs.append(pl.BlockSpec((d, LANES), lambda i: (0, 0)))
        args.append(route)
        out_specs.append(pl.BlockSpec((tm, LANES), row))
        out_shape.append(jax.ShapeDtypeStruct((m, LANES), _F32))
    return pl.pallas_call(
        functools.partial(_mm_ln_body, alpha=alpha, n1=None if n1 == m // tm else n1, post=post),
        grid=(m // tm,),
        in_specs=in_specs,
        out_specs=out_specs,
        out_shape=out_shape,
        scratch_shapes=scratch,
        compiler_params=_cparams(1),
        name="mm_ln",
    )(*args)


def _gu_body(x_ref, wg_ref, wu_ref, o_ref, wgb_ref, wub_ref):
    @pl.when(pl.program_id(2) == 0)
    def _():
        wgb_ref[...] = wg_ref[0].astype(_BF16)
        wub_ref[...] = wu_ref[0].astype(_BF16)

    xb = x_ref[...].astype(_BF16)
    g = _dot(xb, wgb_ref[...])
    u = _dot(xb, wub_ref[...])
    o_ref[0] = (jax.nn.silu(g) * u).astype(o_ref.dtype)


def _gate_up(x, w_gu, *, g0, ng, f, tf):
    m, k = x.shape
    tm = _big_row_tile(m)
    nf = f // tf
    return pl.pallas_call(
        _gu_body,
        grid=(ng, nf, m // tm),
        in_specs=[pl.BlockSpec((tm, k), lambda g, j, i: (i, 0)),
                  pl.BlockSpec((1, k, tf), lambda g, j, i: (g + g0, 0, j)),
                  pl.BlockSpec((1, k, tf), lambda g, j, i: (g + g0, 0, j + nf))],
        out_specs=pl.BlockSpec((1, tm, tf), lambda g, j, i: (g, i, j)),
        out_shape=jax.ShapeDtypeStruct((ng, m, f), _BF16),
        scratch_shapes=[pltpu.VMEM((k, tf), _BF16), pltpu.VMEM((k, tf), _BF16)],
        compiler_params=_cparams(3),
        name="gate_up",
    )(x, w_gu, w_gu)


def _route_top2(x, router_pad):
    x_hi = x.astype(_BF16)
    x_lo = (x - x_hi.astype(_F32)).astype(_BF16)
    r_hi = router_pad.astype(_BF16)
    r_lo = (router_pad - r_hi.astype(_F32)).astype(_BF16)
    logits = _dot(x_hi, r_hi) + (_dot(x_hi, r_lo) + _dot(x_lo, r_hi))
    lane = lax.broadcasted_iota(jnp.int32, logits.shape, 1).astype(_F32)
    l1 = jnp.where(lane < N_EXPERTS, logits, -jnp.inf)
    m1 = jnp.max(l1, axis=-1, keepdims=True)
    i1 = jnp.min(jnp.where(l1 == m1, lane, float(LANES)), axis=-1, keepdims=True)
    l2 = jnp.where(lane == i1, -jnp.inf, l1)
    m2 = jnp.max(l2, axis=-1, keepdims=True)
    i2 = jnp.min(jnp.where(l2 == m2, lane, float(LANES)), axis=-1, keepdims=True)
    e2 = jnp.exp(m2 - m1)
    den = 1.0 + e2
    return (jnp.where(lane == 0.0, i1, 0.0) + jnp.where(lane == 1.0, i2, 0.0)
            + jnp.where(lane == 2.0, 1.0 / den, 0.0) + jnp.where(lane == 3.0, e2 / den, 0.0))


def _route_plan(rt, tm):
    m = rt.shape[0]
    ef = rt[:, :2].astype(jnp.int32).reshape(-1)
    onehot = (ef[:, None] == jnp.arange(N_EXPERTS, dtype=jnp.int32)[None, :]).astype(jnp.int32)
    csum = jnp.cumsum(onehot, axis=0)
    cnt = csum[-1]
    rank = jnp.sum((csum - onehot) * onehot, axis=1)
    cnt_p = ((cnt + tm - 1) // tm) * tm
    end = jnp.cumsum(cnt_p)
    start = end - cnt_p
    dest = start[ef] + rank
    n_tiles = (2 * m) // tm + N_EXPERTS
    rows = n_tiles * tm
    pair = jnp.zeros((rows,), jnp.int32).at[dest].set(jnp.arange(2 * m, dtype=jnp.int32))
    tile_start = jnp.arange(n_tiles, dtype=jnp.int32) * tm
    texp = jnp.sum((tile_start[:, None] >= end[None, :]).astype(jnp.int32), axis=1)
    used = tile_start < end[-1]
    last = jnp.maximum(end[-1] // tm - 1, 0)
    texp = jnp.where(used, texp, texp[last])
    nval = jnp.where(used, jnp.clip(start[texp] + cnt[texp] - tile_start, 0, tm), 0)
    tile = jnp.repeat(jnp.arange(n_tiles, dtype=jnp.int32), tm)
    in_tile = jnp.arange(rows, dtype=jnp.int32) - tile * tm
    live = in_tile < jnp.repeat(nval, tm)
    tok, slot = pair // 2, pair % 2
    spare = 2 * m + (tile % 2) * tm + in_tile
    idx = jnp.where(live, tok, 0) | (jnp.where(live, slot * m + tok, spare) << IDX_SHIFT)
    n_used = (end[-1] // tm).reshape(1)
    return texp, n_used, idx


def _moe_routed_body(texp_ref, nused_ref, idx_ref, x_hbm, wg_ref, wu_ref, wd_ref, out_hbm,
                     xs_ref, ys_ref, wgb_ref, wub_ref, wdb_ref, gsem, ssem):
    i = pl.program_id(0)
    ng = xs_ref.shape[1]
    tm = ng * SUBLANES
    d = xs_ref.shape[3]
    n_used = nused_ref[0]
    slot = i % 2

    def gather(tile, s):
        def body(g, c):
            for u in range(SUBLANES):
                tok = idx_ref[tile * tm + g * SUBLANES + u] & ((1 << IDX_SHIFT) - 1)
                pltpu.make_async_copy(x_hbm.at[tok >> 3, tok & 7], xs_ref.at[s, g, u],
                                      gsem.at[s]).start(priority=u % 2)
            return c
        lax.fori_loop(0, ng, body, 0)

    def scatter(tile, s):
        def body(g, c):
            for u in range(SUBLANES):
                dst = idx_ref[tile * tm + g * SUBLANES + u] >> IDX_SHIFT
                pltpu.make_async_copy(ys_ref.at[s, g, u], out_hbm.at[dst >> 3, dst & 7],
                                      ssem.at[s]).start(priority=u % 2)
            return c
        lax.fori_loop(0, ng, body, 0)

    def wait_gather(s):
        pltpu.make_async_copy(x_hbm.at[pl.ds(0, ng)], xs_ref.at[s], gsem.at[s]).wait()

    def wait_scatter(s):
        pltpu.make_async_copy(ys_ref.at[s], out_hbm.at[pl.ds(0, ng)], ssem.at[s]).wait()

    @pl.when((i == 0) & (n_used > 0))
    def _():
        gather(0, 0)

    @pl.when(i + 1 < n_used)
    def _():
        gather(i + 1, 1 - slot)

    @pl.when(i < n_used)
    def _():
        wait_gather(slot)

        @pl.when(i >= 2)
        def _():
            wait_scatter(slot)

        @pl.when((i == 0) | (texp_ref[i] != texp_ref[jnp.maximum(i - 1, 0)]))
        def _():
            wgb_ref[...] = wg_ref[0].astype(_BF16)
            wub_ref[...] = wu_ref[0].astype(_BF16)
            wdb_ref[...] = wd_ref[0].astype(_BF16)

        xb = xs_ref[slot].reshape(tm, d).astype(_BF16)
        h = (jax.nn.silu(_dot(xb, wgb_ref[...])) * _dot(xb, wub_ref[...])).astype(_BF16)
        ys_ref[slot] = _dot(h, wdb_ref[...]).reshape(ng, SUBLANES, d)
        scatter(i, slot)

    @pl.when(i == pl.num_programs(0) - 1)
    def _():
        @pl.when(n_used >= 1)
        def _():
            wait_scatter((n_used - 1) % 2)

        @pl.when(n_used >= 2)
        def _():
            wait_scatter(n_used % 2)

        ys_ref[0] = jnp.zeros(ys_ref.shape[1:], _F32)
        spare0 = out_hbm.shape[0] - 2 * ng
        fills = [pltpu.make_async_copy(ys_ref.at[0], out_hbm.at[pl.ds(spare0 + k * ng, ng)], ssem.at[0])
                 for k in range(2)]
        for cp in fills:
            cp.start()
        for cp in fills:
            cp.wait()


def _moe_routed(x, plan, w_gu, w_down, e0, *, tm):
    texp, n_used, idx = plan
    m, d = x.shape
    f = w_down.shape[1]
    n_tiles = texp.shape[0]
    once = pl.Buffered(1)
    return pl.pallas_call(
        _moe_routed_body,
        grid_spec=pltpu.PrefetchScalarGridSpec(
            num_scalar_prefetch=3,
            grid=(n_tiles,),
            in_specs=[pl.BlockSpec(memory_space=pl.ANY),
                      pl.BlockSpec((1, d, f), lambda i, te, nu, ix: (te[i] + e0, 0, 0), pipeline_mode=once),
                      pl.BlockSpec((1, d, f), lambda i, te, nu, ix: (te[i] + e0, 0, 1), pipeline_mode=once),
                      pl.BlockSpec((1, f, d), lambda i, te, nu, ix: (te[i] + e0, 0, 0), pipeline_mode=once)],
            out_specs=pl.BlockSpec(memory_space=pl.ANY),
            scratch_shapes=[pltpu.VMEM((2, tm // SUBLANES, SUBLANES, d), _F32),
                            pltpu.VMEM((2, tm // SUBLANES, SUBLANES, d), _F32),
                            pltpu.VMEM((d, f), _BF16), pltpu.VMEM((d, f), _BF16), pltpu.VMEM((f, d), _BF16),
                            pltpu.SemaphoreType.DMA((2,)), pltpu.SemaphoreType.DMA((2,))]),
        out_shape=jax.ShapeDtypeStruct(((2 * m + 2 * tm) // SUBLANES, SUBLANES, d), _F32),
        compiler_params=_cparams(1),
        name="moe_routed",
    )(texp, n_used, idx, x.reshape(m // SUBLANES, SUBLANES, d), w_gu, w_gu,
      w_down).reshape(2 * m + 2 * tm, d)


def _combine_ln_body(a_ref, b_ref, rt_ref, x_ref, g_ref, bb_ref, *o_refs, alpha, n_prompt):
    ff = rt_ref[:, 2:3] * a_ref[...] + rt_ref[:, 3:4] * b_ref[...]
    res = _layer_norm(alpha * x_ref[...] + ff, g_ref[0], bb_ref[0])
    if n_prompt is None:
        o_refs[0][...] = res
        o_refs[1][...] = res.astype(_BF16)
    else:
        i = pl.program_id(0)
        op_ref, os_ref = o_refs

        @pl.when(i < n_prompt)
        def _():
            op_ref[...] = res.reshape(op_ref.shape)

        @pl.when(i >= n_prompt)
        def _():
            os_ref[...] = res


def _combine_ln(y2, rt, x, ln_g, ln_b, li, *, alpha, prompt_shape=None):
    m, d = x.shape
    tm = min(ROW_TILE, m)
    nt = m // tm
    if prompt_shape is None:
        n_prompt = None
        out_specs = [pl.BlockSpec((tm, d), lambda i: (i, 0)), pl.BlockSpec((tm, d), lambda i: (i, 0))]
        out_shape = [jax.ShapeDtypeStruct((m, d), _F32), jax.ShapeDtypeStruct((m, d), _BF16)]
    else:
        bp, n_prompt, seg = prompt_shape
        assert bp * seg == tm and n_prompt == nt - 1
        out_specs = [pl.BlockSpec((bp, 1, seg, d), lambda i: (0, jnp.minimum(i, n_prompt - 1), 0, 0)),
                     pl.BlockSpec((tm, d), lambda i: (0, 0))]
        out_shape = [jax.ShapeDtypeStruct((bp, n_prompt, seg, d), _F32), jax.ShapeDtypeStruct((tm, d), _F32)]
    return pl.pallas_call(
        functools.partial(_combine_ln_body, alpha=alpha, n_prompt=n_prompt),
        grid=(nt,),
        in_specs=[pl.BlockSpec((tm, d), lambda i: (i, 0)),
                  pl.BlockSpec((tm, d), lambda i: (i + nt, 0)),
                  pl.BlockSpec((tm, LANES), lambda i: (i, 0)),
                  pl.BlockSpec((tm, d), lambda i: (i, 0)),
                  pl.BlockSpec((1, 1, d), lambda i: (li, 0, 0)),
                  pl.BlockSpec((1, 1, d), lambda i: (li, 0, 0))],
        out_specs=out_specs,
        out_shape=out_shape,
        compiler_params=_cparams(1),
        name="combine_ln",
    )(y2, y2, rt, x, ln_g, ln_b)


def _pack_body(xp_ref, xs_ref, o_ref, ob_ref, *, n_prompt):
    i = pl.program_id(0)

    @pl.when(i < n_prompt)
    def _():
        o_ref[...] = xp_ref[...].reshape(o_ref.shape)
        ob_ref[...] = xp_ref[...].reshape(o_ref.shape).astype(_BF16)

    @pl.when(i >= n_prompt)
    def _():
        o_ref[...] = xs_ref[...]
        ob_ref[...] = xs_ref[...].astype(_BF16)


def _pack_tokens(x_prompt4, x_sample2):
    bp, n_prompt, seg, d = x_prompt4.shape
    tm = bp * seg
    assert x_sample2.shape == (tm, d)
    return pl.pallas_call(
        functools.partial(_pack_body, n_prompt=n_prompt),
        grid=(n_prompt + 1,),
        in_specs=[pl.BlockSpec((bp, 1, seg, d), lambda i: (0, jnp.minimum(i, n_prompt - 1), 0, 0)),
                  pl.BlockSpec((tm, d), lambda i: (0, 0))],
        out_specs=[pl.BlockSpec((tm, d), lambda i: (i, 0)), pl.BlockSpec((tm, d), lambda i: (i, 0))],
        out_shape=[jax.ShapeDtypeStruct(((n_prompt + 1) * tm, d), _F32),
                   jax.ShapeDtypeStruct(((n_prompt + 1) * tm, d), _BF16)],
        compiler_params=_cparams(1),
        name="pack_tokens",
    )(x_prompt4, x_sample2)


def _seg_scan(a, b, tstep, n_steps, stride):
    sh = stride
    for s in range(n_steps):
        keep = tstep >= (1 << s)
        a_sh = pltpu.roll(a, sh, 0)
        b_sh = pltpu.roll(b, sh, 0)
        b = jnp.where(keep, a * b_sh + b, b)
        a = jnp.where(keep, a * a_sh, a)
        sh *= 2
    return a, b


def _lru_body(gate_ref, rec_ref, h0_ref, c0_ref, cw_ref, cb_ref, gw_ref, gb_ref, lam_ref,
              y_ref, hout_ref, cout_ref, xp_ref, hc_ref, cv_ref, gwb_ref, *, nseg, seg, stride, pad):
    rows = nseg * seg
    bw = gw_ref.shape[1]

    @pl.when(pl.program_id(0) == 0)
    def _():
        xp_ref[:, 0:pad, :] = c0_ref[...]
        hc_ref[...] = h0_ref[...]
        gwb_ref[...] = gw_ref[...].astype(_BF16)

    for s in range(nseg):
        xp_ref[s, pad:pad + seg, :] = rec_ref[s * seg:(s + 1) * seg, :]
        if stride % SUBLANES == 0:
            taps = [xp_ref[s, pad - k * stride:pad - k * stride + seg, :] for k in range(CONV_W)]
        else:
            hx = xp_ref[s, 0:pad + seg, :]
            taps = [hx[pad:]] + [pltpu.roll(hx, k * stride, 0)[pad:] for k in range(1, CONV_W)]
        acc = taps[CONV_W - 1] * cw_ref[0:1, :]
        for j in range(1, CONV_W):
            acc = acc + taps[CONV_W - 1 - j] * cw_ref[j:j + 1, :]
        cv_ref[s * seg:(s + 1) * seg, :] = acc + cb_ref[...]
        tail = xp_ref[s, seg:seg + pad, :]
        xp_ref[s, 0:pad, :] = tail
        cout_ref[s] = tail

    row = lax.broadcasted_iota(jnp.int32, (rows, bw), 0)
    if stride == 1:
        tstep = row & (seg - 1)
    else:
        tstep = row >> int(math.log2(stride))
    n_steps = int(math.log2(seg // stride))

    for n in range(LRU_BLOCKS):
        cs = slice(n * bw, (n + 1) * bw)
        cv = cv_ref[:, cs]
        gates = _dot(cv.astype(_BF16), gwb_ref[n]) + gb_ref[n]
        r = jax.nn.sigmoid(gates[:, :bw])
        ig = jax.nn.sigmoid(gates[:, bw:])
        log_a = -LRU_C * r * jax.nn.softplus(-lam_ref[:, cs])
        a = jnp.exp(log_a)
        s = -jnp.tanh(log_a) * (a * a + 1.0)
        mult = jnp.where(s > 0.0, s * lax.rsqrt(s), 0.0)
        a_cum, b_cum = _seg_scan(a, mult * ig * cv, tstep, n_steps, stride)
        hc = hc_ref[:, cs]
        if stride == 1:
            hprev = jnp.concatenate([jnp.broadcast_to(hc[s:s + 1, :], (seg, bw)) for s in range(nseg)], axis=0)
        else:
            hprev = jnp.concatenate([hc] * (seg // stride), axis=0)
        h = a_cum * hprev + b_cum
        y_ref[:, cs] = (h * jax.nn.gelu(gate_ref[:, cs])).astype(y_ref.dtype)
        if stride == 1:
            hc_ref[:, cs] = jnp.concatenate([h[(s + 1) * seg - 1:(s + 1) * seg, :] for s in range(nseg)], axis=0)
        else:
            hc_ref[:, cs] = h[rows - stride:, :]

    hout_ref[...] = hc_ref[...]


def _lru_seq(gu, h0, c0, conv_w, conv_b, gate_w, gate_b, lam, *, n_tiles, nseg, seg, stride, pad):
    rows = nseg * seg
    r = gu.shape[1] // 2
    bw = r // LRU_BLOCKS
    const2 = lambda i: (0, 0)
    const3 = lambda i: (0, 0, 0)
    return pl.pallas_call(
        functools.partial(_lru_body, nseg=nseg, seg=seg, stride=stride, pad=pad),
        grid=(n_tiles,),
        in_specs=[pl.BlockSpec((rows, r), lambda i: (i, 0)),
                  pl.BlockSpec((rows, r), lambda i: (i, 1)),
                  pl.BlockSpec((nseg * stride, r), const2),
                  pl.BlockSpec((nseg, pad, r), const3),
                  pl.BlockSpec((CONV_W, r), const2),
                  pl.BlockSpec((1, r), const2),
                  pl.BlockSpec((LRU_BLOCKS, bw, 2 * bw), const3),
                  pl.BlockSpec((LRU_BLOCKS, 1, 2 * bw), const3),
                  pl.BlockSpec((1, r), const2)],
        out_specs=[pl.BlockSpec((rows, r), lambda i: (i, 0)),
                   pl.BlockSpec((nseg * stride, r), const2),
                   pl.BlockSpec((nseg, pad, r), const3)],
        out_shape=[jax.ShapeDtypeStruct((n_tiles * rows, r), _BF16),
                   jax.ShapeDtypeStruct((nseg * stride, r), _F32),
                   jax.ShapeDtypeStruct((nseg, pad, r), _F32)],
        scratch_shapes=[pltpu.VMEM((nseg, pad + seg, r), _F32),
                        pltpu.VMEM((nseg * stride, r), _F32),
                        pltpu.VMEM((rows, r), _F32),
                        pltpu.VMEM((LRU_BLOCKS, bw, 2 * bw), _BF16)],
        compiler_params=_cparams(1),
        name="lru_seq",
    )(gu, gu, h0, c0, conv_w, conv_b, gate_w, gate_b, lam)


def _l2norm(t):
    return t * lax.rsqrt(jnp.sum(t * t, axis=-1, keepdims=True) + RMS_EPS)


def _gdn_chunks(seqs, norm_g, *, c, hg):
    dk = GDN_DK
    qk_w = GDN_HEADS * dk
    r4 = hg * c
    lg_c = int(math.log2(c))
    row = lax.broadcasted_iota(jnp.int32, (r4, r4), 0)
    col = lax.broadcasted_iota(jnp.int32, (r4, r4), 1)
    blk = {1 << s: (row >> s) == (col >> s) for s in range(3, lg_c + 1)}
    incl = blk[c] & (row >= col)
    strict = blk[c] & (row > col)
    units = [(si, list(range(g0, g0 + hg))) for si in range(len(seqs)) for g0 in range(0, GDN_HEADS, hg)]
    blocks = [slice(hh * c, (hh + 1) * c) for hh in range(hg)]

    def stack(fn, hs, axis=0):
        return jnp.concatenate([fn(h) for h in hs], axis=axis)

    tr = lax.broadcasted_iota(jnp.int32, (c, c), 0)
    tc = lax.broadcasted_iota(jnp.int32, (c, c), 1)
    tri = jnp.where(tr >= tc, 1.0, 0.0)
    gc = [_dot(tri, sq[3], precision=_HI) for sq in seqs]
    gct = [jnp.concatenate([x, jnp.zeros((LANES - c, LANES), _F32)], axis=0).T for x in gc]

    q4 = [stack(lambda h: _l2norm(seqs[si][0](h * dk)) * (dk ** -0.5), hs) for si, hs in units]
    k4 = [stack(lambda h: _l2norm(seqs[si][0](qk_w + h * dk)), hs) for si, hs in units]
    v4 = [stack(lambda h: seqs[si][0](2 * qk_w + h * dk), hs) for si, hs in units]
    beta4 = [stack(lambda h: jnp.broadcast_to(seqs[si][2][:, h:h + 1], (c, dk)), hs) for si, hs in units]
    gc4 = [stack(lambda h: jnp.broadcast_to(gc[si][:, h:h + 1], (c, dk)), hs) for si, hs in units]
    gcol = [stack(lambda h: jnp.broadcast_to(gc[si][:, h:h + 1], (c, r4)), hs) for si, hs in units]
    grow = [stack(lambda h: gct[si][h:h + 1, 0:c], hs, axis=1) for si, hs in units]
    decay = [jnp.exp(jnp.where(incl, gcl - grw, -jnp.inf)) for gcl, grw in zip(gcol, grow)]
    kb4 = [k * bt for k, bt in zip(k4, beta4)]
    aq = [_dot_nt(jnp.concatenate([kb, q], axis=0).astype(_BF16), k.astype(_BF16)) for kb, q, k in zip(kb4, q4, k4)]
    a4 = [jnp.where(strict, x[:r4] * dc, 0.0) for x, dc in zip(aq, decay)]
    attn4 = [x[r4:] * dc for x, dc in zip(aq, decay)]
    c0 = min(c, 16)
    np_ = [-jnp.where(blk[c0], x, 0.0) for x in a4]
    e = list(np_)
    for _ in range(int(math.log2(c0)) - 1):
        np_ = [_bdot(x, x) for x in np_]
        e = [ei + ni + _bdot(ni, ei) for ei, ni in zip(e, np_)]
    size = c0
    while size < c:
        lo = [jnp.where(blk[2 * size] & ~blk[size], x, 0.0) for x in a4]
        m = [li + _bdot(li, ei) for li, ei in zip(lo, e)]
        e = [ei - (mi + _bdot(ei, mi)) for ei, mi in zip(e, m)]
        size *= 2
    egc4 = [jnp.exp(x) for x in gc4]
    rhs = [jnp.concatenate([v * bt, kb * eg], axis=1) for v, bt, kb, eg in zip(v4, beta4, kb4, egc4)]
    uw = [r + _bdot(ei, r) for r, ei in zip(rhs, e)]
    qg4 = [q * eg for q, eg in zip(q4, egc4)]
    ws_qs = [[_dot(jnp.concatenate([uw[u][rs, dk:], qg4[u][rs]], axis=0).astype(_BF16),
                   seqs[si][4][h].astype(_BF16)) for rs, h in zip(blocks, hs)]
             for u, (si, hs) in enumerate(units)]
    vn = [[uw[u][rs, :dk] - ws_qs[u][hh][:c] for hh, rs in enumerate(blocks)] for u in range(len(units))]
    o4 = [jnp.concatenate([x[c:] for x in ws_qs[u]], axis=0)
          + _bdot(attn4[u], jnp.concatenate(vn[u], axis=0)) for u in range(len(units))]
    outs = [[None] * GDN_HEADS for _ in seqs]
    new_state = [[None] * GDN_HEADS for _ in seqs]
    for u, (si, hs) in enumerate(units):
        z, state = seqs[si][1], seqs[si][4]
        for hh, (rs, h) in enumerate(zip(blocks, hs)):
            glast = gc[si][c - 1:c, h:h + 1]
            kd = k4[u][rs] * jnp.exp(glast - gc4[u][rs])
            new_state[si][h] = state[h] * jnp.exp(glast) + _dot_tn(kd.astype(_BF16), vn[u][hh].astype(_BF16))
            oh = o4[u][rs]
            oh = oh * lax.rsqrt(jnp.mean(oh * oh, axis=-1, keepdims=True) + RMS_EPS) * norm_g
            outs[si][h] = oh * jax.nn.silu(z[:, h * dk:(h + 1) * dk])
    return outs, new_state


def _gdn_gates(ba, alog, dtb):
    beta = jax.nn.sigmoid(ba[:, :LANES])
    g = -jnp.exp(alog) * jax.nn.softplus(ba[:, LANES:] + dtb)
    return beta, g


def _gdn_conv(xp_ref, s, x, cw_ref, *, pad, seg):
    xp_ref[s, pad:pad + seg, :] = x

    def block(col):
        cs = slice(col, col + GDN_DK)
        hx = xp_ref[s, pl.ds(0, pad + seg), cs]
        acc = pltpu.roll(hx, CONV_W - 1, 0)[pad:] * cw_ref[0:1, cs]
        for j in range(1, CONV_W - 1):
            acc = acc + pltpu.roll(hx, CONV_W - 1 - j, 0)[pad:] * cw_ref[j:j + 1, cs]
        return jax.nn.silu(acc + hx[pad:] * cw_ref[CONV_W - 1:CONV_W, cs])

    return block


def _gdn_prompt_body(qkv_ref, z_ref, ba_ref, cw_ref, alog_ref, dtb_ref, ng_ref,
                     o_ref, sout_ref, cout_ref, xp_ref, s_ref, *, nseg, seg, pad):
    @pl.when(pl.program_id(0) == 0)
    def _():
        xp_ref[...] = jnp.zeros(xp_ref.shape, _F32)
        s_ref[...] = jnp.zeros(s_ref.shape, _F32)

    def per_group(i, carry):
        bs = [i * GDN_SEQS_PER_STEP + k for k in range(GDN_SEQS_PER_STEP)]
        r0 = [pl.multiple_of(b * seg, seg) for b in bs]
        seqs = []
        for b, r in zip(bs, r0):
            qkv = _gdn_conv(xp_ref, b, qkv_ref[pl.ds(r, seg), :], cw_ref, pad=pad, seg=seg)
            beta, g = _gdn_gates(ba_ref[pl.ds(r, seg), :], alog_ref[...], dtb_ref[...])
            seqs.append((qkv, z_ref[pl.ds(r, seg), :], beta, g, [s_ref[b, h] for h in range(GDN_HEADS)]))
        outs, new_state = _gdn_chunks(seqs, ng_ref[...], c=seg, hg=GDN_STACK_ROWS // seg)
        for k, (b, r) in enumerate(zip(bs, r0)):
            tail = xp_ref[b, pl.ds(seg, pad), :]
            xp_ref[b, 0:pad, :] = tail
            cout_ref[b] = tail
            for h in range(GDN_HEADS):
                s_ref[b, h] = new_state[k][h]
                o_ref[pl.ds(r, seg), h * GDN_DV:(h + 1) * GDN_DV] = outs[k][h].astype(o_ref.dtype)
        return carry

    lax.fori_loop(0, nseg // GDN_SEQS_PER_STEP, per_group, 0)

    @pl.when(pl.program_id(0) == pl.num_programs(0) - 1)
    def _():
        sout_ref[...] = s_ref[...]


def _gdn_prompt(proj, ba, conv_w, alog, dtb, norm_g, *, n_tiles, nseg, seg):
    rows = nseg * seg
    cd = conv_w.shape[1]
    vw = GDN_HEADS * GDN_DV
    pad = 8
    const2 = lambda i: (0, 0)
    return pl.pallas_call(
        functools.partial(_gdn_prompt_body, nseg=nseg, seg=seg, pad=pad),
        grid=(n_tiles,),
        in_specs=[pl.BlockSpec((rows, cd), lambda i: (i, 0)),
                  pl.BlockSpec((rows, vw), lambda i: (i, cd // vw)),
                  pl.BlockSpec((rows, 2 * LANES), lambda i: (i, 0)),
                  pl.BlockSpec((CONV_W, cd), const2),
                  pl.BlockSpec((1, LANES), const2),
                  pl.BlockSpec((1, LANES), const2),
                  pl.BlockSpec((1, GDN_DV), const2)],
        out_specs=[pl.BlockSpec((rows, vw), lambda i: (i, 0)),
                   pl.BlockSpec((nseg, GDN_HEADS, GDN_DK, GDN_DV), lambda i: (0, 0, 0, 0)),
                   pl.BlockSpec((nseg, pad, cd), lambda i: (0, 0, 0))],
        out_shape=[jax.ShapeDtypeStruct((n_tiles * rows, vw), _BF16),
                   jax.ShapeDtypeStruct((nseg, GDN_HEADS, GDN_DK, GDN_DV), _F32),
                   jax.ShapeDtypeStruct((nseg, pad, cd), _F32)],
        scratch_shapes=[pltpu.VMEM((nseg, pad + seg, cd), _F32),
                        pltpu.VMEM((nseg, GDN_HEADS, GDN_DK, GDN_DV), _F32)],
        compiler_params=_cparams(1),
        name="gdn_prompt",
    )(proj, proj, ba, conv_w, alog, dtb, norm_g)


def _gdn_sample_body(qkv_ref, z_ref, ba_ref, s0_ref, c0_ref, cw_ref, alog_ref, dtb_ref, ng_ref, sall_ref,
                     o_ref, sout_ref, cout_ref, xp_ref, *, nb, seg, steps, pad):
    del sall_ref
    s0_ref, sout_ref = s0_ref.at[0], sout_ref.at[0]
    def per_group(i, carry):
        bs = [i * GDN_SEQS_PER_STEP + k for k in range(GDN_SEQS_PER_STEP)]
        seqs = []
        for k, b in enumerate(bs):
            xp_ref[k, 0:pad, :] = c0_ref[b]
            qkv = _gdn_conv(xp_ref, k, qkv_ref[b], cw_ref, pad=pad, seg=seg)
            cout_ref[b] = xp_ref[k, pl.ds(steps, pad), :]
            beta, g = _gdn_gates(ba_ref[b], alog_ref[...], dtb_ref[...])
            live = lax.broadcasted_iota(jnp.int32, beta.shape, 0) < steps
            seqs.append((qkv, z_ref[b], jnp.where(live, beta, 0.0), jnp.where(live, g, 0.0),
                         [s0_ref[b, h] for h in range(GDN_HEADS)]))
        outs, new_state = _gdn_chunks(seqs, ng_ref[...], c=seg, hg=GDN_HEADS)
        for k, b in enumerate(bs):
            for h in range(GDN_HEADS):
                sout_ref[b, h] = new_state[k][h]
                o_ref[b, :, h * GDN_DV:(h + 1) * GDN_DV] = outs[k][h]
        return carry

    lax.fori_loop(0, nb // GDN_SEQS_PER_STEP, per_group, 0)


def _gdn_sample(proj, ba, s0_all, layer, s_new_all, c0, conv_w, alog, dtb, norm_g, *, steps):
    nbatch, seg, _ = proj.shape
    cd = conv_w.shape[1]
    vw = GDN_HEADS * GDN_DV
    pad = c0.shape[1]
    nb = 8
    const2 = lambda i: (0, 0)
    return pl.pallas_call(
        functools.partial(_gdn_sample_body, nb=nb, seg=seg, steps=steps, pad=pad),
        grid=(nbatch // nb,),
        in_specs=[pl.BlockSpec((nb, seg, cd), lambda i: (i, 0, 0)),
                  pl.BlockSpec((nb, seg, vw), lambda i: (i, 0, cd // vw)),
                  pl.BlockSpec((nb, seg, 2 * LANES), lambda i: (i, 0, 0)),
                  pl.BlockSpec((1, nb, GDN_HEADS, GDN_DK, GDN_DV), lambda i: (layer, i, 0, 0, 0)),
                  pl.BlockSpec((nb, pad, cd), lambda i: (i, 0, 0)),
                  pl.BlockSpec((CONV_W, cd), const2),
                  pl.BlockSpec((1, LANES), const2),
                  pl.BlockSpec((1, LANES), const2),
                  pl.BlockSpec((1, GDN_DV), const2),
                  pl.BlockSpec(memory_space=pl.ANY)],
        out_specs=[pl.BlockSpec((nb, seg, vw), lambda i: (i, 0, 0)),
                   pl.BlockSpec((1, nb, GDN_HEADS, GDN_DK, GDN_DV), lambda i: (layer, i, 0, 0, 0)),
                   pl.BlockSpec((nb, pad, cd), lambda i: (i, 0, 0))],
        out_shape=[jax.ShapeDtypeStruct((nbatch, seg, vw), _F32),
                   jax.ShapeDtypeStruct(s_new_all.shape, _F32),
                   jax.ShapeDtypeStruct((nbatch, pad, cd), _F32)],
        scratch_shapes=[pltpu.VMEM((GDN_SEQS_PER_STEP, pad + seg, cd), _F32)],
        input_output_aliases={9: 1},
        compiler_params=_cparams(1),
        name="gdn_sample",
    )(proj, proj, ba, s0_all, c0, conv_w, alog, dtb, norm_g, s_new_all)


def _softmax_rows(s):
    e = jnp.exp(s - jnp.max(s, axis=-1, keepdims=True))
    return e / jnp.sum(e, axis=-1, keepdims=True)


def _attn_prompt_body(q_ref, k_ref, v_ref, o_ref, *, nseg, seg, ntile):
    hd = q_ref.shape[1] // XA_HEADS
    scale = hd ** -0.5
    for b in range(nseg):
        runs = [slice((t * nseg + b) * seg, (t * nseg + b + 1) * seg) for t in range(ntile)]
        heads = [slice(h * hd, (h + 1) * hd) for h in range(XA_HEADS)]
        s = [_dot_nt(jnp.concatenate([q_ref[rs, cs] for rs in runs], axis=0), k_ref[b, :, cs]) * scale
             for cs in heads]
        p = [_softmax_rows(sh).astype(_BF16) for sh in s]
        o = [_dot(ph, v_ref[b, :, cs]).astype(o_ref.dtype) for ph, cs in zip(p, heads)]
        for oh, cs in zip(o, heads):
            for t, rs in enumerate(runs):
                o_ref[rs, cs] = oh[t * seg:(t + 1) * seg]


def _attn_prompt(q, mem_k, mem_v, *, n_tiles, nseg, seg):
    ntile = math.gcd(n_tiles, 4)
    rows = nseg * seg * ntile
    d = q.shape[1]
    mem = mem_k.shape[1]
    return pl.pallas_call(
        functools.partial(_attn_prompt_body, nseg=nseg, seg=seg, ntile=ntile),
        grid=(n_tiles // ntile,),
        in_specs=[pl.BlockSpec((rows, d), lambda i: (i, 0)),
                  pl.BlockSpec((nseg, mem, d), lambda i: (0, 0, 0)),
                  pl.BlockSpec((nseg, mem, d), lambda i: (0, 0, 0))],
        out_specs=pl.BlockSpec((rows, d), lambda i: (i, 0)),
        out_shape=jax.ShapeDtypeStruct((n_tiles * nseg * seg, d), _BF16),
        compiler_params=_cparams(1),
        name="attn_prompt",
    )(q, mem_k, mem_v)


def _attn_sample_body(q_ref, k_ref, v_ref, o_ref, *, nb):
    t8 = q_ref.shape[1]
    mem, nh, hd = k_ref.shape[2:]
    scale = hd ** -0.5
    qh = lax.broadcasted_iota(jnp.int32, (nh * t8, mem * nh), 0) // t8
    kh = lax.broadcasted_iota(jnp.int32, (nh * t8, mem * nh), 1) % nh
    own = qh == kh

    def per_group(i, carry):
        bs = [i * ATTN_SEQS_PER_STEP + k for k in range(ATTN_SEQS_PER_STEP)]
        q2 = [jnp.concatenate([q_ref[b][:, h * hd:(h + 1) * hd] for h in range(nh)], axis=0).astype(_BF16)
              for b in bs]
        s = [jnp.where(own, _dot_nt(qq, k_ref[0, b].reshape(mem * nh, hd).astype(_BF16)) * scale, -jnp.inf)
             for qq, b in zip(q2, bs)]
        p = [_softmax_rows(x).astype(_BF16) for x in s]
        o2 = [_dot(pp, v_ref[0, b].reshape(mem * nh, hd).astype(_BF16)) for pp, b in zip(p, bs)]
        for oo, b in zip(o2, bs):
            for h in range(nh):
                o_ref[b, :, h * hd:(h + 1) * hd] = oo[h * t8:(h + 1) * t8]
        return carry

    lax.fori_loop(0, nb // ATTN_SEQS_PER_STEP, per_group, 0)


def _attn_sample(q, cache_k, cache_v, layer):
    nbatch, t8, d = q.shape
    mem, nh, hd = cache_k.shape[2:]
    nb = 8
    return pl.pallas_call(
        functools.partial(_attn_sample_body, nb=nb),
        grid=(nbatch // nb,),
        in_specs=[pl.BlockSpec((nb, t8, d), lambda i: (i, 0, 0)),
                  pl.BlockSpec((1, nb, mem, nh, hd), lambda i: (layer, i, 0, 0, 0)),
                  pl.BlockSpec((1, nb, mem, nh, hd), lambda i: (layer, i, 0, 0, 0))],
        out_specs=pl.BlockSpec((nb, t8, d), lambda i: (i, 0, 0)),
        out_shape=jax.ShapeDtypeStruct(q.shape, _F32),
        compiler_params=_cparams(1),
        name="attn_sample",
    )(q, cache_k, cache_v)


def kernel(x_prompt, x_sample, mem_prompt, state_lru_h, state_lru_conv, state_gdn_s, state_gdn_conv,
           cache_mem_k, cache_mem_v, ln_g, ln_b, xa_wq, xa_wk, xa_wv, xa_wo,
           lru_w_in, lru_conv_w, lru_conv_b, lru_gate_w, lru_gate_b, lru_lambda, lru_w_out,
           gdn_w_in, gdn_conv_w, gdn_a_log, gdn_dt_bias, gdn_norm_g, gdn_w_out,
           ffn_w_gu, ffn_w_down, moe_router, moe_w_gu, moe_w_down):
    depth = ln_g.shape[0]
    bp, seq, d = x_prompt.shape
    bs, ts, _ = x_sample.shape
    mem = mem_prompt.shape[1]
    alpha = (2 * depth) ** 0.25
    seg = GDN_CHUNK
    nchunk = seq // seg
    mp, ms = bp * seq, bs * ts
    assert bp * seg == ROW_TILE and ms == ROW_TILE and seq % seg == 0 and ts >= CONV_W - 1
    assert depth % 2 == 0
    t8 = 8
    pad_p = 8
    pad_s = (CONV_W - 1) * bs
    r = lru_w_in.shape[2] // 2
    cd = gdn_conv_w.shape[2]
    vw = GDN_HEADS * GDN_DV
    d_ff = ffn_w_gu.shape[2] // 2
    d_fe = moe_w_gu.shape[3] // 2

    x, xb = _pack_tokens(x_prompt.reshape(bp, nchunk, seg, d), x_sample.reshape(ms, d))

    memx = mem_prompt.reshape(bp * mem, d)
    mem_k_p = _mm(memx, xa_wk, ng=depth, tn=1024, out_dtype=_F32).reshape(depth, bp, mem, d)
    mem_v_p = _mm(memx, xa_wv, ng=depth, tn=1024, out_dtype=_F32).reshape(depth, bp, mem, d)
    mem_k_b = mem_k_p.astype(_BF16)
    mem_v_b = mem_v_p.astype(_BF16)

    ln_g3 = ln_g.reshape(depth * 3, 1, d)
    moe_w_gu3 = moe_w_gu.reshape((-1,) + moe_w_gu.shape[2:])
    moe_w_down3 = moe_w_down.reshape((-1,) + moe_w_down.shape[2:])
    assert mp + ms < (1 << IDX_SHIFT)
    ln_b3 = ln_b.reshape(depth * 3, 1, d)

    def to_time_major(a):
        return a.reshape(bs, ts, -1).transpose(1, 0, 2).reshape(ms, -1)

    def to_batch_major(a):
        return a.reshape(ts, bs, -1).transpose(1, 0, 2).reshape(ms, -1)

    def pad_t(a):
        a = a.reshape(bs, ts, -1)
        return jnp.pad(a, ((0, 0), (0, t8 - ts), (0, 0)))

    lru_h_p, lru_c_p, lru_h_s, lru_c_s = [], [], [], []
    gdn_s_p, gdn_c_p, gdn_c_s = [], [], []
    gdn_s_s = jnp.zeros(state_gdn_s.shape, _F32)

    for i in range(depth):
        j = i // 2
        if i % 2 == 0:
            gu = _mm(xb, lru_w_in, g0=j, tn=1024, out_dtype=_F32)[0]
            lam = lru_lambda[j].reshape(1, r)
            cb = lru_conv_b[j].reshape(1, r)
            gb = lru_gate_b[j].reshape(LRU_BLOCKS, 1, -1)
            y, h_p, c_p = _lru_seq(gu, jnp.zeros((bp, r), _F32), jnp.zeros((bp, pad_p, r), _F32),
                                   lru_conv_w[j], cb, lru_gate_w[j], gb, lam,
                                   n_tiles=nchunk, nseg=bp, seg=seg, stride=1, pad=pad_p)
            gu_s = to_time_major(gu[mp:])
            c0_s = state_lru_conv[j].transpose(1, 0, 2).reshape(1, pad_s, r)
            y_s, h_s, c_s = _lru_seq(gu_s, state_lru_h[j], c0_s, lru_conv_w[j], cb, lru_gate_w[j], gb, lam,
                                     n_tiles=1, nseg=1, seg=ms, stride=bs, pad=pad_s)
            y_s = to_batch_major(y_s)
            lru_h_p.append(h_p)
            lru_c_p.append(c_p[:, pad_p - (CONV_W - 1):, :])
            lru_h_s.append(h_s)
            lru_c_s.append(c_s.reshape(CONV_W - 1, bs, r).transpose(1, 0, 2))
            w_out, wi = lru_w_out, j
        else:
            proj = _mm(xb, gdn_w_in, g0=j, n=cd + vw, tn=1024, out_dtype=_F32)[0]
            w_tail = gdn_w_in[j, :, cd + vw:]
            w_ba = jnp.zeros((1, d, 2 * LANES), _F32)
            w_ba = w_ba.at[0, :, :GDN_HEADS].set(w_tail[:, :GDN_HEADS])
            w_ba = w_ba.at[0, :, LANES:LANES + GDN_HEADS].set(w_tail[:, GDN_HEADS:])
            ba = _mm(xb, w_ba, tn=2 * LANES, out_dtype=_F32)[0]
            alog = jnp.zeros((1, LANES), _F32).at[0, :GDN_HEADS].set(gdn_a_log[j])
            dtb = jnp.zeros((1, LANES), _F32).at[0, :GDN_HEADS].set(gdn_dt_bias[j])
            ng = gdn_norm_g[j].reshape(1, GDN_DV)
            y, s_p, c_p = _gdn_prompt(proj, ba, gdn_conv_w[j], alog, dtb, ng, n_tiles=nchunk, nseg=bp, seg=seg)
            c0_s = jnp.pad(state_gdn_conv[j], ((0, 0), (pad_p - (CONV_W - 1), 0), (0, 0)))
            y_s, gdn_s_s, c_s = _gdn_sample(pad_t(proj[mp:]), pad_t(ba[mp:]), state_gdn_s, j, gdn_s_s, c0_s,
                                            gdn_conv_w[j], alog, dtb, ng, steps=ts)
            y_s = y_s[:, :ts, :].reshape(ms, vw)
            gdn_s_p.append(s_p)
            gdn_c_p.append(c_p[:, pad_p - (CONV_W - 1):, :])
            gdn_c_s.append(c_s[:, pad_p - (CONV_W - 1):, :])
            w_out, wi = gdn_w_out, j
        x, xb, q = _mm_ln(y, y_s, w_out, wi, x, ln_g3, ln_b3, 3 * i, alpha=alpha, proj=(xa_wq, i))

        o = _attn_prompt(q, mem_k_b[i], mem_v_b[i], n_tiles=nchunk, nseg=bp, seg=seg)
        o_s = _attn_sample(pad_t(q[mp:].astype(_F32)), cache_mem_k, cache_mem_v, i)[:, :ts, :].reshape(ms, d)

        if i % 2 == 0:
            x, xb = _mm_ln(o, o_s, xa_wo, i, x, ln_g3, ln_b3, 3 * i + 1, alpha=alpha)
            hdn = _gate_up(xb, ffn_w_gu, g0=j, ng=1, f=d_ff, tf=512)[0]
            x, xb = _mm_ln(hdn, None, ffn_w_down, j, x, ln_g3, ln_b3, 3 * i + 2, alpha=alpha)
        else:
            router_pad = jnp.zeros((d, LANES), _F32).at[:, :N_EXPERTS].set(moe_router[j])
            x, xb, rt = _mm_ln(o, o_s, xa_wo, i, x, ln_g3, ln_b3, 3 * i + 1, alpha=alpha, route=router_pad)
            y2 = _moe_routed(x, _route_plan(rt, ROW_TILE), moe_w_gu3, moe_w_down3, j * N_EXPERTS, tm=ROW_TILE)
            if i < depth - 1:
                x, xb = _combine_ln(y2, rt, x, ln_g3, ln_b3, 3 * i + 2, alpha=alpha)
            else:
                y_prompt, y_sample = _combine_ln(y2, rt, x, ln_g3, ln_b3, 3 * i + 2, alpha=alpha,
                                                 prompt_shape=(bp, nchunk, seg))

    y_prompt = y_prompt.reshape(bp, seq, d)
    y_sample = y_sample.reshape(bs, ts, d)
    hd = d // XA_HEADS
    return (y_prompt, y_sample,
            jnp.stack(lru_h_p), jnp.stack(lru_c_p), jnp.stack(gdn_s_p), jnp.stack(gdn_c_p),
            mem_k_p.reshape(depth, bp, mem, XA_HEADS, hd), mem_v_p.reshape(depth, bp, mem, XA_HEADS, hd),
            jnp.stack(lru_h_s), jnp.stack(lru_c_s), gdn_s_s, jnp.stack(gdn_c_s))
```
